```python
import math
import jax, jax.numpy as jnp
from jax import lax
import numpy as np

D_MODEL = 2048
BATCH = 32
SEQ = 256
DEPTH = 1
DEC_BATCH = 2
DEC_SEQ = 4096
PAST_LEN = 256

GRID_W = 64
D_HYENA = 1024
HYENA_ORDER = 2
FILTER_EMB = 33
FILTER_HIDDEN = 64
DECAY_FAST = 0.3
DECAY_SLOW = 1.5
DECAY_TARGET = 1e-2
MAX_DECAY = math.log(DECAY_TARGET) / DECAY_FAST
MIN_DECAY = math.log(DECAY_TARGET) / DECAY_SLOW
D_SSM = 1024
SSM_GROUP = 16
N_GROUPS = D_SSM // SSM_GROUP
SSM_STATE = 64
DT_MIN = 1e-3
DT_MAX = 1e-1
LAMBDA_RE_MAX = -1e-4
D_FF = 5632
N_IN = 3 * D_HYENA + D_SSM + 2 * D_MODEL
EPS = 1e-6

kernel_name = 'hyena_s5_gated_hybrid_dit_step'

F32 = jnp.float32


def _rmsnorm(x, g):
    xf = x.astype(F32)
    y = xf * lax.rsqrt(jnp.mean(xf * xf, axis=-1, keepdims=True) + EPS) * g.astype(F32)
    return y.astype(x.dtype)


def _short_conv(x, w, b):
    xp = jnp.pad(x, ((0, 0), (1, 1), (0, 0)))
    return xp[:, :-2] * w[0] + xp[:, 1:-1] * w[1] + xp[:, 2:] * w[2] + b


def _dwconv3x3(x, w, b, rows, cols):
    bsz, n, ch = x.shape
    xg = jnp.pad(x.reshape(bsz, rows, cols, ch), ((0, 0), (1, 1), (1, 1), (0, 0)))
    y = b
    for i in range(3):
        for j in range(3):
            y = y + xg[:, i:i + rows, j:j + cols] * w[i, j]
    return y.reshape(bsz, n, ch)


def _hyena_filters(L, w1, b1, w2, b2, w3, b3, w4, freq):
    pos = jnp.arange(L, dtype=F32)
    t = jnp.linspace(0.0, 1.0, L, dtype=F32)
    bands = (FILTER_EMB - 1) // 2
    fr = jnp.linspace(1e-4, bands - 1, bands, dtype=F32)
    ang = (2.0 * math.pi / L) * pos[:, None] * fr[None, :]
    z = jnp.concatenate([t[:, None], jnp.cos(ang), -jnp.sin(ang)], axis=-1)
    fq = freq.astype(F32)
    h = jnp.sin(fq * (z @ w1.astype(F32) + b1.astype(F32)))
    h = jnp.sin(fq * (h @ w2.astype(F32) + b2.astype(F32)))
    h = jnp.sin(fq * (h @ w3.astype(F32) + b3.astype(F32)))
    h = (h @ w4.astype(F32)).reshape(L, HYENA_ORDER, 2, D_HYENA)
    deltas = jnp.abs(jnp.linspace(MIN_DECAY, MAX_DECAY, D_HYENA, dtype=F32))
    decay = jnp.exp(-t[:, None] * deltas[None, :])
    h = h * decay[:, None, None, :]
    fwd, bwd = h[:, :, 0], h[:, :, 1]
    k = jnp.concatenate([fwd, jnp.zeros((1, HYENA_ORDER, D_HYENA), F32), bwd[:0:-1]], axis=0)
    k = k / jnp.sum(jnp.abs(k), axis=0, keepdims=True)
    return jnp.fft.rfft(k, axis=0)


def _hyena_mixer(u, p):
    L = u.shape[1]
    uc = _short_conv(u, p['hy_short_w'], p['hy_short_b']).astype(F32)
    x1, x2, v = jnp.split(uc, 3, axis=-1)
    k_f = _hyena_filters(L, p['hy_f_w1'], p['hy_f_b1'], p['hy_f_w2'], p['hy_f_b2'],
                         p['hy_f_w3'], p['hy_f_b3'], p['hy_f_w4'], p['hy_f_freq'])
    bias = p['hy_bias'].astype(F32)
    z = v
    for o, gate in enumerate((x1, x2)):
        z_f = jnp.fft.rfft(z, n=2 * L, axis=1)
        conv = jnp.fft.irfft(z_f * k_f[None, :, o], n=2 * L, axis=1)[:, :L]
        z = gate * (conv + bias[o] * z)
    return z.astype(u.dtype)


def _scan_combine(left, right):
    a1, b1 = left
    a2, b2 = right
    return a1 * a2, a2 * b1 + b2


def _s5_mixer(u, s0, p):
    bsz, L, _ = u.shape
    uf = u.astype(F32)
    uc = uf.reshape(bsz, L, N_GROUPS, SSM_GROUP).astype(jnp.complex64)
    lam = lax.complex(jnp.minimum(p['ssm_lambda_re'].astype(F32), LAMBDA_RE_MAX),
                      p['ssm_lambda_im'].astype(F32))
    dt = jnp.exp(p['ssm_log_dt'].astype(F32))[..., None]
    lam_bar = jnp.exp(lam * dt)
    b_c = lax.complex(p['ssm_b_re'].astype(F32), p['ssm_b_im'].astype(F32))
    b_bar = ((lam_bar - 1.0) / lam)[..., None] * b_c
    c_c = lax.complex(p['ssm_c_re'].astype(F32), p['ssm_c_im'].astype(F32))
    y = uf * p['ssm_d'].astype(F32)
    finals = []
    for direction in range(2):
        reverse = direction == 1
        bu = jnp.einsum('blgc,gpc->blgp', uc, b_bar[direction])
        if s0 is not None:
            edge = -1 if reverse else 0
            bu = bu.at[:, edge].add(lam_bar[direction] * s0[:, direction])
        a = jnp.broadcast_to(lam_bar[direction], bu.shape)
        _, s = lax.associative_scan(_scan_combine, (a, bu), reverse=reverse, axis=1)
        y = y + jnp.real(jnp.einsum('blgp,gcp->blgc', s, c_c[direction])).reshape(bsz, L, D_SSM)
        finals.append(s[:, 0] if reverse else s[:, -1])
    g = jax.nn.gelu(y)
    out = g * jax.nn.sigmoid(g @ p['ssm_glu_w'].astype(F32) + p['ssm_glu_b'].astype(F32))
    return out.astype(u.dtype), jnp.stack(finals, axis=1)


def _layer(x, cond, s0, rows, cols, p):
    mod = jax.nn.silu(cond) @ p['ada_w'] + p['ada_b']
    sh1, sc1, g1, sh2, sc2, g2 = jnp.split(mod[:, None, :], 6, axis=-1)
    h = _rmsnorm(x, p['norm1_g']) * (1.0 + sc1) + sh1
    proj = h @ p['w_in'] + p['b_in']
    s_a = 3 * D_HYENA
    s_b = s_a + D_SSM
    s_c = s_b + D_MODEL
    hy_in, ssm_in, gate_h, gate_s = jnp.split(proj, [s_a, s_b, s_c], axis=-1)
    y_h = _hyena_mixer(hy_in, p)
    y_s, st = _s5_mixer(ssm_in, s0, p)
    merged = (jax.nn.sigmoid(gate_h) * (y_h @ p['w_branch_h'])
              + jax.nn.sigmoid(gate_s) * (y_s @ p['w_branch_s']))
    x = x + g1 * (merged @ p['w_out'])
    h = _rmsnorm(x, p['norm2_g']) * (1.0 + sc2) + sh2
    up = _dwconv3x3(h @ p['ffn_up'], p['ffn_conv_w'], p['ffn_conv_b'], rows, cols)
    a, b = jnp.split(up, 2, axis=-1)
    x = x + g2 * ((jax.nn.gelu(a) * b) @ p['ffn_down'])
    return x, st


def setup_inputs(seed: int = 0) -> dict:
    key = jax.random.key(seed)
    keys = iter(jax.random.split(key, 64))

    def nrm(shape, scale):
        return jax.random.normal(next(keys), shape, F32) * scale

    ssm_shape = (DEPTH, 2, N_GROUPS, SSM_STATE)
    lam_im = jnp.broadcast_to(math.pi * jnp.arange(SSM_STATE, dtype=F32), ssm_shape)
    return {
        'x_prompt': nrm((BATCH, SEQ, D_MODEL), 1.0),
        'x_sample': nrm((DEC_BATCH, DEC_SEQ, D_MODEL), 1.0),
        'state_ssm_re': nrm((DEC_BATCH, DEPTH, 2, N_GROUPS, SSM_STATE), 0.5),
        'state_ssm_im': nrm((DEC_BATCH, DEPTH, 2, N_GROUPS, SSM_STATE), 0.5),
        'c': nrm((DEC_BATCH, D_MODEL), 1.0),
        'c_ctx': nrm((D_MODEL,), 1.0),
        'ada_w': nrm((DEPTH, D_MODEL, 6 * D_MODEL), 0.5 * D_MODEL ** -0.5),
        'ada_b': nrm((DEPTH, 6 * D_MODEL), 0.02),
        'norm1_g': 1.0 + nrm((DEPTH, D_MODEL), 0.02),
        'norm2_g': 1.0 + nrm((DEPTH, D_MODEL), 0.02),
        'final_g': 1.0 + nrm((D_MODEL,), 0.02),
        'w_in': nrm((DEPTH, D_MODEL, N_IN), D_MODEL ** -0.5),
        'b_in': nrm((DEPTH, N_IN), 0.02),
        'hy_short_w': nrm((DEPTH, 3, 3 * D_HYENA), 3 ** -0.5),
        'hy_short_b': nrm((DEPTH, 3 * D_HYENA), 0.02),
        'hy_f_w1': nrm((DEPTH, FILTER_EMB, FILTER_HIDDEN), FILTER_EMB ** -0.5),
        'hy_f_b1': nrm((DEPTH, FILTER_HIDDEN), 0.1),
        'hy_f_w2': nrm((DEPTH, FILTER_HIDDEN, FILTER_HIDDEN), FILTER_HIDDEN ** -0.5),
        'hy_f_b2': nrm((DEPTH, FILTER_HIDDEN), 0.1),
        'hy_f_w3': nrm((DEPTH, FILTER_HIDDEN, FILTER_HIDDEN), FILTER_HIDDEN ** -0.5),
        'hy_f_b3': nrm((DEPTH, FILTER_HIDDEN), 0.1),
        'hy_f_w4': nrm((DEPTH, FILTER_HIDDEN, HYENA_ORDER * 2 * D_HYENA), FILTER_HIDDEN ** -0.5),
        'hy_f_freq': 1.0 + nrm((DEPTH, FILTER_HIDDEN), 0.01),
        'hy_bias': nrm((DEPTH, HYENA_ORDER, D_HYENA), 1.0),
        'w_branch_h': nrm((DEPTH, D_HYENA, D_MODEL), D_HYENA ** -0.5),
        'ssm_lambda_re': -0.5 + nrm(ssm_shape, 0.01),
        'ssm_lambda_im': lam_im + nrm(ssm_shape, 0.01),
        'ssm_log_dt': jax.random.uniform(next(keys), (DEPTH, 2, N_GROUPS), F32,
                                         math.log(DT_MIN), math.log(DT_MAX)),
        'ssm_b_re': nrm((DEPTH, 2, N_GROUPS, SSM_STATE, SSM_GROUP), (2 * SSM_GROUP) ** -0.5),
        'ssm_b_im': nrm((DEPTH, 2, N_GROUPS, SSM_STATE, SSM_GROUP), (2 * SSM_GROUP) ** -0.5),
        'ssm_c_re': nrm((DEPTH, 2, N_GROUPS, SSM_GROUP, SSM_STATE), (2 * SSM_STATE) ** -0.5),
        'ssm_c_im': nrm((DEPTH, 2, N_GROUPS, SSM_GROUP, SSM_STATE), (2 * SSM_STATE) ** -0.5),
        'ssm_d': nrm((DEPTH, D_SSM), 1.0),
        'ssm_glu_w': nrm((DEPTH, D_SSM, D_SSM), D_SSM ** -0.5),
        'ssm_glu_b': nrm((DEPTH, D_SSM), 0.02),
        'w_branch_s': nrm((DEPTH, D_SSM, D_MODEL), D_SSM ** -0.5),
        'w_out': nrm((DEPTH, D_MODEL, D_MODEL), D_MODEL ** -0.5),
        'ffn_up': nrm((DEPTH, D_MODEL, 2 * D_FF), D_MODEL ** -0.5),
        'ffn_conv_w': nrm((DEPTH, 3, 3, 2 * D_FF), 1.0 / 3.0),
        'ffn_conv_b': nrm((DEPTH, 2 * D_FF), 0.02),
        'ffn_down': nrm((DEPTH, D_FF, D_MODEL), D_FF ** -0.5),
    }


def reference(x_prompt, x_sample, state_ssm_re, state_ssm_im, c, c_ctx,
              ada_w, ada_b, norm1_g, norm2_g, final_g, w_in, b_in,
              hy_short_w, hy_short_b, hy_f_w1, hy_f_b1, hy_f_w2, hy_f_b2, hy_f_w3, hy_f_b3,
              hy_f_w4, hy_f_freq, hy_bias, w_branch_h,
              ssm_lambda_re, ssm_lambda_im, ssm_log_dt, ssm_b_re, ssm_b_im, ssm_c_re, ssm_c_im,
              ssm_d, ssm_glu_w, ssm_glu_b, w_branch_s, w_out,
              ffn_up, ffn_conv_w, ffn_conv_b, ffn_down):
    ctx_len = x_prompt.shape[1]
    rows = x_sample.shape[1] // GRID_W
    xp, xs = x_prompt, x_sample
    new_re, new_im = [], []
    for l in range(DEPTH):
        p = {
            'ada_w': ada_w[l], 'ada_b': ada_b[l], 'norm1_g': norm1_g[l], 'norm2_g': norm2_g[l],
            'w_in': w_in[l], 'b_in': b_in[l],
            'hy_short_w': hy_short_w[l], 'hy_short_b': hy_short_b[l],
            'hy_f_w1': hy_f_w1[l], 'hy_f_b1': hy_f_b1[l], 'hy_f_w2': hy_f_w2[l], 'hy_f_b2': hy_f_b2[l],
            'hy_f_w3': hy_f_w3[l], 'hy_f_b3': hy_f_b3[l], 'hy_f_w4': hy_f_w4[l],
            'hy_f_freq': hy_f_freq[l], 'hy_bias': hy_bias[l], 'w_branch_h': w_branch_h[l],
            'ssm_lambda_re': ssm_lambda_re[l], 'ssm_lambda_im': ssm_lambda_im[l],
            'ssm_log_dt': ssm_log_dt[l], 'ssm_b_re': ssm_b_re[l], 'ssm_b_im': ssm_b_im[l],
            'ssm_c_re': ssm_c_re[l], 'ssm_c_im': ssm_c_im[l], 'ssm_d': ssm_d[l],
            'ssm_glu_w': ssm_glu_w[l], 'ssm_glu_b': ssm_glu_b[l], 'w_branch_s': w_branch_s[l],
            'w_out': w_out[l], 'ffn_up': ffn_up[l], 'ffn_conv_w': ffn_conv_w[l],
            'ffn_conv_b': ffn_conv_b[l], 'ffn_down': ffn_down[l],
        }
        xp, st = _layer(xp, c_ctx[None, :], None, 1, ctx_len, p)
        new_re.append(jnp.real(st))
        new_im.append(jnp.imag(st))
        s0 = lax.complex(state_ssm_re[:, l].astype(F32), state_ssm_im[:, l].astype(F32))
        xs, _ = _layer(xs, c, s0, rows, GRID_W, p)
    y_prompt = _rmsnorm(xp, final_g)
    y_sample = _rmsnorm(xs, final_g)
    new_state_ssm_re = jnp.stack(new_re, axis=1)
    new_state_ssm_im = jnp.stack(new_im, axis=1)
    return (y_prompt, y_sample, new_state_ssm_re, new_state_ssm_im)
```

```python
import functools
import math

import jax
import jax.numpy as jnp
import numpy as np
from jax import lax
from jax.experimental import pallas as pl
from jax.experimental.pallas import tpu as pltpu

F32 = jnp.float32
BF16 = jnp.bfloat16
HIGHEST = lax.Precision.HIGHEST

GRID_W = 64
EPS = 1e-6
SSM_GROUP = 16
SSM_STATE = 64
FILTER_EMB = 33
FILTER_HIDDEN = 64
HYENA_ORDER = 2
DECAY_FAST = 0.3
DECAY_SLOW = 1.5
DECAY_TARGET = 1e-2
LAMBDA_RE_MAX = -1e-4

S5_CHUNK = 16
S5_ROWS = 8
V7X_VMEM_LIMIT = 56 * 2**20
N_MOD = 6


def _cparams(sem):
    return pltpu.CompilerParams(dimension_semantics=sem, vmem_limit_bytes=V7X_VMEM_LIMIT)


def _resident(shape, index_map):
    return pl.BlockSpec(shape, index_map, pipeline_mode=pl.Buffered(1))


def _mod_rmsnorm(x, g, sc, sh):
    ms = jnp.mean(x * x, axis=-1, keepdims=True)
    return x * lax.rsqrt(ms + EPS) * g * (1.0 + sc) + sh


def _ada_body(c_ref, w_ref, b_ref, o_ref):
    c = c_ref[...]
    s = c * jax.nn.sigmoid(c)
    o_ref[...] = jnp.dot(s, w_ref[...], precision=HIGHEST, preferred_element_type=F32) + b_ref[...]


def _ada(cond, w, b):
    d, n = w.shape
    tn = 1024
    return pl.pallas_call(
        _ada_body,
        grid=(n // tn,),
        in_specs=[pl.BlockSpec((8, d), lambda j: (0, 0)),
                  pl.BlockSpec((d, tn), lambda j: (0, j)),
                  pl.BlockSpec((1, tn), lambda j: (0, j))],
        out_specs=pl.BlockSpec((8, tn), lambda j: (0, j)),
        out_shape=jax.ShapeDtypeStruct((8, n), F32),
        compiler_params=_cparams(("parallel",)),
        name="ada",
    )(cond, w, b.reshape(1, n))


def _inproj_body(n_plain, x_ref, g_ref, sc_ref, sh_ref, w_ref, b_ref, o_ref, h_scr):
    j = pl.program_id(1)

    @pl.when(j == 0)
    def _():
        h_scr[...] = _mod_rmsnorm(x_ref[...], g_ref[...], sc_ref[...], sh_ref[...]).astype(BF16)

    acc = jnp.dot(h_scr[...], w_ref[...], preferred_element_type=F32) + b_ref[...]

    @pl.when(j < n_plain)
    def _():
        o_ref[...] = acc.astype(BF16)

    @pl.when(j >= n_plain)
    def _():
        o_ref[...] = jax.nn.sigmoid(acc).astype(BF16)


def _in_proj(x, mod4, cond_of_tile, g, w, b, n_plain_cols, tm=1024, tn=1024):
    r, d = x.shape
    n = w.shape[1]
    mod_spec = lambda chunk: pl.BlockSpec((None, None, 1, d), lambda i, j: (cond_of_tile(i, tm), chunk, 0, 0))
    return pl.pallas_call(
        functools.partial(_inproj_body, n_plain_cols // tn),
        grid=(r // tm, n // tn),
        in_specs=[pl.BlockSpec((tm, d), lambda i, j: (i, 0)),
                  pl.BlockSpec((1, d), lambda i, j: (0, 0)),
                  mod_spec(1), mod_spec(0),
                  pl.BlockSpec((d, tn), lambda i, j: (0, j)),
                  pl.BlockSpec((1, tn), lambda i, j: (0, j))],
        out_specs=pl.BlockSpec((tm, tn), lambda i, j: (i, j)),
        out_shape=jax.ShapeDtypeStruct((r, n), BF16),
        scratch_shapes=[pltpu.VMEM((tm, d), BF16)],
        compiler_params=_cparams(("parallel", "arbitrary")),
        name="in_proj",
    )(x, g.reshape(1, d), mod4, mod4, w, b.reshape(1, n))


def _shifted_dft_tables(p):
    theta = np.pi * (2.0 * np.arange(p) + 1.0) / (2.0 * p)
    n = np.arange(p)
    a = theta[:, None] * n[None, :]
    fwd = np.concatenate([np.cos(a), -np.sin(a)], axis=0)
    m = np.arange(2 * p) - p
    am = theta[:, None] * m[None, :]
    filt = np.concatenate([np.cos(am), -np.sin(am)], axis=0)
    filt[:, 0] = 0.0
    inv = np.concatenate([np.cos(a).T, -np.sin(a).T], axis=1) / p
    return fwd, filt, inv


_FILTER_RC = 1024


def _filter_positions(l):
    pos = np.abs(np.arange(2 * l) - l).astype(np.float64)
    t = pos / (l - 1)
    bands = (FILTER_EMB - 1) // 2
    fr = np.linspace(1e-4, bands - 1, bands)
    ang = (2.0 * math.pi / l) * pos[:, None] * fr[None, :]
    z = np.concatenate([t[:, None], np.cos(ang), -np.sin(ang)], axis=-1)
    zp = np.zeros((2 * l, 128), np.float32)
    zp[:, :FILTER_EMB] = z
    return zp


def _filter_hidden_body(z_ref, w1_ref, b1_ref, w2_ref, b2_ref, w3_ref, b3_ref, fq_ref, o_ref):
    fq = fq_ref[...]
    dot = functools.partial(jnp.dot, precision=HIGHEST, preferred_element_type=F32)
    h = jnp.sin(fq * (dot(z_ref[...], w1_ref[...]) + b1_ref[...]))
    h = jnp.sin(fq * (dot(h, w2_ref[...]) + b2_ref[...]))
    o_ref[...] = jnp.sin(fq * (dot(h, w3_ref[...]) + b3_ref[...]))


def _filter_hidden(zp, w1p, b1, w2, b2, w3, b3, freq):
    rows = zp.shape[0]
    rc = min(rows, _FILTER_RC)
    hd = w2.shape[0]
    full = lambda shape: pl.BlockSpec(shape, lambda i: (0,) * len(shape))
    row = lambda a: a.reshape(1, -1)
    return pl.pallas_call(
        _filter_hidden_body,
        grid=(rows // rc,),
        in_specs=[pl.BlockSpec((rc, 128), lambda i: (i, 0)), full((128, hd)), full((1, hd)), full((hd, hd)),
                  full((1, hd)), full((hd, hd)), full((1, hd)), full((1, hd))],
        out_specs=pl.BlockSpec((rc, hd), lambda i: (i, 0)),
        out_shape=jax.ShapeDtypeStruct((rows, hd), F32),
        compiler_params=_cparams(("parallel",)),
        name=f"hyena_filter_mlp_{rows}",
    )(zp, w1p, row(b1), w2, row(b2), w3, row(b3), row(freq))


def _filter_body(l, p, t_ref, h_ref, w4f_ref, w4b_ref, dl_ref, mg_ref, o_ref, kk_scr, mg_scr):
    mg_scr[...] = mg_ref[...].astype(BF16)
    dot = functools.partial(jnp.dot, precision=HIGHEST, preferred_element_type=F32)
    rc = min(2 * l, _FILTER_RC)
    ct = kk_scr.shape[1]

    def taps(i, asum):
        r0 = pl.multiple_of(i * rc, rc)
        h = h_ref[pl.ds(r0, rc), :]
        q = r0 + lax.broadcasted_iota(jnp.int32, (rc, ct), 0)
        k = jnp.where(q >= l, dot(h, w4f_ref[...]), dot(h, w4b_ref[...]))
        k = jnp.where(q == 0, 0.0, k) * jnp.exp(-t_ref[pl.ds(r0, rc), :] * dl_ref[...])
        kk_scr[pl.ds(r0, rc), :] = k.astype(BF16)
        return asum + jnp.sum(jnp.abs(k), axis=0, keepdims=True)

    asum = lax.fori_loop(0, (2 * l) // rc, taps, jnp.zeros((1, ct), F32))
    scale = 1.0 / asum
    nwin = 2 * (l // p) - 1

    def win(w, _):
        seg = kk_scr[pl.ds(pl.multiple_of(w * p, p), 2 * p), :]
        o_ref[w] = jnp.dot(mg_scr[...], seg, preferred_element_type=F32) * scale
        return 0

    lax.fori_loop(0, nwin, win, 0)


def _hyena_filters(l, p, w1, b1, w2, b2, w3, b3, w4, freq, c, ct=128):
    nwin = 2 * (l // p) - 1
    nct = c // ct
    zp = jnp.asarray(_filter_positions(l))
    w1p = jnp.zeros((128, FILTER_HIDDEN), F32).at[:FILTER_EMB].set(w1)
    hid = _filter_hidden(zp, w1p, b1, w2, b2, w3, b3, freq)
    deltas = np.abs(np.linspace(math.log(DECAY_TARGET) / DECAY_SLOW, math.log(DECAY_TARGET) / DECAY_FAST, c))
    mg = jnp.asarray(_shifted_dft_tables(p)[1], F32)
    full = lambda shape: pl.BlockSpec(shape, lambda o, j: (0,) * len(shape))
    return pl.pallas_call(
        functools.partial(_filter_body, l, p),
        grid=(HYENA_ORDER, nct),
        in_specs=[full((2 * l, 1)), full((2 * l, FILTER_HIDDEN)),
                  pl.BlockSpec((FILTER_HIDDEN, ct), lambda o, j: (0, o * 2 * nct + j)),
                  pl.BlockSpec((FILTER_HIDDEN, ct), lambda o, j: (0, o * 2 * nct + nct + j)),
                  pl.BlockSpec((1, ct), lambda o, j: (0, j)),
                  full((2 * p, 2 * p))],
        out_specs=pl.BlockSpec((None, nwin, 2 * p, ct), lambda o, j: (o, 0, 0, j)),
        out_shape=jax.ShapeDtypeStruct((HYENA_ORDER, nwin, 2 * p, c), F32),
        scratch_shapes=[pltpu.VMEM((2 * l, ct), BF16), pltpu.VMEM((2 * p, 2 * p), BF16)],
        compiler_params=_cparams(("parallel", "parallel")),
        name=f"hyena_filter_{l}",
    )(zp[:, 0:1], hid, w4, w4, jnp.asarray(deltas, F32).reshape(1, c), mg)


_HY_RC = 64


def _hyena_body(nseq, l, p, x1_ref, x2_ref, v_ref, w1_ref, w2_ref, wv_ref, b1_ref, b2_ref, bv_ref, hb_ref,
                g_ref, fz_ref, fi_ref, o_ref, z_scr, c_scr, zh_scr, yh_scr, z1_scr, fz_scr, fi_scr):
    fz_scr[...] = fz_ref[...].astype(BF16)
    fi_scr[...] = fi_ref[...].astype(BF16)
    o = pl.program_id(1)
    s = pl.program_id(2)
    rows = nseq * l
    nb = l // p
    ct = o_ref.shape[1]
    pos = lax.rem(lax.broadcasted_iota(jnp.int32, (rows, ct), 0), l)

    def short_conv(u_ref, w_ref, b_ref):
        u = u_ref[...].astype(F32)
        um = jnp.where(pos == 0, 0.0, pltpu.roll(u, 1, 0))
        up = jnp.where(pos == l - 1, 0.0, pltpu.roll(u, rows - 1, 0))
        w = w_ref[...]
        return um * w[0:1] + u * w[1:2] + up * w[2:3] + b_ref[...]

    def long_conv():
        for q in range(nseq):
            base = q * l

            def fwd(j, _):
                zb = z_scr[pl.ds(pl.multiple_of(base + j * p, p), p), :].astype(BF16)
                zh_scr[j] = jnp.dot(fz_scr[...], zb, preferred_element_type=F32)
                return 0

            lax.fori_loop(0, nb, fwd, 0)

            def out_block(i, _):
                for rc in range(p // _HY_RC):
                    re = pl.ds(rc * _HY_RC, _HY_RC)
                    im = pl.ds(p + rc * _HY_RC, _HY_RC)

                    def acc(j, carry):
                        ar, ai = carry
                        w = i - j + (nb - 1)
                        gr, gi = g_ref[w, re, :], g_ref[w, im, :]
                        zr, zi = zh_scr[j, re, :], zh_scr[j, im, :]
                        return ar + gr * zr - gi * zi, ai + gr * zi + gi * zr

                    zero = jnp.zeros((_HY_RC, ct), F32)
                    ar, ai = lax.fori_loop(0, nb, acc, (zero, zero))
                    yh_scr[re, :] = ar
                    yh_scr[im, :] = ai
                c_scr[pl.ds(pl.multiple_of(base + i * p, p), p), :] = jnp.dot(
                    fi_scr[...], yh_scr[...].astype(BF16), preferred_element_type=F32)
                return 0

            lax.fori_loop(0, nb, out_block, 0)

    hb = hb_ref[...]

    @pl.when(o == 0)
    def _():
        v = short_conv(v_ref, wv_ref, bv_ref)
        z_scr[...] = v
        long_conv()
        x1 = short_conv(x1_ref, w1_ref, b1_ref)
        z1_scr[s] = x1 * (c_scr[...] + hb[0:1] * v)

    @pl.when(o == 1)
    def _():
        z1 = z1_scr[s]
        z_scr[...] = z1
        long_conv()
        x2 = short_conv(x2_ref, w2_ref, b2_ref)
        o_ref[...] = (x2 * (c_scr[...] + hb[1:2] * z1)).astype(BF16)


def _hyena(proj, l, p, nseq, short_w, short_b, hy_bias, ghat, c, ct=128):
    r = proj.shape[0]
    rows = nseq * l
    nsb = r // rows
    nct = c // ct
    nwin = ghat.shape[1]
    fwd, _, inv = _shifted_dft_tables(p)
    fz = jnp.asarray(fwd, F32)
    fi = jnp.asarray(inv, F32)
    sb = short_b.reshape(1, 3 * c)
    col = lambda k: (lambda j, o, s: (s, k * nct + j))
    wcol = lambda k: (lambda j, o, s: (0, k * nct + j))
    return pl.pallas_call(
        functools.partial(_hyena_body, nseq, l, p),
        grid=(nct, HYENA_ORDER, nsb),
        in_specs=[pl.BlockSpec((rows, ct), col(0)), pl.BlockSpec((rows, ct), col(1)), pl.BlockSpec((rows, ct), col(2)),
                  pl.BlockSpec((3, ct), wcol(0)), pl.BlockSpec((3, ct), wcol(1)), pl.BlockSpec((3, ct), wcol(2)),
                  pl.BlockSpec((1, ct), wcol(0)), pl.BlockSpec((1, ct), wcol(1)), pl.BlockSpec((1, ct), wcol(2)),
                  pl.BlockSpec((HYENA_ORDER, ct), lambda j, o, s: (0, j)),
                  pl.BlockSpec((None, nwin, 2 * p, ct), lambda j, o, s: (o, 0, 0, j)),
                  pl.BlockSpec((2 * p, p), lambda j, o, s: (0, 0)),
                  pl.BlockSpec((p, 2 * p), lambda j, o, s: (0, 0))],
        out_specs=pl.BlockSpec((rows, ct), lambda j, o, s: (s * o, j)),
        out_shape=jax.ShapeDtypeStruct((r, c), BF16),
        scratch_shapes=[pltpu.VMEM((rows, ct), F32), pltpu.VMEM((rows, ct), F32),
                        pltpu.VMEM((l // p, 2 * p, ct), F32), pltpu.VMEM((2 * p, ct), F32),
                        pltpu.VMEM((nsb, rows, ct), F32), pltpu.VMEM((2 * p, p), BF16), pltpu.VMEM((p, 2 * p), BF16)],
        compiler_params=_cparams(("parallel", "arbitrary", "arbitrary")),
        name=f"hyena_{l}",
    )(proj, proj, proj, short_w, short_w, short_w, sb, sb, sb, hy_bias, ghat, fz, fi)


def _s5_param_body(lre_ref, lim_ref, ldt_ref, btr_ref, bti_ref, cr_ref, ci_ref,
                   k_ref, wsr_ref, wsi_ref, wor_ref, woi_ref, etr_ref, eti_ref):
    t = S5_CHUNK
    backward = pl.program_id(0) == 1
    lr = jnp.minimum(lre_ref[...], LAMBDA_RE_MAX)
    li = lim_ref[...]
    dt = jnp.exp(ldt_ref[...])
    ar, ai = lr * dt, li * dt

    def power(e):
        mag = jnp.exp(e * ar)
        return mag * jnp.cos(e * ai), mag * jnp.sin(e * ai)

    one = jnp.ones((1, 1), F32)
    lbr, lbi = power(one)
    nr, ni = lbr - 1.0, lbi
    den = lr * lr + li * li
    qr, qi = (nr * lr + ni * li) / den, (ni * lr - nr * li) / den
    btr, bti = btr_ref[...], bti_ref[...]
    bbr, bbi = qr * btr - qi * bti, qr * bti + qi * btr
    cr, ci = cr_ref[...], ci_ref[...]

    step = lax.broadcasted_iota(jnp.int32, (t, 1, 1), 0).astype(F32)
    er, ei = power(step)
    cer, cei = cr[None] * er - ci[None] * ei, cr[None] * ei + ci[None] * er
    nt = (((1,), (1,)), ((), ()))
    dg = functools.partial(lax.dot_general, dimension_numbers=nt, precision=HIGHEST, preferred_element_type=F32)
    k_ref[...] = (dg(cer.reshape(t * SSM_GROUP, SSM_STATE), bbr)
                  - dg(cei.reshape(t * SSM_GROUP, SSM_STATE), bbi))
    er, ei = power(jnp.where(backward, step, (t - 1.0) - step))
    wsr_ref[...] = er * bbr[None] - ei * bbi[None]
    wsi_ref[...] = er * bbi[None] + ei * bbr[None]
    er, ei = power(jnp.where(backward, t - step, step + 1.0))
    wor_ref[...] = cr[None] * er - ci[None] * ei
    woi_ref[...] = -(cr[None] * ei + ci[None] * er)
    etr, eti = power(one * t)
    etr_ref[...] = etr
    eti_ref[...] = eti


def _s5_params(lam_re, lam_im, log_dt, b_re, b_im, c_re, c_im):
    g = lam_re.shape[1]
    t, cg, ps = S5_CHUNK, SSM_GROUP, SSM_STATE
    vec = lambda a: a.reshape(2, g, 1, ps)
    ldt = jnp.broadcast_to(log_dt[:, :, None, None], (2, g, 1, ps))
    bt = lambda a: jnp.swapaxes(a, -1, -2)
    dg_spec = lambda *shape: pl.BlockSpec((None, None) + shape, lambda d, i: (d, i) + (0,) * len(shape))
    outs = [jax.ShapeDtypeStruct((2, g, t * cg, cg), F32)] + \
           [jax.ShapeDtypeStruct((2, g, t, cg, ps), F32)] * 4 + [jax.ShapeDtypeStruct((2, g, 1, ps), F32)] * 2
    return pl.pallas_call(
        _s5_param_body,
        grid=(2, g),
        in_specs=[dg_spec(1, ps), dg_spec(1, ps), dg_spec(1, ps), dg_spec(cg, ps), dg_spec(cg, ps),
                  dg_spec(cg, ps), dg_spec(cg, ps)],
        out_specs=[dg_spec(t * cg, cg)] + [dg_spec(t, cg, ps)] * 4 + [dg_spec(1, ps)] * 2,
        out_shape=outs,
        compiler_params=_cparams(("arbitrary", "arbitrary")),
        name="s5_params",
    )(vec(lam_re), vec(lam_im), ldt, bt(b_re), bt(b_im), c_re, c_im)


def _s5_operators(lam_re, lam_im, log_dt, b_re, b_im, c_re, c_im):
    kk, wsr, wsi, wor, woi, etr, eti = _s5_params(lam_re, lam_im, log_dt, b_re, b_im, c_re, c_im)
    g = kk.shape[1]
    t, cg, ps = S5_CHUNK, SSM_GROUP, SSM_STATE
    kd = kk.reshape(2, g, t, cg, cg)
    s_idx = np.arange(t)[:, None]
    t_idx = np.arange(t)[None, :]

    def toeplitz(kdir, lag):
        m = jnp.where((lag >= 0)[None, :, :, None, None], kdir[:, np.mod(lag, t)], 0.0)
        return jnp.transpose(m, (0, 1, 4, 2, 3)).reshape(g, t * cg, t * cg)

    mf = toeplitz(kd[0], t_idx - s_idx)
    mb = toeplitz(kd[1], s_idx - t_idx)
    flat = lambda a: a.reshape(g, t * cg, ps)
    w1 = jnp.concatenate([mf, mb, flat(wsr[0]), flat(wsr[1]), flat(wsi[0]), flat(wsi[1])], axis=-1)
    tr = lambda a: jnp.swapaxes(flat(a), 1, 2)
    w2 = jnp.concatenate([tr(wor[0]), tr(wor[1]), tr(woi[0]), tr(woi[1])], axis=1)
    lam = jnp.concatenate([etr[0], etr[1], eti[0], eti[1]], axis=-1)
    return w1.astype(BF16), w2.astype(BF16), lam


def _s5_body(nk, bp, gb, u_ref, w1_ref, w2_ref, lam_ref, s0_ref, y_ref, fin_ref, yl_scr, x_scr, p_scr):
    tw = S5_CHUNK * SSM_GROUP
    half = SSM_STATE
    for gi in range(gb):
        z = jnp.dot(u_ref[gi], w1_ref[gi], preferred_element_type=F32)
        yl_scr[gi] = z[:, :tw] + z[:, tw:2 * tw]
        x_scr[gi] = z[:, 2 * tw:]
    fwd_lane = lax.broadcasted_iota(jnp.int32, (bp, 2 * half), 1) < half

    def step(t, carry):
        kf = pl.ds(pl.multiple_of(t * bp, bp), bp)
        kb = pl.ds(pl.multiple_of((nk - 1 - t) * bp, bp), bp)
        new = []
        for gi in range(gb):
            sr, si = carry[gi]
            p_scr[gi, kf, 0:2 * half] = jnp.where(fwd_lane, sr, p_scr[gi, kf, 0:2 * half])
            p_scr[gi, kb, 0:2 * half] = jnp.where(fwd_lane, p_scr[gi, kb, 0:2 * half], sr)
            p_scr[gi, kf, 2 * half:] = jnp.where(fwd_lane, si, p_scr[gi, kf, 2 * half:])
            p_scr[gi, kb, 2 * half:] = jnp.where(fwd_lane, p_scr[gi, kb, 2 * half:], si)
            xr = jnp.where(fwd_lane, x_scr[gi, kf, 0:2 * half], x_scr[gi, kb, 0:2 * half])
            xi = jnp.where(fwd_lane, x_scr[gi, kf, 2 * half:], x_scr[gi, kb, 2 * half:])
            lam = lam_ref[gi]
            lr, li = lam[:, 0:2 * half], lam[:, 2 * half:]
            new.append((lr * sr - li * si + xr, lr * si + li * sr + xi))
        return tuple(new)

    p_scr[...] = jnp.zeros(p_scr.shape, F32)
    init = tuple((s0_ref[gi][:, 0:2 * half], s0_ref[gi][:, 2 * half:]) for gi in range(gb))
    fin = lax.fori_loop(0, nk, step, init)
    for gi in range(gb):
        y = yl_scr[gi] + jnp.dot(p_scr[gi].astype(BF16), w2_ref[gi], preferred_element_type=F32)
        y_ref[gi] = y.astype(BF16)
        fin_ref[gi] = jnp.concatenate(fin[gi], axis=-1)


def _s5_scan(u, w1, w2, lam, s0, nk, bp, gb=2):
    g, nrows, tw = u.shape
    blk = lambda *shape: pl.BlockSpec((gb,) + shape, lambda i: (i,) + (0,) * len(shape))
    return pl.pallas_call(
        functools.partial(_s5_body, nk, bp, gb),
        grid=(g // gb,),
        in_specs=[blk(nrows, tw), blk(tw, 3 * tw), blk(tw, tw), blk(1, tw), blk(bp, tw)],
        out_specs=[blk(nrows, tw), blk(bp, tw)],
        out_shape=[jax.ShapeDtypeStruct((g, nrows, tw), BF16), jax.ShapeDtypeStruct((g, bp, tw), F32)],
        scratch_shapes=[pltpu.VMEM((gb, nrows, tw), F32)] * 3,
        compiler_params=_cparams(("parallel",)),
        name=f"s5_scan_{nk}",
    )(u, w1, w2, lam, s0)


def _s5(proj, col0, bsz, l, w1, w2, lam, s0):
    g = w1.shape[0]
    t, cg, ps = S5_CHUNK, SSM_GROUP, SSM_STATE
    nk = l // t
    bp = -(-bsz // S5_ROWS) * S5_ROWS
    u = proj[:, col0:col0 + g * cg].reshape(bsz, nk, t, g, cg)
    u = jnp.transpose(u, (3, 1, 0, 2, 4))
    u = jnp.pad(u, ((0, 0), (0, 0), (0, bp - bsz), (0, 0), (0, 0))).reshape(g, nk * bp, t * cg)
    if s0 is None:
        s0p = jnp.zeros((g, bp, 4 * ps), F32)
    else:
        s0_re, s0_im = s0
        s0p = jnp.concatenate([s0_re[:, 0], s0_re[:, 1], s0_im[:, 0], s0_im[:, 1]], axis=-1)
        s0p = jnp.pad(jnp.transpose(s0p, (1, 0, 2)), ((0, 0), (0, bp - bsz), (0, 0)))
    y, fin = _s5_scan(u, w1, w2, lam, s0p, nk, bp)
    y = y.reshape(g, nk, bp, t, cg)[:, :, :bsz]
    y = jnp.transpose(y, (2, 1, 3, 0, 4)).reshape(bsz * l, g * cg)
    fin = jnp.transpose(fin[:, :bsz].reshape(g, bsz, 2, 2, ps), (2, 1, 3, 0, 4))
    return y, fin[0], fin[1]


def _merge_body(yh_ref, ys_ref, u_ref, gh_ref, gs_ref, x_ref, d_ref, gw_ref, gbias_ref, wbh_ref, wbs_ref, wo_ref,
                g1_ref, n2_ref, sc2_ref, sh2_ref, x1_ref, h2_ref):
    y = ys_ref[...].astype(F32) + u_ref[...].astype(F32) * d_ref[...]
    g = jax.nn.gelu(y)
    gl = jnp.dot(g.astype(BF16), gw_ref[...], preferred_element_type=F32) + gbias_ref[...]
    ys = (g * jax.nn.sigmoid(gl)).astype(BF16)
    bh = jnp.dot(yh_ref[...], wbh_ref[...], preferred_element_type=F32)
    bs = jnp.dot(ys, wbs_ref[...], preferred_element_type=F32)
    merged = gh_ref[...].astype(F32) * bh + gs_ref[...].astype(F32) * bs
    x1 = x_ref[...] + g1_ref[...] * jnp.dot(merged.astype(BF16), wo_ref[...], preferred_element_type=F32)
    x1_ref[...] = x1
    h2_ref[...] = _mod_rmsnorm(x1, n2_ref[...], sc2_ref[...], sh2_ref[...]).astype(BF16)


def _merge(yh, ys, proj, x, mod4, cond_of_tile, ssm_d, glu_w, glu_b, wbh, wbs, wo, n2g, c_h, c_s, tm=256):
    r, d = x.shape
    row_blk = lambda width, k: pl.BlockSpec((tm, width), lambda i: (i, k))
    mod_spec = lambda chunk: pl.BlockSpec((None, None, 1, d), lambda i: (cond_of_tile(i, tm), chunk, 0, 0))
    const = lambda shape: _resident(shape, lambda i: (0,) * len(shape))
    u_col = (3 * c_h) // c_s
    gate0 = (3 * c_h + c_s) // d
    return pl.pallas_call(
        _merge_body,
        grid=(r // tm,),
        in_specs=[row_blk(c_h, 0), row_blk(c_s, 0), row_blk(c_s, u_col), row_blk(d, gate0), row_blk(d, gate0 + 1),
                  row_blk(d, 0), const((1, c_s)), const((c_s, c_s)), const((1, c_s)), const((c_h, d)),
                  const((c_s, d)), const((d, d)), mod_spec(2), const((1, d)), mod_spec(4), mod_spec(3)],
        out_specs=[row_blk(d, 0), row_blk(d, 0)],
        out_shape=[jax.ShapeDtypeStruct((r, d), F32), jax.ShapeDtypeStruct((r, d), BF16)],
        compiler_params=_cparams(("parallel",)),
        name="merge",
    )(yh, ys, proj, proj, proj, x, ssm_d.reshape(1, c_s), glu_w, glu_b.reshape(1, c_s), wbh, wbs, wo,
      mod4, n2g.reshape(1, d), mod4, mod4)


def _ffn_up_body(rows, cols, h_ref, wa_ref, wb_ref, cwa_ref, cwb_ref, cba_ref, cbb_ref, o_ref, *scr):
    tm, tn = o_ref.shape
    ridx = lax.broadcasted_iota(jnp.int32, (tm, tn), 0)
    col = lax.rem(ridx, cols)
    not_first = col != 0
    not_last = col != cols - 1
    h = h_ref[...]

    def conv(w_ref, cw_ref, cb_ref, scr3):
        a = jnp.dot(h, w_ref[...], preferred_element_type=F32)
        am = jnp.where(not_first, pltpu.roll(a, 1, 0), 0.0)
        ap = jnp.where(not_last, pltpu.roll(a, tm - 1, 0), 0.0)
        cw = cw_ref[...]
        tap = lambda i: am * cw[3 * i:3 * i + 1] + a * cw[3 * i + 1:3 * i + 2] + ap * cw[3 * i + 2:3 * i + 3]
        y = tap(1) + cb_ref[...]
        if rows > 1:
            up_scr, dn_scr = scr3
            zero = jnp.zeros((cols, tn), F32)
            up_scr[0:cols, :] = zero
            up_scr[cols:, :] = tap(0)
            dn_scr[tm:, :] = zero
            dn_scr[0:tm, :] = tap(2)
            y = y + up_scr[0:tm, :] + dn_scr[cols:, :]
        return y

    a = conv(wa_ref, cwa_ref, cba_ref, scr[0:2])
    b = conv(wb_ref, cwb_ref, cbb_ref, scr[0:2])
    o_ref[...] = (jax.nn.gelu(a) * b).astype(BF16)


def _ffn_up(h2, w_up, conv_w, conv_b, rows, cols, tn=256):
    r, d = h2.shape
    f = w_up.shape[1] // 2
    tm = rows * cols if rows > 1 else 4096
    nt = f // tn
    cw = conv_w.reshape(9, 2 * f)
    cb = conv_b.reshape(1, 2 * f)
    scratch = [pltpu.VMEM((tm + cols, tn), F32)] * 2 if rows > 1 else []
    return pl.pallas_call(
        functools.partial(_ffn_up_body, rows, cols),
        grid=(r // tm, nt),
        in_specs=[pl.BlockSpec((tm, d), lambda i, j: (i, 0), pipeline_mode=pl.Buffered(1)),
                  pl.BlockSpec((d, tn), lambda i, j: (0, j)), pl.BlockSpec((d, tn), lambda i, j: (0, nt + j)),
                  pl.BlockSpec((9, tn), lambda i, j: (0, j)), pl.BlockSpec((9, tn), lambda i, j: (0, nt + j)),
                  pl.BlockSpec((1, tn), lambda i, j: (0, j)), pl.BlockSpec((1, tn), lambda i, j: (0, nt + j))],
        out_specs=pl.BlockSpec((tm, tn), lambda i, j: (i, j)),
        out_shape=jax.ShapeDtypeStruct((r, f), BF16),
        scratch_shapes=scratch,
        compiler_params=_cparams(("parallel", "arbitrary")),
        name=f"ffn_up_{rows}x{cols}",
    )(h2, w_up, w_up, cw, cw, cb, cb)


def _ffn_down_body(a_ref, w_ref, x_ref, g2_ref, fg_ref, o_ref):
    x2 = x_ref[...] + g2_ref[...] * jnp.dot(a_ref[...], w_ref[...], preferred_element_type=F32)
    ms = jnp.mean(x2 * x2, axis=-1, keepdims=True)
    o_ref[...] = x2 * lax.rsqrt(ms + EPS) * fg_ref[...]


def _ffn_down(act, w, x1, mod4, cond_of_tile, final_g, tm=256):
    r, f = act.shape
    d = w.shape[1]
    return pl.pallas_call(
        _ffn_down_body,
        grid=(r // tm,),
        in_specs=[pl.BlockSpec((tm, f), lambda i: (i, 0)),
                  _resident((f, d), lambda i: (0, 0)),
                  pl.BlockSpec((tm, d), lambda i: (i, 0)),
                  pl.BlockSpec((None, None, 1, d), lambda i: (cond_of_tile(i, tm), 5, 0, 0)),
                  _resident((1, d), lambda i: (0, 0))],
        out_specs=pl.BlockSpec((tm, d), lambda i: (i, 0)),
        out_shape=jax.ShapeDtypeStruct((r, d), F32),
        compiler_params=_cparams(("parallel",)),
        name="ffn_down",
    )(act, w, x1, mod4, final_g.reshape(1, d))


def _segment(x, cond_base, per_batch_cond, s0, rows, cols, hy_block, hy_nseq, mod4, p, final_g):
    bsz, l, d = x.shape
    xf = x.reshape(bsz * l, d)
    c_h = p["wbh"].shape[0]
    c_s = p["wbs"].shape[0]

    def cond_of_tile(i, tm):
        return cond_base + (i * tm) // l if per_batch_cond else cond_base

    proj = _in_proj(xf, mod4, cond_of_tile, p["norm1_g"], p["w_in"], p["b_in"], 3 * c_h + c_s)
    yh = _hyena(proj, l, hy_block, hy_nseq, p["hy_short_w"], p["hy_short_b"], p["hy_bias"], p["ghat"][l], c_h)
    ys, fin_re, fin_im = _s5(proj, 3 * c_h, bsz, l, p["s5_w1"], p["s5_w2"], p["s5_lam"], s0)
    x1, h2 = _merge(yh, ys, proj, xf, mod4, cond_of_tile, p["ssm_d"], p["glu_w"], p["glu_b"],
                    p["wbh"], p["wbs"], p["wo"], p["norm2_g"], c_h, c_s)
    act = _ffn_up(h2, p["ffn_up"], p["ffn_conv_w"], p["ffn_conv_b"], rows, cols)
    y = _ffn_down(act, p["ffn_down"], x1, mod4, cond_of_tile, final_g)
    return y.reshape(bsz, l, d), fin_re, fin_im


def kernel(x_prompt, x_sample, state_ssm_re, state_ssm_im, c, c_ctx, ada_w, ada_b, norm1_g, norm2_g, final_g, w_in, b_in, hy_short_w, hy_short_b, hy_f_w1, hy_f_b1, hy_f_w2, hy_f_b2, hy_f_w3, hy_f_b3, hy_f_w4, hy_f_freq, hy_bias, w_branch_h, ssm_lambda_re, ssm_lambda_im, ssm_log_dt, ssm_b_re, ssm_b_im, ssm_c_re, ssm_c_im, ssm_d, ssm_glu_w, ssm_glu_b, w_branch_s, w_out, ffn_up, ffn_conv_w, ffn_conv_b, ffn_down):
    depth = ada_w.shape[0]
    assert depth == 1, "the final norm is fused into the (single) layer's last kernel"
    d = x_prompt.shape[-1]
    ctx_len = x_prompt.shape[1]
    lat_len = x_sample.shape[1]
    dec_b = x_sample.shape[0]
    c_h = w_branch_h.shape[1]
    assert 1 + dec_b <= 8
    cond = jnp.zeros((8, d), F32).at[0].set(c_ctx).at[1:1 + dec_b].set(c)
    lay = 0
    mod4 = _ada(cond, ada_w[lay], ada_b[lay]).reshape(8, N_MOD, 1, d)
    hy_blocks = {ctx_len: 256, lat_len: 512}
    ghat = {l: _hyena_filters(l, hy_blocks[l], hy_f_w1[lay], hy_f_b1[lay], hy_f_w2[lay], hy_f_b2[lay],
                              hy_f_w3[lay], hy_f_b3[lay], hy_f_w4[lay], hy_f_freq[lay], c_h)
            for l in (ctx_len, lat_len)}
    s5_w1, s5_w2, s5_lam = _s5_operators(ssm_lambda_re[lay], ssm_lambda_im[lay], ssm_log_dt[lay],
                                         ssm_b_re[lay], ssm_b_im[lay], ssm_c_re[lay], ssm_c_im[lay])
    p = dict(norm1_g=norm1_g[lay], norm2_g=norm2_g[lay], w_in=w_in[lay].astype(BF16), b_in=b_in[lay],
             hy_short_w=hy_short_w[lay], hy_short_b=hy_short_b[lay], hy_bias=hy_bias[lay], ghat=ghat,
             s5_w1=s5_w1, s5_w2=s5_w2, s5_lam=s5_lam, ssm_d=ssm_d[lay], glu_w=ssm_glu_w[lay].astype(BF16),
             glu_b=ssm_glu_b[lay], wbh=w_branch_h[lay].astype(BF16), wbs=w_branch_s[lay].astype(BF16),
             wo=w_out[lay].astype(BF16), ffn_up=ffn_up[lay].astype(BF16), ffn_conv_w=ffn_conv_w[lay],
             ffn_conv_b=ffn_conv_b[lay], ffn_down=ffn_down[lay].astype(BF16))
    y_prompt, st_re, st_im = _segment(x_prompt, 0, False, None, 1, ctx_len, hy_blocks[ctx_len], 8, mod4, p, final_g)
    s0 = (state_ssm_re[:, lay].astype(F32), state_ssm_im[:, lay].astype(F32))
    y_sample, _, _ = _segment(x_sample, 1, True, s0, lat_len // GRID_W, GRID_W, hy_blocks[lat_len], 1, mod4, p, final_g)
    return y_prompt, y_sample, st_re[:, None], st_im[:, None]
```

```python
import functools
import math

import jax
import jax.numpy as jnp
import numpy as np
from jax import lax
from jax.experimental import pallas as pl
from jax.experimental.pallas import tpu as pltpu

F32 = jnp.float32
BF16 = jnp.bfloat16
HIGHEST = lax.Precision.HIGHEST

GRID_W = 64
EPS = 1e-6
SSM_GROUP = 16
SSM_STATE = 64
FILTER_EMB = 33
FILTER_HIDDEN = 64
HYENA_ORDER = 2
DECAY_FAST = 0.3
DECAY_SLOW = 1.5
DECAY_TARGET = 1e-2
LAMBDA_RE_MAX = -1e-4

S5_CHUNK = 16
S5_TILE_GROUPS = 8
V7X_VMEM_LIMIT = 56 * 2**20
N_MOD = 6


def _cparams(sem):
    return pltpu.CompilerParams(dimension_semantics=sem, vmem_limit_bytes=V7X_VMEM_LIMIT)


def _resident(shape, index_map):
    return pl.BlockSpec(shape, index_map, pipeline_mode=pl.Buffered(1))


def _mod_rmsnorm(x, g, sc, sh):
    ms = jnp.mean(x * x, axis=-1, keepdims=True)
    return x * lax.rsqrt(ms + EPS) * g * (1.0 + sc) + sh


def _ada_body(c_ref, w_ref, b_ref, o_ref):
    c = c_ref[...]
    s = c * jax.nn.sigmoid(c)
    o_ref[...] = jnp.dot(s, w_ref[...], precision=HIGHEST, preferred_element_type=F32) + b_ref[...]


def _ada(cond, w, b):
    d, n = w.shape
    tn = 1024
    return pl.pallas_call(
        _ada_body,
        grid=(n // tn,),
        in_specs=[pl.BlockSpec((8, d), lambda j: (0, 0)),
                  pl.BlockSpec((d, tn), lambda j: (0, j)),
                  pl.BlockSpec((1, tn), lambda j: (0, j))],
        out_specs=pl.BlockSpec((8, tn), lambda j: (0, j)),
        out_shape=jax.ShapeDtypeStruct((8, n), F32),
        compiler_params=_cparams(("parallel",)),
        name="ada",
    )(cond, w, b.reshape(1, n))


def _inproj_body(n_plain, x_ref, g_ref, sc_ref, sh_ref, w_ref, b_ref, o_ref, h_scr):
    j = pl.program_id(1)

    @pl.when(j == 0)
    def _():
        h_scr[...] = _mod_rmsnorm(x_ref[...], g_ref[...], sc_ref[...], sh_ref[...]).astype(BF16)

    acc = jnp.dot(h_scr[...], w_ref[...], preferred_element_type=F32) + b_ref[...]

    @pl.when(j < n_plain)
    def _():
        o_ref[...] = acc.astype(BF16)

    @pl.when(j >= n_plain)
    def _():
        o_ref[...] = jax.nn.sigmoid(acc).astype(BF16)


def _in_proj(x, mod4, cond_of_tile, g, w, b, n_plain_cols, tm=1024, tn=1024):
    r, d = x.shape
    n = w.shape[1]
    mod_spec = lambda chunk: pl.BlockSpec((None, None, 1, d), lambda i, j: (cond_of_tile(i, tm), chunk, 0, 0))
    return pl.pallas_call(
        functools.partial(_inproj_body, n_plain_cols // tn),
        grid=(r // tm, n // tn),
        in_specs=[pl.BlockSpec((tm, d), lambda i, j: (i, 0)),
                  pl.BlockSpec((1, d), lambda i, j: (0, 0)),
                  mod_spec(1), mod_spec(0),
                  pl.BlockSpec((d, tn), lambda i, j: (0, j)),
                  pl.BlockSpec((1, tn), lambda i, j: (0, j))],
        out_specs=pl.BlockSpec((tm, tn), lambda i, j: (i, j)),
        out_shape=jax.ShapeDtypeStruct((r, n), BF16),
        scratch_shapes=[pltpu.VMEM((tm, d), BF16)],
        compiler_params=_cparams(("parallel", "arbitrary")),
        name="in_proj",
    )(x, g.reshape(1, d), mod4, mod4, w, b.reshape(1, n))


def _shifted_dft_tables(p):
    theta = np.pi * (2.0 * np.arange(p) + 1.0) / (2.0 * p)
    n = np.arange(p)
    a = theta[:, None] * n[None, :]
    fwd = np.concatenate([np.cos(a), -np.sin(a)], axis=0)
    m = np.arange(2 * p) - p
    am = theta[:, None] * m[None, :]
    filt = np.concatenate([np.cos(am), -np.sin(am)], axis=0)
    filt[:, 0] = 0.0
    inv = np.concatenate([np.cos(a).T, -np.sin(a).T], axis=1) / p
    return fwd, filt, inv


_FILTER_RC = 1024


def _filter_positions(l):
    pos = np.abs(np.arange(2 * l) - l).astype(np.float64)
    t = pos / (l - 1)
    bands = (FILTER_EMB - 1) // 2
    fr = np.linspace(1e-4, bands - 1, bands)
    ang = (2.0 * math.pi / l) * pos[:, None] * fr[None, :]
    z = np.concatenate([t[:, None], np.cos(ang), -np.sin(ang)], axis=-1)
    zp = np.zeros((2 * l, 128), np.float32)
    zp[:, :FILTER_EMB] = z
    return zp


def _filter_hidden_body(z_ref, w1_ref, b1_ref, w2_ref, b2_ref, w3_ref, b3_ref, fq_ref, o_ref):
    fq = fq_ref[...]
    dot = functools.partial(jnp.dot, precision=HIGHEST, preferred_element_type=F32)
    h = jnp.sin(fq * (dot(z_ref[...], w1_ref[...]) + b1_ref[...]))
    h = jnp.sin(fq * (dot(h, w2_ref[...]) + b2_ref[...]))
    o_ref[...] = jnp.sin(fq * (dot(h, w3_ref[...]) + b3_ref[...]))


def _filter_hidden(zp, w1p, b1, w2, b2, w3, b3, freq):
    rows = zp.shape[0]
    rc = min(rows, _FILTER_RC)
    hd = w2.shape[0]
    full = lambda shape: pl.BlockSpec(shape, lambda i: (0,) * len(shape))
    row = lambda a: a.reshape(1, -1)
    return pl.pallas_call(
        _filter_hidden_body,
        grid=(rows // rc,),
        in_specs=[pl.BlockSpec((rc, 128), lambda i: (i, 0)), full((128, hd)), full((1, hd)), full((hd, hd)),
                  full((1, hd)), full((hd, hd)), full((1, hd)), full((1, hd))],
        out_specs=pl.BlockSpec((rc, hd), lambda i: (i, 0)),
        out_shape=jax.ShapeDtypeStruct((rows, hd), F32),
        compiler_params=_cparams(("parallel",)),
        name=f"hyena_filter_mlp_{rows}",
    )(zp, w1p, row(b1), w2, row(b2), w3, row(b3), row(freq))


def _filter_body(l, p, t_ref, h_ref, w4f_ref, w4b_ref, dl_ref, mg_ref, o_ref, kk_scr, mg_scr):
    mg_scr[...] = mg_ref[...].astype(BF16)
    dot = functools.partial(jnp.dot, precision=HIGHEST, preferred_element_type=F32)
    rc = min(2 * l, _FILTER_RC)
    ct = kk_scr.shape[1]

    def taps(i, asum):
        r0 = pl.multiple_of(i * rc, rc)
        h = h_ref[pl.ds(r0, rc), :]
        q = r0 + lax.broadcasted_iota(jnp.int32, (rc, ct), 0)
        k = jnp.where(q >= l, dot(h, w4f_ref[...]), dot(h, w4b_ref[...]))
        k = jnp.where(q == 0, 0.0, k) * jnp.exp(-t_ref[pl.ds(r0, rc), :] * dl_ref[...])
        kk_scr[pl.ds(r0, rc), :] = k.astype(BF16)
        return asum + jnp.sum(jnp.abs(k), axis=0, keepdims=True)

    asum = lax.fori_loop(0, (2 * l) // rc, taps, jnp.zeros((1, ct), F32))
    scale = 1.0 / asum
    nwin = 2 * (l // p) - 1

    def win(w, _):
        seg = kk_scr[pl.ds(pl.multiple_of(w * p, p), 2 * p), :]
        o_ref[w] = jnp.dot(mg_scr[...], seg, preferred_element_type=F32) * scale
        return 0

    lax.fori_loop(0, nwin, win, 0)


def _hyena_filters(l, p, w1, b1, w2, b2, w3, b3, w4, freq, c, ct=128):
    nwin = 2 * (l // p) - 1
    nct = c // ct
    zp = jnp.asarray(_filter_positions(l))
    w1p = jnp.zeros((128, FILTER_HIDDEN), F32).at[:FILTER_EMB].set(w1)
    hid = _filter_hidden(zp, w1p, b1, w2, b2, w3, b3, freq)
    deltas = np.abs(np.linspace(math.log(DECAY_TARGET) / DECAY_SLOW, math.log(DECAY_TARGET) / DECAY_FAST, c))
    mg = jnp.asarray(_shifted_dft_tables(p)[1], F32)
    full = lambda shape: pl.BlockSpec(shape, lambda o, j: (0,) * len(shape))
    return pl.pallas_call(
        functools.partial(_filter_body, l, p),
        grid=(HYENA_ORDER, nct),
        in_specs=[full((2 * l, 1)), full((2 * l, FILTER_HIDDEN)),
                  pl.BlockSpec((FILTER_HIDDEN, ct), lambda o, j: (0, o * 2 * nct + j)),
                  pl.BlockSpec((FILTER_HIDDEN, ct), lambda o, j: (0, o * 2 * nct + nct + j)),
                  pl.BlockSpec((1, ct), lambda o, j: (0, j)),
                  full((2 * p, 2 * p))],
        out_specs=pl.BlockSpec((None, nwin, 2 * p, ct), lambda o, j: (o, 0, 0, j)),
        out_shape=jax.ShapeDtypeStruct((HYENA_ORDER, nwin, 2 * p, c), F32),
        scratch_shapes=[pltpu.VMEM((2 * l, ct), BF16), pltpu.VMEM((2 * p, 2 * p), BF16)],
        compiler_params=_cparams(("parallel", "parallel")),
        name=f"hyena_filter_{l}",
    )(zp[:, 0:1], hid, w4, w4, jnp.asarray(deltas, F32).reshape(1, c), mg)


_HY_RC = 64


def _hyena_body(nseq, l, p, x1_ref, x2_ref, v_ref, w1_ref, w2_ref, wv_ref, b1_ref, b2_ref, bv_ref, hb_ref,
                g_ref, fz_ref, fi_ref, o_ref, z_scr, c_scr, zh_scr, yh_scr, z1_scr, fz_scr, fi_scr):
    fz_scr[...] = fz_ref[...].astype(BF16)
    fi_scr[...] = fi_ref[...].astype(BF16)
    o = pl.program_id(1)
    s = pl.program_id(2)
    rows = nseq * l
    nb = l // p
    ct = o_ref.shape[1]
    pos = lax.rem(lax.broadcasted_iota(jnp.int32, (rows, ct), 0), l)

    def short_conv(u_ref, w_ref, b_ref):
        u = u_ref[...].astype(F32)
        um = jnp.where(pos == 0, 0.0, pltpu.roll(u, 1, 0))
        up = jnp.where(pos == l - 1, 0.0, pltpu.roll(u, rows - 1, 0))
        w = w_ref[...]
        return um * w[0:1] + u * w[1:2] + up * w[2:3] + b_ref[...]

    def long_conv():
        for q in range(nseq):
            base = q * l

            def fwd(j, _):
                zb = z_scr[pl.ds(pl.multiple_of(base + j * p, p), p), :].astype(BF16)
                zh_scr[j] = jnp.dot(fz_scr[...], zb, preferred_element_type=F32)
                return 0

            lax.fori_loop(0, nb, fwd, 0)

            def out_block(i, _):
                for rc in range(p // _HY_RC):
                    re = pl.ds(rc * _HY_RC, _HY_RC)
                    im = pl.ds(p + rc * _HY_RC, _HY_RC)

                    def acc(j, carry):
                        ar, ai = carry
                        w = i - j + (nb - 1)
                        gr, gi = g_ref[w, re, :], g_ref[w, im, :]
                        zr, zi = zh_scr[j, re, :], zh_scr[j, im, :]
                        return ar + gr * zr - gi * zi, ai + gr * zi + gi * zr

                    zero = jnp.zeros((_HY_RC, ct), F32)
                    ar, ai = lax.fori_loop(0, nb, acc, (zero, zero))
                    yh_scr[re, :] = ar
                    yh_scr[im, :] = ai
                c_scr[pl.ds(pl.multiple_of(base + i * p, p), p), :] = jnp.dot(
                    fi_scr[...], yh_scr[...].astype(BF16), preferred_element_type=F32)
                return 0

            lax.fori_loop(0, nb, out_block, 0)

    hb = hb_ref[...]

    @pl.when(o == 0)
    def _():
        v = short_conv(v_ref, wv_ref, bv_ref)
        z_scr[...] = v
        long_conv()
        x1 = short_conv(x1_ref, w1_ref, b1_ref)
        z1_scr[s] = x1 * (c_scr[...] + hb[0:1] * v)

    @pl.when(o == 1)
    def _():
        z1 = z1_scr[s]
        z_scr[...] = z1
        long_conv()
        x2 = short_conv(x2_ref, w2_ref, b2_ref)
        o_ref[...] = (x2 * (c_scr[...] + hb[1:2] * z1)).astype(BF16)


def _hyena(proj, l, p, nseq, short_w, short_b, hy_bias, ghat, c, ct=128):
    r = proj.shape[0]
    rows = nseq * l
    nsb = r // rows
    nct = c // ct
    nwin = ghat.shape[1]
    fwd, _, inv = _shifted_dft_tables(p)
    fz = jnp.asarray(fwd, F32)
    fi = jnp.asarray(inv, F32)
    sb = short_b.reshape(1, 3 * c)
    col = lambda k: (lambda j, o, s: (s, k * nct + j))
    wcol = lambda k: (lambda j, o, s: (0, k * nct + j))
    return pl.pallas_call(
        functools.partial(_hyena_body, nseq, l, p),
        grid=(nct, HYENA_ORDER, nsb),
        in_specs=[pl.BlockSpec((rows, ct), col(0)), pl.BlockSpec((rows, ct), col(1)), pl.BlockSpec((rows, ct), col(2)),
                  pl.BlockSpec((3, ct), wcol(0)), pl.BlockSpec((3, ct), wcol(1)), pl.BlockSpec((3, ct), wcol(2)),
                  pl.BlockSpec((1, ct), wcol(0)), pl.BlockSpec((1, ct), wcol(1)), pl.BlockSpec((1, ct), wcol(2)),
                  pl.BlockSpec((HYENA_ORDER, ct), lambda j, o, s: (0, j)),
                  pl.BlockSpec((None, nwin, 2 * p, ct), lambda j, o, s: (o, 0, 0, j)),
                  pl.BlockSpec((2 * p, p), lambda j, o, s: (0, 0)),
                  pl.BlockSpec((p, 2 * p), lambda j, o, s: (0, 0))],
        out_specs=pl.BlockSpec((rows, ct), lambda j, o, s: (s * o, j)),
        out_shape=jax.ShapeDtypeStruct((r, c), BF16),
        scratch_shapes=[pltpu.VMEM((rows, ct), F32), pltpu.VMEM((rows, ct), F32),
                        pltpu.VMEM((l // p, 2 * p, ct), F32), pltpu.VMEM((2 * p, ct), F32),
                        pltpu.VMEM((nsb, rows, ct), F32), pltpu.VMEM((2 * p, p), BF16), pltpu.VMEM((p, 2 * p), BF16)],
        compiler_params=_cparams(("parallel", "arbitrary", "arbitrary")),
        name=f"hyena_{l}",
    )(proj, proj, proj, short_w, short_w, short_w, sb, sb, sb, hy_bias, ghat, fz, fi)


def _s5_param_body(lre_ref, lim_ref, ldt_ref, btr_ref, bti_ref, cr_ref, ci_ref,
                   k_ref, wsr_ref, wsi_ref, wor_ref, woi_ref, etr_ref, eti_ref):
    t = S5_CHUNK
    backward = pl.program_id(0) == 1
    lr = jnp.minimum(lre_ref[...], LAMBDA_RE_MAX)
    li = lim_ref[...]
    dt = jnp.exp(ldt_ref[...])
    ar, ai = lr * dt, li * dt

    def power(e):
        mag = jnp.exp(e * ar)
        return mag * jnp.cos(e * ai), mag * jnp.sin(e * ai)

    one = jnp.ones((1, 1), F32)
    lbr, lbi = power(one)
    nr, ni = lbr - 1.0, lbi
    den = lr * lr + li * li
    qr, qi = (nr * lr + ni * li) / den, (ni * lr - nr * li) / den
    btr, bti = btr_ref[...], bti_ref[...]
    bbr, bbi = qr * btr - qi * bti, qr * bti + qi * btr
    cr, ci = cr_ref[...], ci_ref[...]

    step = lax.broadcasted_iota(jnp.int32, (t, 1, 1), 0).astype(F32)
    er, ei = power(step)
    cer, cei = cr[None] * er - ci[None] * ei, cr[None] * ei + ci[None] * er
    nt = (((1,), (1,)), ((), ()))
    dg = functools.partial(lax.dot_general, dimension_numbers=nt, precision=HIGHEST, preferred_element_type=F32)
    k_ref[...] = (dg(cer.reshape(t * SSM_GROUP, SSM_STATE), bbr)
                  - dg(cei.reshape(t * SSM_GROUP, SSM_STATE), bbi))
    er, ei = power(jnp.where(backward, step, (t - 1.0) - step))
    wsr_ref[...] = er * bbr[None] - ei * bbi[None]
    wsi_ref[...] = er * bbi[None] + ei * bbr[None]
    er, ei = power(jnp.where(backward, t - step, step + 1.0))
    wor_ref[...] = cr[None] * er - ci[None] * ei
    woi_ref[...] = -(cr[None] * ei + ci[None] * er)
    etr, eti = power(one * t)
    etr_ref[...] = etr
    eti_ref[...] = eti


def _s5_params(lam_re, lam_im, log_dt, b_re, b_im, c_re, c_im):
    g = lam_re.shape[1]
    t, cg, ps = S5_CHUNK, SSM_GROUP, SSM_STATE
    vec = lambda a: a.reshape(2, g, 1, ps)
    ldt = jnp.broadcast_to(log_dt[:, :, None, None], (2, g, 1, ps))
    bt = lambda a: jnp.swapaxes(a, -1, -2)
    dg_spec = lambda *shape: pl.BlockSpec((None, None) + shape, lambda d, i: (d, i) + (0,) * len(shape))
    outs = [jax.ShapeDtypeStruct((2, g, t * cg, cg), F32)] + \
           [jax.ShapeDtypeStruct((2, g, t, cg, ps), F32)] * 4 + [jax.ShapeDtypeStruct((2, g, 1, ps), F32)] * 2
    return pl.pallas_call(
        _s5_param_body,
        grid=(2, g),
        in_specs=[dg_spec(1, ps), dg_spec(1, ps), dg_spec(1, ps), dg_spec(cg, ps), dg_spec(cg, ps),
                  dg_spec(cg, ps), dg_spec(cg, ps)],
        out_specs=[dg_spec(t * cg, cg)] + [dg_spec(t, cg, ps)] * 4 + [dg_spec(1, ps)] * 2,
        out_shape=outs,
        compiler_params=_cparams(("arbitrary", "arbitrary")),
        name="s5_params",
    )(vec(lam_re), vec(lam_im), ldt, bt(b_re), bt(b_im), c_re, c_im)


def _s5_operators(lam_re, lam_im, log_dt, b_re, b_im, c_re, c_im):
    kk, wsr, wsi, wor, woi, etr, eti = _s5_params(lam_re, lam_im, log_dt, b_re, b_im, c_re, c_im)
    g = kk.shape[1]
    t, cg, ps = S5_CHUNK, SSM_GROUP, SSM_STATE
    gl = S5_TILE_GROUPS
    nb = g // gl
    sw = 4 * ps
    eye = np.eye(gl, dtype=bool)
    kd = jnp.swapaxes(kk.reshape(2, g, t, cg, cg), -1, -2)
    lagk = jnp.concatenate([kd[0][:, ::-1], kd[1][:, 1:]], axis=1)

    def lane_blocks(a):
        x = a.shape[1]
        a = a.reshape(nb, gl, x, cg, 1, cg)
        a = jnp.where(eye[None, :, None, None, :, None], a, 0.0)
        return jnp.transpose(a, (0, 2, 1, 3, 4, 5)).reshape(nb, x * gl * cg, gl * cg)

    bdr = lane_blocks(lagk)
    bd0 = lane_blocks(kd[1][:, 0:1])
    ws = jnp.concatenate([wsr[0], wsr[1], wsi[0], wsi[1]], axis=-1)
    ws = ws.reshape(nb, gl, t, cg, 1, sw)
    wx = jnp.where(eye[None, :, None, None, :, None], ws, 0.0)
    wx = jnp.transpose(wx, (0, 2, 1, 3, 4, 5)).reshape(nb, t * gl * cg, gl * sw)
    wo = jnp.concatenate([wor[0], wor[1], woi[0], woi[1]], axis=-1)
    wo = jnp.transpose(wo, (0, 3, 1, 2)).reshape(nb, gl, sw, t, 1, cg)
    w2 = jnp.where(eye[None, :, None, None, :, None], wo, 0.0)
    w2 = w2.reshape(nb, gl * sw, t * gl * cg)
    lam = jnp.concatenate([etr[0], etr[1], eti[0], eti[1]], axis=-1).reshape(nb, 1, gl * sw)
    return bdr.astype(BF16), bd0.astype(BF16), wx.astype(BF16), w2.astype(BF16), lam


def _s5_body(bsz, nk, chunk_major, u_ref, bdr_ref, bd0_ref, wx_ref, w2_ref, lam_ref, s0_ref, y_ref, fin_ref,
             tok_scr, u_scr, x_scr, p_scr, pb_scr):
    t_, lt, gl = S5_CHUNK, 128, S5_TILE_GROUPS
    half = SSM_STATE
    sw = 4 * half
    l = nk * t_
    cw = t_ * lt

    def token_rows(t):
        if chunk_major:
            return [(k * bsz, bsz, k * t_ + t, l) for k in range(nk)]
        return [(b * nk, nk, b * l + t, t_) for b in range(bsz)]

    tok_scr[...] = u_ref[...].astype(F32)
    for t in range(t_):
        for r0, n, tok0, stride in token_rows(t):
            u_scr[r0:r0 + n, t * lt:(t + 1) * lt] = tok_scr[pl.ds(tok0, n, stride=stride), :].astype(BF16)
    x_scr[...] = jnp.dot(u_scr[...], wx_ref[...], preferred_element_type=F32)
    p_scr[...] = jnp.zeros(p_scr.shape, F32)

    srows = bsz if chunk_major else 8
    nset = 1 if chunk_major else bsz
    s0_of = (lambda q: slice(0, bsz)) if chunk_major else (lambda q: slice(q, q + 1))
    fwd_lane = lax.broadcasted_iota(jnp.int32, (srows, 2 * half), 1) < half
    row_id = lax.broadcasted_iota(jnp.int32, (srows, 2 * half), 0)
    lam = lam_ref[...]
    cols_a = lambda gi: pl.ds(gi * sw, 2 * half)
    cols_b = lambda gi: pl.ds(gi * sw + 2 * half, 2 * half)
    lam_of = lambda gi: (lam[:, gi * sw:gi * sw + 2 * half], lam[:, gi * sw + 2 * half:(gi + 1) * sw])

    def step(t, carry):
        kf = pl.ds(pl.multiple_of(t * bsz, bsz), bsz)
        kb = pl.ds(pl.multiple_of((nk - 1 - t) * bsz, bsz), bsz)
        new = []
        for gi in range(gl):
            ca, cb = cols_a(gi), cols_b(gi)
            sr, si = carry[gi]
            p_scr[kf, ca] = jnp.where(fwd_lane, sr, p_scr[kf, ca])
            p_scr[kb, ca] = jnp.where(fwd_lane, p_scr[kb, ca], sr)
            p_scr[kf, cb] = jnp.where(fwd_lane, si, p_scr[kf, cb])
            p_scr[kb, cb] = jnp.where(fwd_lane, p_scr[kb, cb], si)
            xr = jnp.where(fwd_lane, x_scr[kf, ca], x_scr[kb, ca])
            xi = jnp.where(fwd_lane, x_scr[kf, cb], x_scr[kb, cb])
            lr, li = lam_of(gi)
            new.append((lr * sr - li * si + xr, lr * si + li * sr + xi))
        return tuple(new)

    def step8(t8, carry):
        nb8 = nk // 8
        new = []
        for q in range(bsz):
            kf = pl.ds(pl.multiple_of(q * nk + t8 * 8, 8), 8)
            kb = pl.ds(pl.multiple_of(q * nk + (nb8 - 1 - t8) * 8, 8), 8)
            for gi in range(gl):
                ca, cb = cols_a(gi), cols_b(gi)
                sr, si = carry[q * gl + gi]
                xrf, xrb, xif, xib = x_scr[kf, ca], x_scr[kb, ca], x_scr[kf, cb], x_scr[kb, cb]
                prf, prb, pif, pib = p_scr[kf, ca], p_scr[kb, ca], p_scr[kf, cb], p_scr[kb, cb]
                lr, li = lam_of(gi)
                for j in range(8):
                    at_f = (row_id == j) & fwd_lane
                    at_b = (row_id == 7 - j) & jnp.logical_not(fwd_lane)
                    prf, prb = jnp.where(at_f, sr, prf), jnp.where(at_b, sr, prb)
                    pif, pib = jnp.where(at_f, si, pif), jnp.where(at_b, si, pib)
                    row = lambda a, i: jnp.broadcast_to(a[i:i + 1], a.shape)
                    xr = jnp.where(fwd_lane, row(xrf, j), row(xrb, 7 - j))
                    xi = jnp.where(fwd_lane, row(xif, j), row(xib, 7 - j))
                    sr, si = lr * sr - li * si + xr, lr * si + li * sr + xi
                p_scr[kf, ca], p_scr[kb, ca], p_scr[kf, cb], p_scr[kb, cb] = prf, prb, pif, pib
                new.append((sr, si))
        return tuple(new)

    bcast = lambda a: jnp.broadcast_to(a, (srows, 2 * half))
    init = tuple((bcast(s0_ref[s0_of(q), gi * sw:gi * sw + 2 * half]),
                  bcast(s0_ref[s0_of(q), gi * sw + 2 * half:(gi + 1) * sw]))
                 for q in range(nset) for gi in range(gl))
    if chunk_major:
        fin = lax.fori_loop(0, nk, step, init)
    else:
        fin = lax.fori_loop(0, nk // 8, step8, init)
    nfin = bsz if chunk_major else 1
    for q in range(nset):
        for gi in range(gl):
            sr, si = fin[q * gl + gi]
            fin_ref[s0_of(q), gi * sw:gi * sw + 2 * half] = sr[0:nfin]
            fin_ref[s0_of(q), gi * sw + 2 * half:(gi + 1) * sw] = si[0:nfin]

    pb_scr[...] = p_scr[...].astype(BF16)
    for t in range(t_):
        yt = (jnp.dot(u_scr[...], bdr_ref[(t_ - 1 - t) * lt:(t_ - 1 - t) * lt + cw, :], preferred_element_type=F32)
              + jnp.dot(u_scr[:, t * lt:(t + 1) * lt], bd0_ref[...], preferred_element_type=F32)
              + jnp.dot(pb_scr[...], w2_ref[:, t * lt:(t + 1) * lt], preferred_element_type=F32))
        for r0, n, tok0, stride in token_rows(t):
            tok_scr[pl.ds(tok0, n, stride=stride), :] = yt[r0:r0 + n]
    y_ref[...] = tok_scr[...].astype(BF16)


def _s5(proj, col0, bsz, l, ops, s0):
    bdr, bd0, wx, w2, lam = ops
    nb = bdr.shape[0]
    t, ps, gl, lt = S5_CHUNK, SSM_STATE, S5_TILE_GROUPS, 128
    g = nb * gl
    nk = l // t
    r = bsz * l
    nrows = bsz * nk
    sw = 4 * ps
    cw = t * lt
    chunk_major = bsz % 8 == 0
    if s0 is None:
        s0p = jnp.zeros((bsz, g * sw), F32)
    else:
        s0_re, s0_im = s0
        s0p = jnp.concatenate([s0_re[:, 0], s0_re[:, 1], s0_im[:, 0], s0_im[:, 1]], axis=-1).reshape(bsz, g * sw)
    blk = lambda *shape: _resident((None,) + shape, lambda i: (i,) + (0,) * len(shape))
    y, fin = pl.pallas_call(
        functools.partial(_s5_body, bsz, nk, chunk_major),
        grid=(nb,),
        in_specs=[pl.BlockSpec((r, lt), lambda i: (0, col0 // lt + i)),
                  blk(bdr.shape[1], lt), blk(lt, lt), blk(cw, gl * sw), blk(gl * sw, cw), blk(1, gl * sw),
                  pl.BlockSpec((bsz, gl * sw), lambda i: (0, i))],
        out_specs=[pl.BlockSpec((r, lt), lambda i: (0, i)), pl.BlockSpec((bsz, gl * sw), lambda i: (0, i))],
        out_shape=[jax.ShapeDtypeStruct((r, g * SSM_GROUP), BF16), jax.ShapeDtypeStruct((bsz, g * sw), F32)],
        scratch_shapes=[pltpu.VMEM((r, lt), F32), pltpu.VMEM((nrows, cw), BF16), pltpu.VMEM((nrows, gl * sw), F32),
                        pltpu.VMEM((nrows, gl * sw), F32), pltpu.VMEM((nrows, gl * sw), BF16)],
        compiler_params=_cparams(("parallel",)),
        name=f"s5_scan_{nk}",
    )(proj, bdr, bd0, wx, w2, lam, s0p)
    fin = jnp.transpose(fin.reshape(bsz, g, 2, 2, ps), (2, 0, 3, 1, 4))
    return y, fin[0], fin[1]


def _merge_body(yh_ref, ys_ref, u_ref, gh_ref, gs_ref, x_ref, d_ref, gw_ref, gbias_ref, wbh_ref, wbs_ref, wo_ref,
                g1_ref, n2_ref, sc2_ref, sh2_ref, x1_ref, h2_ref):
    y = ys_ref[...].astype(F32) + u_ref[...].astype(F32) * d_ref[...]
    g = jax.nn.gelu(y)
    gl = jnp.dot(g.astype(BF16), gw_ref[...], preferred_element_type=F32) + gbias_ref[...]
    ys = (g * jax.nn.sigmoid(gl)).astype(BF16)
    bh = jnp.dot(yh_ref[...], wbh_ref[...], preferred_element_type=F32)
    bs = jnp.dot(ys, wbs_ref[...], preferred_element_type=F32)
    merged = gh_ref[...].astype(F32) * bh + gs_ref[...].astype(F32) * bs
    x1 = x_ref[...] + g1_ref[...] * jnp.dot(merged.astype(BF16), wo_ref[...], preferred_element_type=F32)
    x1_ref[...] = x1
    h2_ref[...] = _mod_rmsnorm(x1, n2_ref[...], sc2_ref[...], sh2_ref[...]).astype(BF16)


def _merge(yh, ys, proj, x, mod4, cond_of_tile, ssm_d, glu_w, glu_b, wbh, wbs, wo, n2g, c_h, c_s, tm=256):
    r, d = x.shape
    row_blk = lambda width, k: pl.BlockSpec((tm, width), lambda i: (i, k))
    mod_spec = lambda chunk: pl.BlockSpec((None, None, 1, d), lambda i: (cond_of_tile(i, tm), chunk, 0, 0))
    const = lambda shape: _resident(shape, lambda i: (0,) * len(shape))
    u_col = (3 * c_h) // c_s
    gate0 = (3 * c_h + c_s) // d
    return pl.pallas_call(
        _merge_body,
        grid=(r // tm,),
        in_specs=[row_blk(c_h, 0), row_blk(c_s, 0), row_blk(c_s, u_col), row_blk(d, gate0), row_blk(d, gate0 + 1),
                  row_blk(d, 0), const((1, c_s)), const((c_s, c_s)), const((1, c_s)), const((c_h, d)),
                  const((c_s, d)), const((d, d)), mod_spec(2), const((1, d)), mod_spec(4), mod_spec(3)],
        out_specs=[row_blk(d, 0), row_blk(d, 0)],
        out_shape=[jax.ShapeDtypeStruct((r, d), F32), jax.ShapeDtypeStruct((r, d), BF16)],
        compiler_params=_cparams(("parallel",)),
        name="merge",
    )(yh, ys, proj, proj, proj, x, ssm_d.reshape(1, c_s), glu_w, glu_b.reshape(1, c_s), wbh, wbs, wo,
      mod4, n2g.reshape(1, d), mod4, mod4)


_FFN_RC = 512


def _ffn_up_body(rows, cols, h_ref, wa_ref, wb_ref, cwa_ref, cwb_ref, cba_ref, cbb_ref, o_ref):
    tm, tn = o_ref.shape
    rc = _FFN_RC
    n = tm // rc
    col = lax.rem(lax.broadcasted_iota(jnp.int32, (rc, tn), 0), cols)
    not_first = col != 0
    not_last = col != cols - 1
    zero = jnp.zeros((cols, tn), F32)

    def taps(hc, w_ref, cw_ref, cb_ref):
        a = jnp.dot(hc, w_ref[...], preferred_element_type=F32)
        am = jnp.where(not_first, pltpu.roll(a, 1, 0), 0.0)
        ap = jnp.where(not_last, pltpu.roll(a, rc - 1, 0), 0.0)
        cw = cw_ref[...]
        tap = lambda i: am * cw[3 * i:3 * i + 1] + a * cw[3 * i + 1:3 * i + 2] + ap * cw[3 * i + 2:3 * i + 3]
        mid = tap(1) + cb_ref[...]
        return (tap(0), mid, tap(2)) if rows > 1 else (None, mid, None)

    def finish(prev, cur, nxt):
        t0, mid, t2 = cur
        if rows == 1:
            return mid
        above = jnp.concatenate([prev[0][rc - cols:] if prev is not None else zero, t0[:rc - cols]], axis=0)
        below = jnp.concatenate([t2[cols:], nxt[2][:cols] if nxt is not None else zero], axis=0)
        return mid + above + below

    chunks = []
    for i in range(n + 1):
        if i < n:
            hc = h_ref[i * rc:(i + 1) * rc, :]
            chunks.append((taps(hc, wa_ref, cwa_ref, cba_ref), taps(hc, wb_ref, cwb_ref, cbb_ref)))
        if i >= 1:
            j = i - 1
            pick = lambda k, side: chunks[k][side] if 0 <= k < n else None
            a = finish(pick(j - 1, 0), pick(j, 0), pick(j + 1, 0))
            b = finish(pick(j - 1, 1), pick(j, 1), pick(j + 1, 1))
            o_ref[j * rc:(j + 1) * rc, :] = (jax.nn.gelu(a) * b).astype(BF16)


def _ffn_up(h2, w_up, conv_w, conv_b, rows, cols, tn=256):
    r, d = h2.shape
    f = w_up.shape[1] // 2
    tm = rows * cols if rows > 1 else 4096
    assert tm % _FFN_RC == 0 and _FFN_RC % cols == 0 and cols % 8 == 0
    nt = f // tn
    cw = conv_w.reshape(9, 2 * f)
    cb = conv_b.reshape(1, 2 * f)
    return pl.pallas_call(
        functools.partial(_ffn_up_body, rows, cols),
        grid=(r // tm, nt),
        in_specs=[pl.BlockSpec((tm, d), lambda i, j: (i, 0), pipeline_mode=pl.Buffered(1)),
                  pl.BlockSpec((d, tn), lambda i, j: (0, j)), pl.BlockSpec((d, tn), lambda i, j: (0, nt + j)),
                  pl.BlockSpec((9, tn), lambda i, j: (0, j)), pl.BlockSpec((9, tn), lambda i, j: (0, nt + j)),
                  pl.BlockSpec((1, tn), lambda i, j: (0, j)), pl.BlockSpec((1, tn), lambda i, j: (0, nt + j))],
        out_specs=pl.BlockSpec((tm, tn), lambda i, j: (i, j)),
        out_shape=jax.ShapeDtypeStruct((r, f), BF16),
        compiler_params=_cparams(("parallel", "arbitrary")),
        name=f"ffn_up_{rows}x{cols}",
    )(h2, w_up, w_up, cw, cw, cb, cb)


def _ffn_down_body(a_ref, w_ref, x_ref, g2_ref, fg_ref, o_ref):
    x2 = x_ref[...] + g2_ref[...] * jnp.dot(a_ref[...], w_ref[...], preferred_element_type=F32)
    ms = jnp.mean(x2 * x2, axis=-1, keepdims=True)
    o_ref[...] = x2 * lax.rsqrt(ms + EPS) * fg_ref[...]


def _ffn_down(act, w, x1, mod4, cond_of_tile, final_g, tm=256):
    r, f = act.shape
    d = w.shape[1]
    return pl.pallas_call(
        _ffn_down_body,
        grid=(r // tm,),
        in_specs=[pl.BlockSpec((tm, f), lambda i: (i, 0)),
                  _resident((f, d), lambda i: (0, 0)),
                  pl.BlockSpec((tm, d), lambda i: (i, 0)),
                  pl.BlockSpec((None, None, 1, d), lambda i: (cond_of_tile(i, tm), 5, 0, 0)),
                  _resident((1, d), lambda i: (0, 0))],
        out_specs=pl.BlockSpec((tm, d), lambda i: (i, 0)),
        out_shape=jax.ShapeDtypeStruct((r, d), F32),
        compiler_params=_cparams(("parallel",)),
        name="ffn_down",
    )(act, w, x1, mod4, final_g.reshape(1, d))


def _segment(x, cond_base, per_batch_cond, s0, rows, cols, hy_block, hy_nseq, mod4, p, final_g):
    bsz, l, d = x.shape
    xf = x.reshape(bsz * l, d)
    c_h = p["wbh"].shape[0]
    c_s = p["wbs"].shape[0]

    def cond_of_tile(i, tm):
        return cond_base + (i * tm) // l if per_batch_cond else cond_base

    proj = _in_proj(xf, mod4, cond_of_tile, p["norm1_g"], p["w_in"], p["b_in"], 3 * c_h + c_s)
    yh = _hyena(proj, l, hy_block, hy_nseq, p["hy_short_w"], p["hy_short_b"], p["hy_bias"], p["ghat"][l], c_h)
    ys, fin_re, fin_im = _s5(proj, 3 * c_h, bsz, l, p["s5_ops"], s0)
    x1, h2 = _merge(yh, ys, proj, xf, mod4, cond_of_tile, p["ssm_d"], p["glu_w"], p["glu_b"],
                    p["wbh"], p["wbs"], p["wo"], p["norm2_g"], c_h, c_s)
    act = _ffn_up(h2, p["ffn_up"], p["ffn_conv_w"], p["ffn_conv_b"], rows, cols)
    y = _ffn_down(act, p["ffn_down"], x1, mod4, cond_of_tile, final_g)
    return y.reshape(bsz, l, d), fin_re, fin_im


def kernel(x_prompt, x_sample, state_ssm_re, state_ssm_im, c, c_ctx, ada_w, ada_b, norm1_g, norm2_g, final_g, w_in, b_in, hy_short_w, hy_short_b, hy_f_w1, hy_f_b1, hy_f_w2, hy_f_b2, hy_f_w3, hy_f_b3, hy_f_w4, hy_f_freq, hy_bias, w_branch_h, ssm_lambda_re, ssm_lambda_im, ssm_log_dt, ssm_b_re, ssm_b_im, ssm_c_re, ssm_c_im, ssm_d, ssm_glu_w, ssm_glu_b, w_branch_s, w_out, ffn_up, ffn_conv_w, ffn_conv_b, ffn_down):
    depth = ada_w.shape[0]
    assert depth == 1, "the final norm is fused into the (single) layer's last kernel"
    d = x_prompt.shape[-1]
    ctx_len = x_prompt.shape[1]
    lat_len = x_sample.shape[1]
    dec_b = x_sample.shape[0]
    c_h = w_branch_h.shape[1]
    assert 1 + dec_b <= 8
    cond = jnp.zeros((8, d), F32).at[0].set(c_ctx).at[1:1 + dec_b].set(c)
    lay = 0
    mod4 = _ada(cond, ada_w[lay], ada_b[lay]).reshape(8, N_MOD, 1, d)
    hy_blocks = {ctx_len: 256, lat_len: 512}
    ghat = {l: _hyena_filters(l, hy_blocks[l], hy_f_w1[lay], hy_f_b1[lay], hy_f_w2[lay], hy_f_b2[lay],
                              hy_f_w3[lay], hy_f_b3[lay], hy_f_w4[lay], hy_f_freq[lay], c_h)
            for l in (ctx_len, lat_len)}
    s5_ops = _s5_operators(ssm_lambda_re[lay], ssm_lambda_im[lay], ssm_log_dt[lay],
                           ssm_b_re[lay], ssm_b_im[lay], ssm_c_re[lay], ssm_c_im[lay])
    p = dict(norm1_g=norm1_g[lay], norm2_g=norm2_g[lay], w_in=w_in[lay].astype(BF16), b_in=b_in[lay],
             hy_short_w=hy_short_w[lay], hy_short_b=hy_short_b[lay], hy_bias=hy_bias[lay], ghat=ghat,
             s5_ops=s5_ops, ssm_d=ssm_d[lay], glu_w=ssm_glu_w[lay].astype(BF16),
             glu_b=ssm_glu_b[lay], wbh=w_branch_h[lay].astype(BF16), wbs=w_branch_s[lay].astype(BF16),
             wo=w_out[lay].astype(BF16), ffn_up=ffn_up[lay].astype(BF16), ffn_conv_w=ffn_conv_w[lay],
             ffn_conv_b=ffn_conv_b[lay], ffn_down=ffn_down[lay].astype(BF16))
    y_prompt, st_re, st_im = _segment(x_prompt, 0, False, None, 1, ctx_len, hy_blocks[ctx_len], 8, mod4, p, final_g)
    s0 = (state_ssm_re[:, lay].astype(F32), state_ssm_im[:, lay].astype(F32))
    y_sample, _, _ = _segment(x_sample, 1, True, s0, lat_len // GRID_W, GRID_W, hy_blocks[lat_len], 1, mod4, p, final_g)
    return y_prompt, y_sample, st_re[:, None], st_im[:, None]
```

```python
import functools
import math

import jax
import jax.numpy as jnp
import numpy as np
from jax import lax
from jax.experimental import pallas as pl
from jax.experimental.pallas import tpu as pltpu

F32 = jnp.float32
BF16 = jnp.bfloat16
HIGHEST = lax.Precision.HIGHEST

GRID_W = 64
EPS = 1e-6
SSM_GROUP = 16
SSM_STATE = 64
FILTER_EMB = 33
FILTER_HIDDEN = 64
HYENA_ORDER = 2
DECAY_FAST = 0.3
DECAY_SLOW = 1.5
DECAY_TARGET = 1e-2
LAMBDA_RE_MAX = -1e-4

S5_CHUNK = 16
S5_TILE_GROUPS = 8
V7X_VMEM_LIMIT = 56 * 2**20
N_MOD = 6


def _cparams(sem):
    return pltpu.CompilerParams(dimension_semantics=sem, vmem_limit_bytes=V7X_VMEM_LIMIT)


def _resident(shape, index_map):
    return pl.BlockSpec(shape, index_map, pipeline_mode=pl.Buffered(1))


def _mod_rmsnorm(x, g, sc, sh):
    ms = jnp.mean(x * x, axis=-1, keepdims=True)
    return x * lax.rsqrt(ms + EPS) * g * (1.0 + sc) + sh


def _ada_body(c_ref, w_ref, b_ref, o_ref):
    c = c_ref[...]
    s = c * jax.nn.sigmoid(c)
    o_ref[...] = jnp.dot(s, w_ref[...], precision=HIGHEST, preferred_element_type=F32) + b_ref[...]


def _ada(cond, w, b):
    d, n = w.shape
    tn = 1024
    return pl.pallas_call(
        _ada_body,
        grid=(n // tn,),
        in_specs=[pl.BlockSpec((8, d), lambda j: (0, 0)),
                  pl.BlockSpec((d, tn), lambda j: (0, j)),
                  pl.BlockSpec((1, tn), lambda j: (0, j))],
        out_specs=pl.BlockSpec((8, tn), lambda j: (0, j)),
        out_shape=jax.ShapeDtypeStruct((8, n), F32),
        compiler_params=_cparams(("parallel",)),
        name="ada",
    )(cond, w, b.reshape(1, n))


def _inproj_body(n_plain, x_ref, g_ref, sc_ref, sh_ref, w_ref, b_ref, o_ref, h_scr):
    j = pl.program_id(1)

    @pl.when(j == 0)
    def _():
        h_scr[...] = _mod_rmsnorm(x_ref[...], g_ref[...], sc_ref[...], sh_ref[...]).astype(BF16)

    acc = jnp.dot(h_scr[...], w_ref[...], preferred_element_type=F32) + b_ref[...]

    @pl.when(j < n_plain)
    def _():
        o_ref[...] = acc.astype(BF16)

    @pl.when(j >= n_plain)
    def _():
        o_ref[...] = jax.nn.sigmoid(acc).astype(BF16)


def _in_proj(x, mod4, cond_of_tile, g, w, b, n_plain_cols, tm=1024, tn=1024):
    r, d = x.shape
    n = w.shape[1]
    mod_spec = lambda chunk: pl.BlockSpec((None, None, 1, d), lambda i, j: (cond_of_tile(i, tm), chunk, 0, 0))
    return pl.pallas_call(
        functools.partial(_inproj_body, n_plain_cols // tn),
        grid=(r // tm, n // tn),
        in_specs=[pl.BlockSpec((tm, d), lambda i, j: (i, 0)),
                  pl.BlockSpec((1, d), lambda i, j: (0, 0)),
                  mod_spec(1), mod_spec(0),
                  pl.BlockSpec((d, tn), lambda i, j: (0, j)),
                  pl.BlockSpec((1, tn), lambda i, j: (0, j))],
        out_specs=pl.BlockSpec((tm, tn), lambda i, j: (i, j)),
        out_shape=jax.ShapeDtypeStruct((r, n), BF16),
        scratch_shapes=[pltpu.VMEM((tm, d), BF16)],
        compiler_params=_cparams(("parallel", "arbitrary")),
        name="in_proj",
    )(x, g.reshape(1, d), mod4, mod4, w, b.reshape(1, n))


def _shifted_dft_tables(p):
    theta = np.pi * (2.0 * np.arange(p) + 1.0) / (2.0 * p)
    n = np.arange(p)
    a = theta[:, None] * n[None, :]
    fwd = np.concatenate([np.cos(a), -np.sin(a)], axis=0)
    m = np.arange(2 * p) - p
    am = theta[:, None] * m[None, :]
    filt = np.concatenate([np.cos(am), -np.sin(am)], axis=0)
    filt[:, 0] = 0.0
    inv = np.concatenate([np.cos(a).T, -np.sin(a).T], axis=1) / p
    return fwd, filt, inv


_FILTER_RC = 1024


def _filter_positions(l):
    pos = np.abs(np.arange(2 * l) - l).astype(np.float64)
    t = pos / (l - 1)
    bands = (FILTER_EMB - 1) // 2
    fr = np.linspace(1e-4, bands - 1, bands)
    ang = (2.0 * math.pi / l) * pos[:, None] * fr[None, :]
    z = np.concatenate([t[:, None], np.cos(ang), -np.sin(ang)], axis=-1)
    zp = np.zeros((2 * l, 128), np.float32)
    zp[:, :FILTER_EMB] = z
    return zp


def _filter_hidden_body(z_ref, w1_ref, b1_ref, w2_ref, b2_ref, w3_ref, b3_ref, fq_ref, o_ref):
    fq = fq_ref[...]
    dot = functools.partial(jnp.dot, precision=HIGHEST, preferred_element_type=F32)
    h = jnp.sin(fq * (dot(z_ref[...], w1_ref[...]) + b1_ref[...]))
    h = jnp.sin(fq * (dot(h, w2_ref[...]) + b2_ref[...]))
    o_ref[...] = jnp.sin(fq * (dot(h, w3_ref[...]) + b3_ref[...]))


def _filter_hidden(zp, w1p, b1, w2, b2, w3, b3, freq):
    rows = zp.shape[0]
    rc = min(rows, _FILTER_RC)
    hd = w2.shape[0]
    full = lambda shape: pl.BlockSpec(shape, lambda i: (0,) * len(shape))
    row = lambda a: a.reshape(1, -1)
    return pl.pallas_call(
        _filter_hidden_body,
        grid=(rows // rc,),
        in_specs=[pl.BlockSpec((rc, 128), lambda i: (i, 0)), full((128, hd)), full((1, hd)), full((hd, hd)),
                  full((1, hd)), full((hd, hd)), full((1, hd)), full((1, hd))],
        out_specs=pl.BlockSpec((rc, hd), lambda i: (i, 0)),
        out_shape=jax.ShapeDtypeStruct((rows, hd), F32),
        compiler_params=_cparams(("parallel",)),
        name=f"hyena_filter_mlp_{rows}",
    )(zp, w1p, row(b1), w2, row(b2), w3, row(b3), row(freq))


def _filter_body(l, p, h_ref, w4f_ref, w4b_ref, dl_ref, mg_ref, o_ref, kk_scr, mg_scr):
    mg_scr[...] = mg_ref[...].astype(BF16)
    dot = functools.partial(jnp.dot, precision=HIGHEST, preferred_element_type=F32)
    rc = min(2 * l, _FILTER_RC)
    ct = kk_scr.shape[1]

    def taps(i, asum):
        r0 = pl.multiple_of(i * rc, rc)
        h = h_ref[pl.ds(r0, rc), :]
        q = r0 + lax.broadcasted_iota(jnp.int32, (rc, ct), 0)
        k = jnp.where(q >= l, dot(h, w4f_ref[...]), dot(h, w4b_ref[...]))
        t = jnp.abs(q - l).astype(F32) * (1.0 / (l - 1))
        k = jnp.where(q == 0, 0.0, k) * jnp.exp(-t * dl_ref[...])
        kk_scr[pl.ds(r0, rc), :] = k.astype(BF16)
        return asum + jnp.sum(jnp.abs(k), axis=0, keepdims=True)

    asum = lax.fori_loop(0, (2 * l) // rc, taps, jnp.zeros((1, ct), F32))
    scale = 1.0 / asum
    nwin = 2 * (l // p) - 1

    def win(w, _):
        seg = kk_scr[pl.ds(pl.multiple_of(w * p, p), 2 * p), :]
        o_ref[w] = jnp.dot(mg_scr[...], seg, preferred_element_type=F32) * scale
        return 0

    lax.fori_loop(0, nwin, win, 0)


def _hyena_filters(l, p, w1, b1, w2, b2, w3, b3, w4, freq, c, ct=256):
    nwin = 2 * (l // p) - 1
    nct = c // ct
    zp = jnp.asarray(_filter_positions(l))
    w1p = jnp.zeros((128, FILTER_HIDDEN), F32).at[:FILTER_EMB].set(w1)
    hid = _filter_hidden(zp, w1p, b1, w2, b2, w3, b3, freq)
    deltas = np.abs(np.linspace(math.log(DECAY_TARGET) / DECAY_SLOW, math.log(DECAY_TARGET) / DECAY_FAST, c))
    mg = jnp.asarray(_shifted_dft_tables(p)[1], F32)
    full = lambda shape: pl.BlockSpec(shape, lambda o, j: (0,) * len(shape))
    return pl.pallas_call(
        functools.partial(_filter_body, l, p),
        grid=(HYENA_ORDER, nct),
        in_specs=[full((2 * l, FILTER_HIDDEN)),
                  pl.BlockSpec((FILTER_HIDDEN, ct), lambda o, j: (0, o * 2 * nct + j)),
                  pl.BlockSpec((FILTER_HIDDEN, ct), lambda o, j: (0, o * 2 * nct + nct + j)),
                  pl.BlockSpec((1, ct), lambda o, j: (0, j)),
                  full((2 * p, 2 * p))],
        out_specs=pl.BlockSpec((None, nwin, 2 * p, ct), lambda o, j: (o, 0, 0, j)),
        out_shape=jax.ShapeDtypeStruct((HYENA_ORDER, nwin, 2 * p, c), F32),
        scratch_shapes=[pltpu.VMEM((2 * l, ct), BF16), pltpu.VMEM((2 * p, 2 * p), BF16)],
        compiler_params=_cparams(("parallel", "parallel")),
        name=f"hyena_filter_{l}",
    )(hid, w4, w4, jnp.asarray(deltas, F32).reshape(1, c), mg)


_HY_RC = 64


def _hyena_body(nseq, l, p, x1_ref, x2_ref, v_ref, w1_ref, w2_ref, wv_ref, b1_ref, b2_ref, bv_ref, hb_ref,
                g_ref, fz_ref, fi_ref, o_ref, z_scr, c_scr, zh_scr, yh_scr, z1_scr, fz_scr, fi_scr):
    fz_scr[...] = fz_ref[...].astype(BF16)
    fi_scr[...] = fi_ref[...].astype(BF16)
    o = pl.program_id(1)
    s = pl.program_id(2)
    rows = nseq * l
    nb = l // p
    ct = o_ref.shape[1]
    pos = lax.rem(lax.broadcasted_iota(jnp.int32, (rows, ct), 0), l)

    def short_conv(u_ref, w_ref, b_ref):
        u = u_ref[...].astype(F32)
        um = jnp.where(pos == 0, 0.0, pltpu.roll(u, 1, 0))
        up = jnp.where(pos == l - 1, 0.0, pltpu.roll(u, rows - 1, 0))
        w = w_ref[...]
        return um * w[0:1] + u * w[1:2] + up * w[2:3] + b_ref[...]

    def long_conv():
        for q in range(nseq):
            base = q * l

            def fwd(j, _):
                zb = z_scr[pl.ds(pl.multiple_of(base + j * p, p), p), :].astype(BF16)
                zh_scr[j] = jnp.dot(fz_scr[...], zb, preferred_element_type=F32)
                return 0

            lax.fori_loop(0, nb, fwd, 0)

            def out_block(i, _):
                for rc in range(p // _HY_RC):
                    re = pl.ds(rc * _HY_RC, _HY_RC)
                    im = pl.ds(p + rc * _HY_RC, _HY_RC)

                    def acc(j, carry):
                        ar, ai = carry
                        w = i - j + (nb - 1)
                        gr, gi = g_ref[w, re, :], g_ref[w, im, :]
                        zr, zi = zh_scr[j, re, :], zh_scr[j, im, :]
                        return ar + gr * zr - gi * zi, ai + gr * zi + gi * zr

                    zero = jnp.zeros((_HY_RC, ct), F32)
                    ar, ai = lax.fori_loop(0, nb, acc, (zero, zero), unroll=True)
                    yh_scr[re, :] = ar
                    yh_scr[im, :] = ai
                c_scr[pl.ds(pl.multiple_of(base + i * p, p), p), :] = jnp.dot(
                    fi_scr[...], yh_scr[...].astype(BF16), preferred_element_type=F32)
                return 0

            lax.fori_loop(0, nb, out_block, 0)

    hb = hb_ref[...]

    @pl.when(o == 0)
    def _():
        v = short_conv(v_ref, wv_ref, bv_ref)
        z_scr[...] = v
        long_conv()
        x1 = short_conv(x1_ref, w1_ref, b1_ref)
        z1_scr[s] = x1 * (c_scr[...] + hb[0:1] * v)

    @pl.when(o == 1)
    def _():
        z1 = z1_scr[s]
        z_scr[...] = z1
        long_conv()
        x2 = short_conv(x2_ref, w2_ref, b2_ref)
        o_ref[...] = (x2 * (c_scr[...] + hb[1:2] * z1)).astype(BF16)


def _hyena(proj, l, p, nseq, short_w, short_b, hy_bias, ghat, c, ct=128):
    r = proj.shape[0]
    rows = nseq * l
    nsb = r // rows
    nct = c // ct
    nwin = ghat.shape[1]
    fwd, _, inv = _shifted_dft_tables(p)
    fz = jnp.asarray(fwd, F32)
    fi = jnp.asarray(inv, F32)
    sb = short_b.reshape(1, 3 * c)
    col = lambda k: (lambda j, o, s: (s, k * nct + j))
    wcol = lambda k: (lambda j, o, s: (0, k * nct + j))
    return pl.pallas_call(
        functools.partial(_hyena_body, nseq, l, p),
        grid=(nct, HYENA_ORDER, nsb),
        in_specs=[pl.BlockSpec((rows, ct), col(0)), pl.BlockSpec((rows, ct), col(1)), pl.BlockSpec((rows, ct), col(2)),
                  pl.BlockSpec((3, ct), wcol(0)), pl.BlockSpec((3, ct), wcol(1)), pl.BlockSpec((3, ct), wcol(2)),
                  pl.BlockSpec((1, ct), wcol(0)), pl.BlockSpec((1, ct), wcol(1)), pl.BlockSpec((1, ct), wcol(2)),
                  pl.BlockSpec((HYENA_ORDER, ct), lambda j, o, s: (0, j)),
                  pl.BlockSpec((None, nwin, 2 * p, ct), lambda j, o, s: (o, 0, 0, j)),
                  pl.BlockSpec((2 * p, p), lambda j, o, s: (0, 0)),
                  pl.BlockSpec((p, 2 * p), lambda j, o, s: (0, 0))],
        out_specs=pl.BlockSpec((rows, ct), lambda j, o, s: (s * o, j)),
        out_shape=jax.ShapeDtypeStruct((r, c), BF16),
        scratch_shapes=[pltpu.VMEM((rows, ct), F32), pltpu.VMEM((rows, ct), F32),
                        pltpu.VMEM((l // p, 2 * p, ct), F32), pltpu.VMEM((2 * p, ct), F32),
                        pltpu.VMEM((nsb, rows, ct), F32), pltpu.VMEM((2 * p, p), BF16), pltpu.VMEM((p, 2 * p), BF16)],
        compiler_params=_cparams(("parallel", "arbitrary", "arbitrary")),
        name=f"hyena_{l}",
    )(proj, proj, proj, short_w, short_w, short_w, sb, sb, sb, hy_bias, ghat, fz, fi)


def _s5_param_body(lre_ref, lim_ref, ldt_ref, btr_ref, bti_ref, cr_ref, ci_ref,
                   k_ref, wsr_ref, wsi_ref, wor_ref, woi_ref, etr_ref, eti_ref):
    t = S5_CHUNK
    backward = pl.program_id(0) == 1
    lr = jnp.minimum(lre_ref[...], LAMBDA_RE_MAX)
    li = lim_ref[...]
    dt = jnp.exp(ldt_ref[...])
    ar, ai = lr * dt, li * dt

    def power(e):
        mag = jnp.exp(e * ar)
        return mag * jnp.cos(e * ai), mag * jnp.sin(e * ai)

    one = jnp.ones((1, 1), F32)
    lbr, lbi = power(one)
    nr, ni = lbr - 1.0, lbi
    den = lr * lr + li * li
    qr, qi = (nr * lr + ni * li) / den, (ni * lr - nr * li) / den
    btr, bti = btr_ref[...], bti_ref[...]
    bbr, bbi = qr * btr - qi * bti, qr * bti + qi * btr
    cr, ci = cr_ref[...], ci_ref[...]

    step = lax.broadcasted_iota(jnp.int32, (t, 1, 1), 0).astype(F32)
    er, ei = power(step)
    cer, cei = cr[None] * er - ci[None] * ei, cr[None] * ei + ci[None] * er
    nt = (((1,), (1,)), ((), ()))
    dg = functools.partial(lax.dot_general, dimension_numbers=nt, precision=HIGHEST, preferred_element_type=F32)
    k_ref[...] = (dg(cer.reshape(t * SSM_GROUP, SSM_STATE), bbr)
                  - dg(cei.reshape(t * SSM_GROUP, SSM_STATE), bbi))
    er, ei = power(jnp.where(backward, step, (t - 1.0) - step))
    wsr_ref[...] = er * bbr[None] - ei * bbi[None]
    wsi_ref[...] = er * bbi[None] + ei * bbr[None]
    er, ei = power(jnp.where(backward, t - step, step + 1.0))
    wor_ref[...] = cr[None] * er - ci[None] * ei
    woi_ref[...] = -(cr[None] * ei + ci[None] * er)
    etr, eti = power(one * t)
    etr_ref[...] = etr
    eti_ref[...] = eti


def _s5_param_tile_body(*refs):
    for gi in range(refs[0].shape[0]):
        _s5_param_body(*(r.at[gi] for r in refs))


def _s5_params(lam_re, lam_im, log_dt, b_re, b_im, c_re, c_im):
    g = lam_re.shape[1]
    gb = S5_TILE_GROUPS
    t, cg, ps = S5_CHUNK, SSM_GROUP, SSM_STATE
    vec = lambda a: a.reshape(2, g, 1, ps)
    ldt = jnp.broadcast_to(log_dt[:, :, None, None], (2, g, 1, ps))
    bt = lambda a: jnp.swapaxes(a, -1, -2)
    dg_spec = lambda *shape: pl.BlockSpec((None, gb) + shape, lambda d, i: (d, i) + (0,) * len(shape))
    outs = [jax.ShapeDtypeStruct((2, g, t * cg, cg), F32)] + \
           [jax.ShapeDtypeStruct((2, g, t, cg, ps), F32)] * 4 + [jax.ShapeDtypeStruct((2, g, 1, ps), F32)] * 2
    return pl.pallas_call(
        _s5_param_tile_body,
        grid=(2, g // gb),
        in_specs=[dg_spec(1, ps), dg_spec(1, ps), dg_spec(1, ps), dg_spec(cg, ps), dg_spec(cg, ps),
                  dg_spec(cg, ps), dg_spec(cg, ps)],
        out_specs=[dg_spec(t * cg, cg)] + [dg_spec(t, cg, ps)] * 4 + [dg_spec(1, ps)] * 2,
        out_shape=outs,
        compiler_params=_cparams(("arbitrary", "arbitrary")),
        name="s5_params",
    )(vec(lam_re), vec(lam_im), ldt, bt(b_re), bt(b_im), c_re, c_im)


def _s5_operators(lam_re, lam_im, log_dt, b_re, b_im, c_re, c_im):
    kk, wsr, wsi, wor, woi, etr, eti = _s5_params(lam_re, lam_im, log_dt, b_re, b_im, c_re, c_im)
    g = kk.shape[1]
    t, cg, ps = S5_CHUNK, SSM_GROUP, SSM_STATE
    kd = kk.reshape(2, g, t, cg, cg)
    s_idx = np.arange(t)[:, None]
    t_idx = np.arange(t)[None, :]

    def toeplitz(kdir, lag):
        m = jnp.where((lag >= 0)[None, :, :, None, None], kdir[:, np.mod(lag, t)], 0.0)
        return jnp.transpose(m, (0, 1, 4, 2, 3)).reshape(g, t * cg, t * cg)

    mf = toeplitz(kd[0], t_idx - s_idx)
    mb = toeplitz(kd[1], s_idx - t_idx)
    flat = lambda a: a.reshape(g, t * cg, ps)
    w1 = jnp.concatenate([mf, mb, flat(wsr[0]), flat(wsr[1]), flat(wsi[0]), flat(wsi[1])], axis=-1)
    tr = lambda a: jnp.swapaxes(flat(a), 1, 2)
    w2 = jnp.concatenate([tr(wor[0]), tr(wor[1]), tr(woi[0]), tr(woi[1])], axis=1)
    lam = jnp.concatenate([etr[0], etr[1], eti[0], eti[1]], axis=-1)
    lam = lam.reshape(g // S5_TILE_GROUPS, 1, S5_TILE_GROUPS * 4 * ps)
    return w1.astype(BF16), w2.astype(BF16), lam


def _chunk_lane_permutation():
    t, gl, cg = S5_CHUNK, S5_TILE_GROUPS, SSM_GROUP
    src = np.arange(t * gl * cg).reshape(t, gl, cg)
    dst = np.transpose(src, (1, 0, 2)).reshape(-1)
    perm = np.zeros((t * gl * cg, t * gl * cg), np.float32)
    perm[dst, np.arange(t * gl * cg)] = 1.0
    return perm


def _s5_body(bsz, nk, chunk_major, u_ref, perm_ref, permt_ref, w1_ref, w2_ref, lam_ref, s0_ref, y_ref, fin_ref,
             tok_scr, u_scr, ug_scr, yl_scr, x_scr, p_scr):
    t_, lt, gl = S5_CHUNK, 128, S5_TILE_GROUPS
    half = SSM_STATE
    sw = 4 * half
    l = nk * t_
    tw = t_ * SSM_GROUP

    def token_rows(t):
        if chunk_major:
            return [(k * bsz, bsz, k * t_ + t, l) for k in range(nk)]
        return [(b * nk, nk, b * l + t, t_) for b in range(bsz)]

    tok_scr[...] = u_ref[...].astype(F32)
    for t in range(t_):
        for r0, n, tok0, stride in token_rows(t):
            u_scr[r0:r0 + n, t * lt:(t + 1) * lt] = tok_scr[pl.ds(tok0, n, stride=stride), :].astype(BF16)
    ug_scr[...] = jnp.dot(u_scr[...], perm_ref[...], preferred_element_type=F32).astype(BF16)
    for gi in range(gl):
        z = jnp.dot(ug_scr[:, gi * tw:(gi + 1) * tw], w1_ref[gi], preferred_element_type=F32)
        yl_scr[:, gi * tw:(gi + 1) * tw] = z[:, :tw] + z[:, tw:2 * tw]
        x_scr[:, gi * sw:(gi + 1) * sw] = z[:, 2 * tw:]
    p_scr[...] = jnp.zeros(p_scr.shape, F32)

    srows = bsz if chunk_major else 8
    nset = 1 if chunk_major else bsz
    s0_of = (lambda q: slice(0, bsz)) if chunk_major else (lambda q: slice(q, q + 1))
    fwd_lane = lax.broadcasted_iota(jnp.int32, (srows, 2 * half), 1) < half
    row_id = lax.broadcasted_iota(jnp.int32, (srows, 2 * half), 0)
    lam = lam_ref[...]
    cols_a = lambda gi: pl.ds(gi * sw, 2 * half)
    cols_b = lambda gi: pl.ds(gi * sw + 2 * half, 2 * half)
    lam_of = lambda gi: (lam[:, gi * sw:gi * sw + 2 * half], lam[:, gi * sw + 2 * half:(gi + 1) * sw])

    def step(t, carry):
        kf = pl.ds(pl.multiple_of(t * bsz, bsz), bsz)
        kb = pl.ds(pl.multiple_of((nk - 1 - t) * bsz, bsz), bsz)
        new = []
        for gi in range(gl):
            ca, cb = cols_a(gi), cols_b(gi)
            sr, si = carry[gi]
            p_scr[kf, ca] = jnp.where(fwd_lane, sr, p_scr[kf, ca])
            p_scr[kb, ca] = jnp.where(fwd_lane, p_scr[kb, ca], sr)
            p_scr[kf, cb] = jnp.where(fwd_lane, si, p_scr[kf, cb])
            p_scr[kb, cb] = jnp.where(fwd_lane, p_scr[kb, cb], si)
            xr = jnp.where(fwd_lane, x_scr[kf, ca], x_scr[kb, ca])
            xi = jnp.where(fwd_lane, x_scr[kf, cb], x_scr[kb, cb])
            lr, li = lam_of(gi)
            new.append((lr * sr - li * si + xr, lr * si + li * sr + xi))
        return tuple(new)

    def step8(t8, carry):
        nb8 = nk // 8
        new = []
        for q in range(bsz):
            kf = pl.ds(pl.multiple_of(q * nk + t8 * 8, 8), 8)
            kb = pl.ds(pl.multiple_of(q * nk + (nb8 - 1 - t8) * 8, 8), 8)
            for gi in range(gl):
                ca, cb = cols_a(gi), cols_b(gi)
                sr, si = carry[q * gl + gi]
                xrf, xrb, xif, xib = x_scr[kf, ca], x_scr[kb, ca], x_scr[kf, cb], x_scr[kb, cb]
                prf, prb, pif, pib = p_scr[kf, ca], p_scr[kb, ca], p_scr[kf, cb], p_scr[kb, cb]
                lr, li = lam_of(gi)
                for j in range(8):
                    at_f = (row_id == j) & fwd_lane
                    at_b = (row_id == 7 - j) & jnp.logical_not(fwd_lane)
                    prf, prb = jnp.where(at_f, sr, prf), jnp.where(at_b, sr, prb)
                    pif, pib = jnp.where(at_f, si, pif), jnp.where(at_b, si, pib)
                    row = lambda a, i: jnp.broadcast_to(a[i:i + 1], a.shape)
                    xr = jnp.where(fwd_lane, row(xrf, j), row(xrb, 7 - j))
                    xi = jnp.where(fwd_lane, row(xif, j), row(xib, 7 - j))
                    sr, si = lr * sr - li * si + xr, lr * si + li * sr + xi
                p_scr[kf, ca], p_scr[kb, ca], p_scr[kf, cb], p_scr[kb, cb] = prf, prb, pif, pib
                new.append((sr, si))
        return tuple(new)

    bcast = lambda a: jnp.broadcast_to(a, (srows, 2 * half))
    init = tuple((bcast(s0_ref[s0_of(q), gi * sw:gi * sw + 2 * half]),
                  bcast(s0_ref[s0_of(q), gi * sw + 2 * half:(gi + 1) * sw]))
                 for q in range(nset) for gi in range(gl))
    if chunk_major:
        fin = lax.fori_loop(0, nk, step, init)
    else:
        fin = lax.fori_loop(0, nk // 8, step8, init)
    nfin = bsz if chunk_major else 1
    for q in range(nset):
        for gi in range(gl):
            sr, si = fin[q * gl + gi]
            fin_ref[s0_of(q), gi * sw:gi * sw + 2 * half] = sr[0:nfin]
            fin_ref[s0_of(q), gi * sw + 2 * half:(gi + 1) * sw] = si[0:nfin]

    for gi in range(gl):
        y = yl_scr[:, gi * tw:(gi + 1) * tw] + jnp.dot(p_scr[:, gi * sw:(gi + 1) * sw].astype(BF16), w2_ref[gi],
                                                       preferred_element_type=F32)
        u_scr[:, gi * tw:(gi + 1) * tw] = y.astype(BF16)
    yt = jnp.dot(u_scr[...], permt_ref[...], preferred_element_type=F32)
    for t in range(t_):
        for r0, n, tok0, stride in token_rows(t):
            tok_scr[pl.ds(tok0, n, stride=stride), :] = yt[r0:r0 + n, t * lt:(t + 1) * lt]
    y_ref[...] = tok_scr[...].astype(BF16)


def _s5(proj, col0, bsz, l, ops, s0):
    w1, w2, lam = ops
    g = w1.shape[0]
    t, ps, gl, lt = S5_CHUNK, SSM_STATE, S5_TILE_GROUPS, 128
    nb = g // gl
    nk = l // t
    r = bsz * l
    nrows = bsz * nk
    sw = 4 * ps
    cw = t * lt
    tw = t * SSM_GROUP
    chunk_major = bsz % 8 == 0
    assert chunk_major or nk % 8 == 0
    if s0 is None:
        s0p = jnp.zeros((bsz, g * sw), F32)
    else:
        s0_re, s0_im = s0
        s0p = jnp.concatenate([s0_re[:, 0], s0_re[:, 1], s0_im[:, 0], s0_im[:, 1]], axis=-1).reshape(bsz, g * sw)
    perm = _chunk_lane_permutation()
    const = lambda shape: _resident(shape, lambda i: (0,) * len(shape))
    y, fin = pl.pallas_call(
        functools.partial(_s5_body, bsz, nk, chunk_major),
        grid=(nb,),
        in_specs=[pl.BlockSpec((r, lt), lambda i: (0, col0 // lt + i), pipeline_mode=pl.Buffered(1)),
                  const((cw, cw)), const((cw, cw)),
                  pl.BlockSpec((gl, tw, 3 * tw), lambda i: (i, 0, 0)), pl.BlockSpec((gl, tw, tw), lambda i: (i, 0, 0)),
                  pl.BlockSpec((None, 1, gl * sw), lambda i: (i, 0, 0)),
                  pl.BlockSpec((bsz, gl * sw), lambda i: (0, i))],
        out_specs=[pl.BlockSpec((r, lt), lambda i: (0, i)), pl.BlockSpec((bsz, gl * sw), lambda i: (0, i))],
        out_shape=[jax.ShapeDtypeStruct((r, g * SSM_GROUP), BF16), jax.ShapeDtypeStruct((bsz, g * sw), F32)],
        scratch_shapes=[pltpu.VMEM((r, lt), F32), pltpu.VMEM((nrows, cw), BF16), pltpu.VMEM((nrows, cw), BF16),
                        pltpu.VMEM((nrows, cw), F32), pltpu.VMEM((nrows, gl * sw), F32),
                        pltpu.VMEM((nrows, gl * sw), F32)],
        compiler_params=_cparams(("parallel",)),
        name=f"s5_scan_{nk}",
    )(proj, jnp.asarray(perm, BF16), jnp.asarray(perm.T, BF16), w1, w2, lam, s0p)
    fin = jnp.transpose(fin.reshape(bsz, g, 2, 2, ps), (2, 0, 3, 1, 4))
    return y, fin[0], fin[1]


def _merge_body(yh_ref, ys_ref, u_ref, gh_ref, gs_ref, x_ref, d_ref, gw_ref, gbias_ref, wbh_ref, wbs_ref, wo_ref,
                g1_ref, n2_ref, sc2_ref, sh2_ref, x1_ref, h2_ref):
    y = ys_ref[...].astype(F32) + u_ref[...].astype(F32) * d_ref[...]
    g = jax.nn.gelu(y)
    gl = jnp.dot(g.astype(BF16), gw_ref[...], preferred_element_type=F32) + gbias_ref[...]
    ys = (g * jax.nn.sigmoid(gl)).astype(BF16)
    bh = jnp.dot(yh_ref[...], wbh_ref[...], preferred_element_type=F32)
    bs = jnp.dot(ys, wbs_ref[...], preferred_element_type=F32)
    merged = gh_ref[...].astype(F32) * bh + gs_ref[...].astype(F32) * bs
    x1 = x_ref[...] + g1_ref[...] * jnp.dot(merged.astype(BF16), wo_ref[...], preferred_element_type=F32)
    x1_ref[...] = x1
    h2_ref[...] = _mod_rmsnorm(x1, n2_ref[...], sc2_ref[...], sh2_ref[...]).astype(BF16)


def _merge(yh, ys, proj, x, mod4, cond_of_tile, ssm_d, glu_w, glu_b, wbh, wbs, wo, n2g, c_h, c_s, tm=256):
    r, d = x.shape
    row_blk = lambda width, k: pl.BlockSpec((tm, width), lambda i: (i, k))
    mod_spec = lambda chunk: pl.BlockSpec((None, None, 1, d), lambda i: (cond_of_tile(i, tm), chunk, 0, 0))
    const = lambda shape: _resident(shape, lambda i: (0,) * len(shape))
    u_col = (3 * c_h) // c_s
    gate0 = (3 * c_h + c_s) // d
    return pl.pallas_call(
        _merge_body,
        grid=(r // tm,),
        in_specs=[row_blk(c_h, 0), row_blk(c_s, 0), row_blk(c_s, u_col), row_blk(d, gate0), row_blk(d, gate0 + 1),
                  row_blk(d, 0), const((1, c_s)), const((c_s, c_s)), const((1, c_s)), const((c_h, d)),
                  const((c_s, d)), const((d, d)), mod_spec(2), const((1, d)), mod_spec(4), mod_spec(3)],
        out_specs=[row_blk(d, 0), row_blk(d, 0)],
        out_shape=[jax.ShapeDtypeStruct((r, d), F32), jax.ShapeDtypeStruct((r, d), BF16)],
        compiler_params=_cparams(("parallel",)),
        name="merge",
    )(yh, ys, proj, proj, proj, x, ssm_d.reshape(1, c_s), glu_w, glu_b.reshape(1, c_s), wbh, wbs, wo,
      mod4, n2g.reshape(1, d), mod4, mod4)


_FFN_RC = 512


def _ffn_up_body(rows, cols, h_ref, wa_ref, wb_ref, cwa_ref, cwb_ref, cba_ref, cbb_ref, o_ref):
    tm, tn = o_ref.shape
    rc = _FFN_RC
    n = tm // rc
    col = lax.rem(lax.broadcasted_iota(jnp.int32, (rc, tn), 0), cols)
    not_first = col != 0
    not_last = col != cols - 1
    zero = jnp.zeros((cols, tn), F32)

    def taps(hc, w_ref, cw_ref, cb_ref):
        a = jnp.dot(hc, w_ref[...], preferred_element_type=F32)
        am = jnp.where(not_first, pltpu.roll(a, 1, 0), 0.0)
        ap = jnp.where(not_last, pltpu.roll(a, rc - 1, 0), 0.0)
        cw = cw_ref[...]
        tap = lambda i: am * cw[3 * i:3 * i + 1] + a * cw[3 * i + 1:3 * i + 2] + ap * cw[3 * i + 2:3 * i + 3]
        mid = tap(1) + cb_ref[...]
        return (tap(0), mid, tap(2)) if rows > 1 else (None, mid, None)

    def finish(prev, cur, nxt):
        t0, mid, t2 = cur
        if rows == 1:
            return mid
        above = jnp.concatenate([prev[0][rc - cols:] if prev is not None else zero, t0[:rc - cols]], axis=0)
        below = jnp.concatenate([t2[cols:], nxt[2][:cols] if nxt is not None else zero], axis=0)
        return mid + above + below

    chunks = []
    for i in range(n + 1):
        if i < n:
            hc = h_ref[i * rc:(i + 1) * rc, :]
            chunks.append((taps(hc, wa_ref, cwa_ref, cba_ref), taps(hc, wb_ref, cwb_ref, cbb_ref)))
        if i >= 1:
            j = i - 1
            pick = lambda k, side: chunks[k][side] if 0 <= k < n else None
            a = finish(pick(j - 1, 0), pick(j, 0), pick(j + 1, 0))
            b = finish(pick(j - 1, 1), pick(j, 1), pick(j + 1, 1))
            o_ref[j * rc:(j + 1) * rc, :] = (jax.nn.gelu(a) * b).astype(BF16)


def _ffn_up(h2, w_up, conv_w, conv_b, rows, cols, tn=256):
    r, d = h2.shape
    f = w_up.shape[1] // 2
    tm = rows * cols if rows > 1 else 4096
    assert tm % _FFN_RC == 0 and _FFN_RC % cols == 0 and cols % 8 == 0
    nt = f // tn
    cw = conv_w.reshape(9, 2 * f)
    cb = conv_b.reshape(1, 2 * f)
    return pl.pallas_call(
        functools.partial(_ffn_up_body, rows, cols),
        grid=(r // tm, nt),
        in_specs=[pl.BlockSpec((tm, d), lambda i, j: (i, 0), pipeline_mode=pl.Buffered(1)),
                  pl.BlockSpec((d, tn), lambda i, j: (0, j)), pl.BlockSpec((d, tn), lambda i, j: (0, nt + j)),
                  pl.BlockSpec((9, tn), lambda i, j: (0, j)), pl.BlockSpec((9, tn), lambda i, j: (0, nt + j)),
                  pl.BlockSpec((1, tn), lambda i, j: (0, j)), pl.BlockSpec((1, tn), lambda i, j: (0, nt + j))],
        out_specs=pl.BlockSpec((tm, tn), lambda i, j: (i, j)),
        out_shape=jax.ShapeDtypeStruct((r, f), BF16),
        compiler_params=_cparams(("parallel", "arbitrary")),
        name=f"ffn_up_{rows}x{cols}",
    )(h2, w_up, w_up, cw, cw, cb, cb)


def _ffn_down_body(a_ref, w_ref, x_ref, g2_ref, fg_ref, o_ref):
    x2 = x_ref[...] + g2_ref[...] * jnp.dot(a_ref[...], w_ref[...], preferred_element_type=F32)
    ms = jnp.mean(x2 * x2, axis=-1, keepdims=True)
    o_ref[...] = x2 * lax.rsqrt(ms + EPS) * fg_ref[...]


def _ffn_down(act, w, x1, mod4, cond_of_tile, final_g, tm=256):
    r, f = act.shape
    d = w.shape[1]
    return pl.pallas_call(
        _ffn_down_body,
        grid=(r // tm,),
        in_specs=[pl.BlockSpec((tm, f), lambda i: (i, 0)),
                  _resident((f, d), lambda i: (0, 0)),
                  pl.BlockSpec((tm, d), lambda i: (i, 0)),
                  pl.BlockSpec((None, None, 1, d), lambda i: (cond_of_tile(i, tm), 5, 0, 0)),
                  _resident((1, d), lambda i: (0, 0))],
        out_specs=pl.BlockSpec((tm, d), lambda i: (i, 0)),
        out_shape=jax.ShapeDtypeStruct((r, d), F32),
        compiler_params=_cparams(("parallel",)),
        name="ffn_down",
    )(act, w, x1, mod4, final_g.reshape(1, d))


def _segment(x, cond_base, per_batch_cond, s0, rows, cols, hy_block, hy_nseq, mod4, p, final_g):
    bsz, l, d = x.shape
    xf = x.reshape(bsz * l, d)
    c_h = p["wbh"].shape[0]
    c_s = p["wbs"].shape[0]

    def cond_of_tile(i, tm):
        return cond_base + (i * tm) // l if per_batch_cond else cond_base

    proj = _in_proj(xf, mod4, cond_of_tile, p["norm1_g"], p["w_in"], p["b_in"], 3 * c_h + c_s)
    yh = _hyena(proj, l, hy_block, hy_nseq, p["hy_short_w"], p["hy_short_b"], p["hy_bias"], p["ghat"][l], c_h)
    ys, fin_re, fin_im = _s5(proj, 3 * c_h, bsz, l, p["s5_ops"], s0)
    x1, h2 = _merge(yh, ys, proj, xf, mod4, cond_of_tile, p["ssm_d"], p["glu_w"], p["glu_b"],
                    p["wbh"], p["wbs"], p["wo"], p["norm2_g"], c_h, c_s)
    act = _ffn_up(h2, p["ffn_up"], p["ffn_conv_w"], p["ffn_conv_b"], rows, cols)
    y = _ffn_down(act, p["ffn_down"], x1, mod4, cond_of_tile, final_g)
    return y.reshape(bsz, l, d), fin_re, fin_im


def kernel(x_prompt, x_sample, state_ssm_re, state_ssm_im, c, c_ctx, ada_w, ada_b, norm1_g, norm2_g, final_g, w_in, b_in, hy_short_w, hy_short_b, hy_f_w1, hy_f_b1, hy_f_w2, hy_f_b2, hy_f_w3, hy_f_b3, hy_f_w4, hy_f_freq, hy_bias, w_branch_h, ssm_lambda_re, ssm_lambda_im, ssm_log_dt, ssm_b_re, ssm_b_im, ssm_c_re, ssm_c_im, ssm_d, ssm_glu_w, ssm_glu_b, w_branch_s, w_out, ffn_up, ffn_conv_w, ffn_conv_b, ffn_down):
    depth = ada_w.shape[0]
    assert depth == 1, "the final norm is fused into the (single) layer's last kernel"
    d = x_prompt.shape[-1]
    ctx_len = x_prompt.shape[1]
    lat_len = x_sample.shape[1]
    dec_b = x_sample.shape[0]
    c_h = w_branch_h.shape[1]
    assert 1 + dec_b <= 8
    cond = jnp.zeros((8, d), F32).at[0].set(c_ctx).at[1:1 + dec_b].set(c)
    lay = 0
    mod4 = _ada(cond, ada_w[lay], ada_b[lay]).reshape(8, N_MOD, 1, d)
    hy_blocks = {ctx_len: 256, lat_len: 512}
    ghat = {l: _hyena_filters(l, hy_blocks[l], hy_f_w1[lay], hy_f_b1[lay], hy_f_w2[lay], hy_f_b2[lay],
                              hy_f_w3[lay], hy_f_b3[lay], hy_f_w4[lay], hy_f_freq[lay], c_h)
            for l in (ctx_len, lat_len)}
    s5_ops = _s5_operators(ssm_lambda_re[lay], ssm_lambda_im[lay], ssm_log_dt[lay],
                           ssm_b_re[lay], ssm_b_im[lay], ssm_c_re[lay], ssm_c_im[lay])
    p = dict(norm1_g=norm1_g[lay], norm2_g=norm2_g[lay], w_in=w_in[lay].astype(BF16), b_in=b_in[lay],
             hy_short_w=hy_short_w[lay], hy_short_b=hy_short_b[lay], hy_bias=hy_bias[lay], ghat=ghat,
             s5_ops=s5_ops, ssm_d=ssm_d[lay], glu_w=ssm_glu_w[lay].astype(BF16),
             glu_b=ssm_glu_b[lay], wbh=w_branch_h[lay].astype(BF16), wbs=w_branch_s[lay].astype(BF16),
             wo=w_out[lay].astype(BF16), ffn_up=ffn_up[lay].astype(BF16), ffn_conv_w=ffn_conv_w[lay],
             ffn_conv_b=ffn_conv_b[lay], ffn_down=ffn_down[lay].astype(BF16))
    y_prompt, st_re, st_im = _segment(x_prompt, 0, False, None, 1, ctx_len, hy_blocks[ctx_len], 8, mod4, p, final_g)
    s0 = (state_ssm_re[:, lay].astype(F32), state_ssm_im[:, lay].astype(F32))
    y_sample, _, _ = _segment(x_sample, 1, True, s0, lat_len // GRID_W, GRID_W, hy_blocks[lat_len], 1, mod4, p, final_g)
    return y_prompt, y_sample, st_re[:, None], st_im[:, None]
```

```python
import functools
import math

import jax
import jax.numpy as jnp
import numpy as np
from jax import lax
from jax.experimental import pallas as pl
from jax.experimental.pallas import tpu as pltpu

F32 = jnp.float32
BF16 = jnp.bfloat16
HIGHEST = lax.Precision.HIGHEST

GRID_W = 64
EPS = 1e-6
SSM_GROUP = 16
SSM_STATE = 64
FILTER_EMB = 33
FILTER_HIDDEN = 64
HYENA_ORDER = 2
DECAY_FAST = 0.3
DECAY_SLOW = 1.5
DECAY_TARGET = 1e-2
LAMBDA_RE_MAX = -1e-4

S5_CHUNK = 16
S5_TILE_GROUPS = 8
V7X_VMEM_LIMIT = 56 * 2**20
N_MOD = 6


def _cparams(sem):
    return pltpu.CompilerParams(dimension_semantics=sem, vmem_limit_bytes=V7X_VMEM_LIMIT)


def _resident(shape, index_map):
    return pl.BlockSpec(shape, index_map, pipeline_mode=pl.Buffered(1))


def _mod_rmsnorm(x, g, sc, sh):
    ms = jnp.mean(x * x, axis=-1, keepdims=True)
    return x * lax.rsqrt(ms + EPS) * g * (1.0 + sc) + sh


def _ada_body(c_ref, w_ref, b_ref, o_ref):
    c = c_ref[...]
    s = c * jax.nn.sigmoid(c)
    o_ref[...] = jnp.dot(s, w_ref[...], precision=HIGHEST, preferred_element_type=F32) + b_ref[...]


def _ada(cond, w, b):
    d, n = w.shape
    tn = 1024
    return pl.pallas_call(
        _ada_body,
        grid=(n // tn,),
        in_specs=[pl.BlockSpec((8, d), lambda j: (0, 0)),
                  pl.BlockSpec((d, tn), lambda j: (0, j)),
                  pl.BlockSpec((1, tn), lambda j: (0, j))],
        out_specs=pl.BlockSpec((8, tn), lambda j: (0, j)),
        out_shape=jax.ShapeDtypeStruct((8, n), F32),
        compiler_params=_cparams(("parallel",)),
        name="ada",
    )(cond, w, b.reshape(1, n))


def _inproj_body(x_ref, g_ref, sc_ref, sh_ref, w_ref, b_ref, o_ref, h_scr):
    @pl.when(pl.program_id(1) == 0)
    def _():
        h_scr[...] = _mod_rmsnorm(x_ref[...], g_ref[...], sc_ref[...], sh_ref[...]).astype(BF16)

    o_ref[...] = (jnp.dot(h_scr[...], w_ref[...], preferred_element_type=F32) + b_ref[...]).astype(BF16)


def _in_proj(x, mod4, cond_of_tile, g, w, b, tm=1024, tn=1024):
    r, d = x.shape
    n = w.shape[1]
    mod_spec = lambda chunk: pl.BlockSpec((None, None, 1, d), lambda i, j: (cond_of_tile(i, tm), chunk, 0, 0))
    return pl.pallas_call(
        _inproj_body,
        grid=(r // tm, n // tn),
        in_specs=[pl.BlockSpec((tm, d), lambda i, j: (i, 0)),
                  pl.BlockSpec((1, d), lambda i, j: (0, 0)),
                  mod_spec(1), mod_spec(0),
                  pl.BlockSpec((d, tn), lambda i, j: (0, j)),
                  pl.BlockSpec((1, tn), lambda i, j: (0, j))],
        out_specs=pl.BlockSpec((tm, tn), lambda i, j: (i, j)),
        out_shape=jax.ShapeDtypeStruct((r, n), BF16),
        scratch_shapes=[pltpu.VMEM((tm, d), BF16)],
        compiler_params=_cparams(("parallel", "arbitrary")),
        name="in_proj",
    )(x, g.reshape(1, d), mod4, mod4, w, b.reshape(1, n))


def _shifted_dft_tables(p):
    theta = np.pi * (2.0 * np.arange(p) + 1.0) / (2.0 * p)
    n = np.arange(p)
    a = theta[:, None] * n[None, :]
    fwd = np.concatenate([np.cos(a), -np.sin(a)], axis=0)
    m = np.arange(2 * p) - p
    am = theta[:, None] * m[None, :]
    filt = np.concatenate([np.cos(am), -np.sin(am)], axis=0)
    filt[:, 0] = 0.0
    inv = np.concatenate([np.cos(a).T, -np.sin(a).T], axis=1) / p
    return fwd, filt, inv


_FILTER_RC = 1024


def _filter_positions(l):
    pos = np.abs(np.arange(2 * l) - l).astype(np.float64)
    t = pos / (l - 1)
    bands = (FILTER_EMB - 1) // 2
    fr = np.linspace(1e-4, bands - 1, bands)
    ang = (2.0 * math.pi / l) * pos[:, None] * fr[None, :]
    z = np.concatenate([t[:, None], np.cos(ang), -np.sin(ang)], axis=-1)
    zp = np.zeros((2 * l, 128), np.float32)
    zp[:, :FILTER_EMB] = z
    return zp


def _filter_hidden_body(z_ref, w1_ref, b1_ref, w2_ref, b2_ref, w3_ref, b3_ref, fq_ref, o_ref):
    fq = fq_ref[...]
    dot = functools.partial(jnp.dot, precision=HIGHEST, preferred_element_type=F32)
    h = jnp.sin(fq * (dot(z_ref[...], w1_ref[...]) + b1_ref[...]))
    h = jnp.sin(fq * (dot(h, w2_ref[...]) + b2_ref[...]))
    o_ref[...] = jnp.sin(fq * (dot(h, w3_ref[...]) + b3_ref[...]))


def _filter_hidden(zp, w1p, b1, w2, b2, w3, b3, freq):
    rows = zp.shape[0]
    rc = min(rows, _FILTER_RC)
    hd = w2.shape[0]
    full = lambda shape: pl.BlockSpec(shape, lambda i: (0,) * len(shape))
    row = lambda a: a.reshape(1, -1)
    return pl.pallas_call(
        _filter_hidden_body,
        grid=(rows // rc,),
        in_specs=[pl.BlockSpec((rc, 128), lambda i: (i, 0)), full((128, hd)), full((1, hd)), full((hd, hd)),
                  full((1, hd)), full((hd, hd)), full((1, hd)), full((1, hd))],
        out_specs=pl.BlockSpec((rc, hd), lambda i: (i, 0)),
        out_shape=jax.ShapeDtypeStruct((rows, hd), F32),
        compiler_params=_cparams(("parallel",)),
        name=f"hyena_filter_mlp_{rows}",
    )(zp, w1p, row(b1), w2, row(b2), w3, row(b3), row(freq))


def _split_bf16(x):
    hi = x.astype(BF16)
    return hi, (x - hi.astype(F32)).astype(BF16)


def _dot_3pass(x, w_hi, w_lo):
    x_hi, x_lo = _split_bf16(x)
    d = functools.partial(jnp.dot, preferred_element_type=F32)
    return d(x_hi, w_hi) + (d(x_lo, w_hi) + d(x_hi, w_lo))


def _filter_body(l, p, h_ref, w4f_ref, w4b_ref, dl_ref, mg_ref, o_ref, kk_scr, mg_scr):
    mg_scr[...] = mg_ref[...].astype(BF16)
    rc = min(l, _FILTER_RC)
    ct = kk_scr.shape[1]
    nhalf = l // rc

    def taps(w_ref, first):
        w_hi, w_lo = _split_bf16(w_ref[...])

        def body(i, asum):
            r0 = pl.multiple_of(i * rc, rc)
            q = r0 + lax.broadcasted_iota(jnp.int32, (rc, ct), 0)
            t = jnp.abs(q - l).astype(F32) * (1.0 / (l - 1))
            k = _dot_3pass(h_ref[pl.ds(r0, rc), :], w_hi, w_lo) * jnp.exp(-t * dl_ref[...])
            k = jnp.where(q == 0, 0.0, k)
            kk_scr[pl.ds(r0, rc), :] = k.astype(BF16)
            return asum + jnp.sum(jnp.abs(k), axis=0, keepdims=True)

        return lambda asum: lax.fori_loop(first, first + nhalf, body, asum)

    asum = taps(w4f_ref, nhalf)(taps(w4b_ref, 0)(jnp.zeros((1, ct), F32)))
    scale = 1.0 / asum
    nwin = 2 * (l // p) - 1

    def win(w, _):
        seg = kk_scr[pl.ds(pl.multiple_of(w * p, p), 2 * p), :]
        o_ref[w] = jnp.dot(mg_scr[...], seg, preferred_element_type=F32) * scale
        return 0

    lax.fori_loop(0, nwin, win, 0)


def _hyena_filters(l, p, w1, b1, w2, b2, w3, b3, w4, freq, c, ct=256):
    nwin = 2 * (l // p) - 1
    nct = c // ct
    zp = jnp.asarray(_filter_positions(l))
    w1p = jnp.zeros((128, FILTER_HIDDEN), F32).at[:FILTER_EMB].set(w1)
    hid = _filter_hidden(zp, w1p, b1, w2, b2, w3, b3, freq)
    deltas = np.abs(np.linspace(math.log(DECAY_TARGET) / DECAY_SLOW, math.log(DECAY_TARGET) / DECAY_FAST, c))
    mg = jnp.asarray(_shifted_dft_tables(p)[1], F32)
    full = lambda shape: pl.BlockSpec(shape, lambda o, j: (0,) * len(shape))
    return pl.pallas_call(
        functools.partial(_filter_body, l, p),
        grid=(HYENA_ORDER, nct),
        in_specs=[full((2 * l, FILTER_HIDDEN)),
                  pl.BlockSpec((FILTER_HIDDEN, ct), lambda o, j: (0, o * 2 * nct + j)),
                  pl.BlockSpec((FILTER_HIDDEN, ct), lambda o, j: (0, o * 2 * nct + nct + j)),
                  pl.BlockSpec((1, ct), lambda o, j: (0, j)),
                  full((2 * p, 2 * p))],
        out_specs=pl.BlockSpec((None, nwin, 2 * p, ct), lambda o, j: (o, 0, 0, j)),
        out_shape=jax.ShapeDtypeStruct((HYENA_ORDER, nwin, 2 * p, c), F32),
        scratch_shapes=[pltpu.VMEM((2 * l, ct), BF16), pltpu.VMEM((2 * p, 2 * p), BF16)],
        compiler_params=_cparams(("parallel", "parallel")),
        name=f"hyena_filter_{l}",
    )(hid, w4, w4, jnp.asarray(deltas, F32).reshape(1, c), mg)


_HY_PASS_VREGS = 8


def _hyena_body(nseq, l, p, x1_ref, x2_ref, v_ref, w1_ref, w2_ref, wv_ref, b1_ref, b2_ref, bv_ref, hb_ref,
                g_ref, fz_ref, fi_ref, o_ref, z_scr, c_scr, zh_scr, yh_scr, z1_scr, fz_scr, fi_scr):
    fz_scr[...] = fz_ref[...].astype(BF16)
    fi_scr[...] = fi_ref[...].astype(BF16)
    o = pl.program_id(1)
    s = pl.program_id(2)
    rows = nseq * l
    nb = l // p
    ct = o_ref.shape[1]
    hrc = _HY_PASS_VREGS * 8 * 128 // ct
    pos = lax.rem(lax.broadcasted_iota(jnp.int32, (rows, ct), 0), l)

    def short_conv(u_ref, w_ref, b_ref):
        u = u_ref[...].astype(F32)
        um = jnp.where(pos == 0, 0.0, pltpu.roll(u, 1, 0))
        up = jnp.where(pos == l - 1, 0.0, pltpu.roll(u, rows - 1, 0))
        w = w_ref[...]
        return um * w[0:1] + u * w[1:2] + up * w[2:3] + b_ref[...]

    def long_conv():
        for q in range(nseq):
            base = q * l

            def fwd(j, _):
                zb = z_scr[pl.ds(pl.multiple_of(base + j * p, p), p), :].astype(BF16)
                zh_scr[j] = jnp.dot(fz_scr[...], zb, preferred_element_type=F32)
                return 0

            lax.fori_loop(0, nb, fwd, 0)

            def out_block(i, _):
                for rc in range(p // hrc):
                    re = pl.ds(rc * hrc, hrc)
                    im = pl.ds(p + rc * hrc, hrc)

                    def acc(j, carry):
                        ar, ai = carry
                        w = i - j + (nb - 1)
                        gr, gi = g_ref[w, re, :], g_ref[w, im, :]
                        zr, zi = zh_scr[j, re, :], zh_scr[j, im, :]
                        return ar + gr * zr - gi * zi, ai + gr * zi + gi * zr

                    zero = jnp.zeros((hrc, ct), F32)
                    ar, ai = lax.fori_loop(0, nb, acc, (zero, zero), unroll=True)
                    yh_scr[re, :] = ar
                    yh_scr[im, :] = ai
                c_scr[pl.ds(pl.multiple_of(base + i * p, p), p), :] = jnp.dot(
                    fi_scr[...], yh_scr[...].astype(BF16), preferred_element_type=F32)
                return 0

            lax.fori_loop(0, nb, out_block, 0)

    hb = hb_ref[...]

    @pl.when(o == 0)
    def _():
        v = short_conv(v_ref, wv_ref, bv_ref)
        z_scr[...] = v
        long_conv()
        x1 = short_conv(x1_ref, w1_ref, b1_ref)
        z1_scr[s] = x1 * (c_scr[...] + hb[0:1] * v)

    @pl.when(o == 1)
    def _():
        z1 = z1_scr[s]
        z_scr[...] = z1
        long_conv()
        x2 = short_conv(x2_ref, w2_ref, b2_ref)
        o_ref[...] = (x2 * (c_scr[...] + hb[1:2] * z1)).astype(BF16)


def _hyena(proj, l, p, nseq, short_w, short_b, hy_bias, ghat, c, ct=128):
    r = proj.shape[0]
    rows = nseq * l
    nsb = r // rows
    nct = c // ct
    nwin = ghat.shape[1]
    fwd, _, inv = _shifted_dft_tables(p)
    fz = jnp.asarray(fwd, F32)
    fi = jnp.asarray(inv, F32)
    sb = short_b.reshape(1, 3 * c)
    col = lambda k: (lambda j, o, s: (s, k * nct + j))
    wcol = lambda k: (lambda j, o, s: (0, k * nct + j))
    return pl.pallas_call(
        functools.partial(_hyena_body, nseq, l, p),
        grid=(nct, HYENA_ORDER, nsb),
        in_specs=[pl.BlockSpec((rows, ct), col(0)), pl.BlockSpec((rows, ct), col(1)), pl.BlockSpec((rows, ct), col(2)),
                  pl.BlockSpec((3, ct), wcol(0)), pl.BlockSpec((3, ct), wcol(1)), pl.BlockSpec((3, ct), wcol(2)),
                  pl.BlockSpec((1, ct), wcol(0)), pl.BlockSpec((1, ct), wcol(1)), pl.BlockSpec((1, ct), wcol(2)),
                  pl.BlockSpec((HYENA_ORDER, ct), lambda j, o, s: (0, j)),
                  pl.BlockSpec((None, nwin, 2 * p, ct), lambda j, o, s: (o, 0, 0, j)),
                  pl.BlockSpec((2 * p, p), lambda j, o, s: (0, 0)),
                  pl.BlockSpec((p, 2 * p), lambda j, o, s: (0, 0))],
        out_specs=pl.BlockSpec((rows, ct), lambda j, o, s: (s * o, j)),
        out_shape=jax.ShapeDtypeStruct((r, c), BF16),
        scratch_shapes=[pltpu.VMEM((rows, ct), F32), pltpu.VMEM((rows, ct), F32),
                        pltpu.VMEM((l // p, 2 * p, ct), F32), pltpu.VMEM((2 * p, ct), F32),
                        pltpu.VMEM((nsb, rows, ct), F32), pltpu.VMEM((2 * p, p), BF16), pltpu.VMEM((p, 2 * p), BF16)],
        compiler_params=_cparams(("parallel", "arbitrary", "arbitrary")),
        name=f"hyena_{l}",
    )(proj, proj, proj, short_w, short_w, short_w, sb, sb, sb, hy_bias, ghat, fz, fi)


def _s5_param_body(lre_ref, lim_ref, ldt_ref, btr_ref, bti_ref, cr_ref, ci_ref,
                   k_ref, wsr_ref, wsi_ref, wor_ref, woi_ref, etr_ref, eti_ref):
    t = S5_CHUNK
    backward = pl.program_id(0) == 1
    lr = jnp.minimum(lre_ref[...], LAMBDA_RE_MAX)
    li = lim_ref[...]
    dt = jnp.exp(ldt_ref[...])
    ar, ai = lr * dt, li * dt

    def power(e):
        mag = jnp.exp(e * ar)
        return mag * jnp.cos(e * ai), mag * jnp.sin(e * ai)

    one = jnp.ones((1, 1), F32)
    lbr, lbi = power(one)
    nr, ni = lbr - 1.0, lbi
    den = lr * lr + li * li
    qr, qi = (nr * lr + ni * li) / den, (ni * lr - nr * li) / den
    btr, bti = btr_ref[...], bti_ref[...]
    bbr, bbi = qr * btr - qi * bti, qr * bti + qi * btr
    cr, ci = cr_ref[...], ci_ref[...]

    step = lax.broadcasted_iota(jnp.int32, (t, 1, 1), 0).astype(F32)
    er, ei = power(step)
    cer, cei = cr[None] * er - ci[None] * ei, cr[None] * ei + ci[None] * er
    nt = (((1,), (1,)), ((), ()))
    dg = functools.partial(lax.dot_general, dimension_numbers=nt, precision=HIGHEST, preferred_element_type=F32)
    k_ref[...] = (dg(cer.reshape(t * SSM_GROUP, SSM_STATE), bbr)
                  - dg(cei.reshape(t * SSM_GROUP, SSM_STATE), bbi))
    er, ei = power(jnp.where(backward, step, (t - 1.0) - step))
    wsr_ref[...] = er * bbr[None] - ei * bbi[None]
    wsi_ref[...] = er * bbi[None] + ei * bbr[None]
    er, ei = power(jnp.where(backward, t - step, step + 1.0))
    wor_ref[...] = cr[None] * er - ci[None] * ei
    woi_ref[...] = -(cr[None] * ei + ci[None] * er)
    etr, eti = power(one * t)
    etr_ref[...] = etr
    eti_ref[...] = eti


def _s5_param_tile_body(*refs):
    for gi in range(refs[0].shape[0]):
        _s5_param_body(*(r.at[gi] for r in refs))


def _s5_params(lam_re, lam_im, log_dt, b_re, b_im, c_re, c_im):
    g = lam_re.shape[1]
    gb = S5_TILE_GROUPS
    t, cg, ps = S5_CHUNK, SSM_GROUP, SSM_STATE
    vec = lambda a: a.reshape(2, g, 1, ps)
    ldt = jnp.broadcast_to(log_dt[:, :, None, None], (2, g, 1, ps))
    bt = lambda a: jnp.swapaxes(a, -1, -2)
    dg_spec = lambda *shape: pl.BlockSpec((None, gb) + shape, lambda d, i: (d, i) + (0,) * len(shape))
    outs = [jax.ShapeDtypeStruct((2, g, t * cg, cg), F32)] + \
           [jax.ShapeDtypeStruct((2, g, t, cg, ps), F32)] * 4 + [jax.ShapeDtypeStruct((2, g, 1, ps), F32)] * 2
    return pl.pallas_call(
        _s5_param_tile_body,
        grid=(2, g // gb),
        in_specs=[dg_spec(1, ps), dg_spec(1, ps), dg_spec(1, ps), dg_spec(cg, ps), dg_spec(cg, ps),
                  dg_spec(cg, ps), dg_spec(cg, ps)],
        out_specs=[dg_spec(t * cg, cg)] + [dg_spec(t, cg, ps)] * 4 + [dg_spec(1, ps)] * 2,
        out_shape=outs,
        compiler_params=_cparams(("arbitrary", "arbitrary")),
        name="s5_params",
    )(vec(lam_re), vec(lam_im), ldt, bt(b_re), bt(b_im), c_re, c_im)


def _s5_operators(lam_re, lam_im, log_dt, b_re, b_im, c_re, c_im):
    kk, wsr, wsi, wor, woi, etr, eti = _s5_params(lam_re, lam_im, log_dt, b_re, b_im, c_re, c_im)
    g = kk.shape[1]
    t, cg, ps = S5_CHUNK, SSM_GROUP, SSM_STATE
    kd = kk.reshape(2, g, t, cg, cg)
    s_idx = np.arange(t)[:, None]
    t_idx = np.arange(t)[None, :]

    def toeplitz(kdir, lag):
        m = jnp.where((lag >= 0)[None, :, :, None, None], kdir[:, np.mod(lag, t)], 0.0)
        return jnp.transpose(m, (0, 1, 4, 2, 3)).reshape(g, t * cg, t * cg)

    mf = toeplitz(kd[0], t_idx - s_idx)
    mb = toeplitz(kd[1], s_idx - t_idx)
    flat = lambda a: a.reshape(g, t * cg, ps)
    w1 = jnp.concatenate([mf, mb, flat(wsr[0]), flat(wsr[1]), flat(wsi[0]), flat(wsi[1])], axis=-1)
    tr = lambda a: jnp.swapaxes(flat(a), 1, 2)
    w2 = jnp.concatenate([tr(wor[0]), tr(wor[1]), tr(woi[0]), tr(woi[1])], axis=1)
    lam = jnp.concatenate([etr[0], etr[1], eti[0], eti[1]], axis=-1)
    lam = lam.reshape(g // S5_TILE_GROUPS, 1, S5_TILE_GROUPS * 4 * ps)
    return w1.astype(BF16), w2.astype(BF16), lam


def _chunk_lane_permutation():
    t, gl, cg = S5_CHUNK, S5_TILE_GROUPS, SSM_GROUP
    src = np.arange(t * gl * cg).reshape(t, gl, cg)
    dst = np.transpose(src, (1, 0, 2)).reshape(-1)
    perm = np.zeros((t * gl * cg, t * gl * cg), np.float32)
    perm[dst, np.arange(t * gl * cg)] = 1.0
    return perm


def _s5_body(bsz, nk, chunk_major, u_ref, perm_ref, permt_ref, w1_ref, w2_ref, lam_ref, s0_ref, y_ref, fin_ref,
             tok_scr, u_scr, ug_scr, yl_scr, x_scr, p_scr):
    t_, lt, gl = S5_CHUNK, 128, S5_TILE_GROUPS
    half = SSM_STATE
    sw = 4 * half
    l = nk * t_
    tw = t_ * SSM_GROUP

    def token_rows(t):
        if chunk_major:
            return [(k * bsz, bsz, k * t_ + t, l) for k in range(nk)]
        return [(b * nk, nk, b * l + t, t_) for b in range(bsz)]

    tok_scr[...] = u_ref[...].astype(F32)
    for t in range(t_):
        for r0, n, tok0, stride in token_rows(t):
            u_scr[r0:r0 + n, t * lt:(t + 1) * lt] = tok_scr[pl.ds(tok0, n, stride=stride), :].astype(BF16)
    ug_scr[...] = jnp.dot(u_scr[...], perm_ref[...], preferred_element_type=F32).astype(BF16)
    for gi in range(gl):
        z = jnp.dot(ug_scr[:, gi * tw:(gi + 1) * tw], w1_ref[gi], preferred_element_type=F32)
        yl_scr[:, gi * tw:(gi + 1) * tw] = z[:, :tw] + z[:, tw:2 * tw]
        x_scr[:, gi * sw:(gi + 1) * sw] = z[:, 2 * tw:]
    p_scr[...] = jnp.zeros(p_scr.shape, F32)

    srows = bsz if chunk_major else 8
    nset = 1 if chunk_major else bsz
    s0_of = (lambda q: slice(0, bsz)) if chunk_major else (lambda q: slice(q, q + 1))
    fwd_lane = lax.broadcasted_iota(jnp.int32, (srows, 2 * half), 1) < half
    row_id = lax.broadcasted_iota(jnp.int32, (srows, 2 * half), 0)
    lam = lam_ref[...]
    cols_a = lambda gi: pl.ds(gi * sw, 2 * half)
    cols_b = lambda gi: pl.ds(gi * sw + 2 * half, 2 * half)
    lam_of = lambda gi: (lam[:, gi * sw:gi * sw + 2 * half], lam[:, gi * sw + 2 * half:(gi + 1) * sw])

    def step(t, carry):
        kf = pl.ds(pl.multiple_of(t * bsz, bsz), bsz)
        kb = pl.ds(pl.multiple_of((nk - 1 - t) * bsz, bsz), bsz)
        new = []
        for gi in range(gl):
            ca, cb = cols_a(gi), cols_b(gi)
            sr, si = carry[gi]
            p_scr[kf, ca] = jnp.where(fwd_lane, sr, p_scr[kf, ca])
            p_scr[kb, ca] = jnp.where(fwd_lane, p_scr[kb, ca], sr)
            p_scr[kf, cb] = jnp.where(fwd_lane, si, p_scr[kf, cb])
            p_scr[kb, cb] = jnp.where(fwd_lane, p_scr[kb, cb], si)
            xr = jnp.where(fwd_lane, x_scr[kf, ca], x_scr[kb, ca])
            xi = jnp.where(fwd_lane, x_scr[kf, cb], x_scr[kb, cb])
            lr, li = lam_of(gi)
            new.append((lr * sr - li * si + xr, lr * si + li * sr + xi))
        return tuple(new)

    def step8(t8, carry):
        nb8 = nk // 8
        new = []
        for q in range(bsz):
            kf = pl.ds(pl.multiple_of(q * nk + t8 * 8, 8), 8)
            kb = pl.ds(pl.multiple_of(q * nk + (nb8 - 1 - t8) * 8, 8), 8)
            for gi in range(gl):
                ca, cb = cols_a(gi), cols_b(gi)
                sr, si = carry[q * gl + gi]
                xrf, xrb, xif, xib = x_scr[kf, ca], x_scr[kb, ca], x_scr[kf, cb], x_scr[kb, cb]
                prf, prb, pif, pib = p_scr[kf, ca], p_scr[kb, ca], p_scr[kf, cb], p_scr[kb, cb]
                lr, li = lam_of(gi)
                for j in range(8):
                    at_f = (row_id == j) & fwd_lane
                    at_b = (row_id == 7 - j) & jnp.logical_not(fwd_lane)
                    prf, prb = jnp.where(at_f, sr, prf), jnp.where(at_b, sr, prb)
                    pif, pib = jnp.where(at_f, si, pif), jnp.where(at_b, si, pib)
                    row = lambda a, i: jnp.broadcast_to(a[i:i + 1], a.shape)
                    xr = jnp.where(fwd_lane, row(xrf, j), row(xrb, 7 - j))
                    xi = jnp.where(fwd_lane, row(xif, j), row(xib, 7 - j))
                    sr, si = lr * sr - li * si + xr, lr * si + li * sr + xi
                p_scr[kf, ca], p_scr[kb, ca], p_scr[kf, cb], p_scr[kb, cb] = prf, prb, pif, pib
                new.append((sr, si))
        return tuple(new)

    bcast = lambda a: jnp.broadcast_to(a, (srows, 2 * half))
    init = tuple((bcast(s0_ref[s0_of(q), gi * sw:gi * sw + 2 * half]),
                  bcast(s0_ref[s0_of(q), gi * sw + 2 * half:(gi + 1) * sw]))
                 for q in range(nset) for gi in range(gl))
    if chunk_major:
        fin = lax.fori_loop(0, nk, step, init)
    else:
        fin = lax.fori_loop(0, nk // 8, step8, init)
    nfin = bsz if chunk_major else 1
    for q in range(nset):
        for gi in range(gl):
            sr, si = fin[q * gl + gi]
            fin_ref[s0_of(q), gi * sw:gi * sw + 2 * half] = sr[0:nfin]
            fin_ref[s0_of(q), gi * sw + 2 * half:(gi + 1) * sw] = si[0:nfin]

    for gi in range(gl):
        y = yl_scr[:, gi * tw:(gi + 1) * tw] + jnp.dot(p_scr[:, gi * sw:(gi + 1) * sw].astype(BF16), w2_ref[gi],
                                                       preferred_element_type=F32)
        u_scr[:, gi * tw:(gi + 1) * tw] = y.astype(BF16)
    yt = jnp.dot(u_scr[...], permt_ref[...], preferred_element_type=F32)
    for t in range(t_):
        for r0, n, tok0, stride in token_rows(t):
            tok_scr[pl.ds(tok0, n, stride=stride), :] = yt[r0:r0 + n, t * lt:(t + 1) * lt]
    y_ref[...] = tok_scr[...].astype(BF16)


def _s5(proj, col0, bsz, l, ops, s0):
    w1, w2, lam = ops
    g = w1.shape[0]
    t, ps, gl, lt = S5_CHUNK, SSM_STATE, S5_TILE_GROUPS, 128
    nb = g // gl
    nk = l // t
    r = bsz * l
    nrows = bsz * nk
    sw = 4 * ps
    cw = t * lt
    tw = t * SSM_GROUP
    chunk_major = bsz % 8 == 0
    assert chunk_major or nk % 8 == 0
    if s0 is None:
        s0p = jnp.zeros((bsz, g * sw), F32)
    else:
        s0_re, s0_im = s0
        s0p = jnp.concatenate([s0_re[:, 0], s0_re[:, 1], s0_im[:, 0], s0_im[:, 1]], axis=-1).reshape(bsz, g * sw)
    perm = _chunk_lane_permutation()
    const = lambda shape: _resident(shape, lambda i: (0,) * len(shape))
    y, fin = pl.pallas_call(
        functools.partial(_s5_body, bsz, nk, chunk_major),
        grid=(nb,),
        in_specs=[pl.BlockSpec((r, lt), lambda i: (0, col0 // lt + i), pipeline_mode=pl.Buffered(1)),
                  const((cw, cw)), const((cw, cw)),
                  pl.BlockSpec((gl, tw, 3 * tw), lambda i: (i, 0, 0)), pl.BlockSpec((gl, tw, tw), lambda i: (i, 0, 0)),
                  pl.BlockSpec((None, 1, gl * sw), lambda i: (i, 0, 0)),
                  pl.BlockSpec((bsz, gl * sw), lambda i: (0, i))],
        out_specs=[pl.BlockSpec((r, lt), lambda i: (0, i)), pl.BlockSpec((bsz, gl * sw), lambda i: (0, i))],
        out_shape=[jax.ShapeDtypeStruct((r, g * SSM_GROUP), BF16), jax.ShapeDtypeStruct((bsz, g * sw), F32)],
        scratch_shapes=[pltpu.VMEM((r, lt), F32), pltpu.VMEM((nrows, cw), BF16), pltpu.VMEM((nrows, cw), BF16),
                        pltpu.VMEM((nrows, cw), F32), pltpu.VMEM((nrows, gl * sw), F32),
                        pltpu.VMEM((nrows, gl * sw), F32)],
        compiler_params=_cparams(("parallel",)),
        name=f"s5_scan_{nk}",
    )(proj, jnp.asarray(perm, BF16), jnp.asarray(perm.T, BF16), w1, w2, lam, s0p)
    fin = jnp.transpose(fin.reshape(bsz, g, 2, 2, ps), (2, 0, 3, 1, 4))
    return y, fin[0], fin[1]


def _merge_body(yh_ref, ys_ref, u_ref, gh_ref, gs_ref, x_ref, d_ref, gw_ref, gbias_ref, wbh_ref, wbs_ref, wo_ref,
                g1_ref, n2_ref, sc2_ref, sh2_ref, x1_ref, h2_ref):
    y = ys_ref[...].astype(F32) + u_ref[...].astype(F32) * d_ref[...]
    g = jax.nn.gelu(y)
    gl = jnp.dot(g.astype(BF16), gw_ref[...], preferred_element_type=F32) + gbias_ref[...]
    ys = (g * jax.nn.sigmoid(gl)).astype(BF16)
    bh = jnp.dot(yh_ref[...], wbh_ref[...], preferred_element_type=F32)
    bs = jnp.dot(ys, wbs_ref[...], preferred_element_type=F32)
    merged = jax.nn.sigmoid(gh_ref[...].astype(F32)) * bh + jax.nn.sigmoid(gs_ref[...].astype(F32)) * bs
    x1 = x_ref[...] + g1_ref[...] * jnp.dot(merged.astype(BF16), wo_ref[...], preferred_element_type=F32)
    x1_ref[...] = x1
    h2_ref[...] = _mod_rmsnorm(x1, n2_ref[...], sc2_ref[...], sh2_ref[...]).astype(BF16)


def _merge(yh, ys, proj, x, mod4, cond_of_tile, ssm_d, glu_w, glu_b, wbh, wbs, wo, n2g, c_h, c_s, tm=256):
    r, d = x.shape
    row_blk = lambda width, k: pl.BlockSpec((tm, width), lambda i: (i, k))
    mod_spec = lambda chunk: pl.BlockSpec((None, None, 1, d), lambda i: (cond_of_tile(i, tm), chunk, 0, 0))
    const = lambda shape: _resident(shape, lambda i: (0,) * len(shape))
    u_col = (3 * c_h) // c_s
    gate0 = (3 * c_h + c_s) // d
    return pl.pallas_call(
        _merge_body,
        grid=(r // tm,),
        in_specs=[row_blk(c_h, 0), row_blk(c_s, 0), row_blk(c_s, u_col), row_blk(d, gate0), row_blk(d, gate0 + 1),
                  row_blk(d, 0), const((1, c_s)), const((c_s, c_s)), const((1, c_s)), const((c_h, d)),
                  const((c_s, d)), const((d, d)), mod_spec(2), const((1, d)), mod_spec(4), mod_spec(3)],
        out_specs=[row_blk(d, 0), row_blk(d, 0)],
        out_shape=[jax.ShapeDtypeStruct((r, d), F32), jax.ShapeDtypeStruct((r, d), BF16)],
        compiler_params=_cparams(("parallel",)),
        name="merge",
    )(yh, ys, proj, proj, proj, x, ssm_d.reshape(1, c_s), glu_w, glu_b.reshape(1, c_s), wbh, wbs, wo,
      mod4, n2g.reshape(1, d), mod4, mod4)


_FFN_RC = 512


def _ffn_up_body(rows, cols, h_ref, wa_ref, wb_ref, cwa_ref, cwb_ref, cba_ref, cbb_ref, o_ref):
    tm, tn = o_ref.shape
    rc = _FFN_RC
    n = tm // rc
    col = lax.rem(lax.broadcasted_iota(jnp.int32, (rc, tn), 0), cols)
    not_first = col != 0
    not_last = col != cols - 1
    zero = jnp.zeros((cols, tn), F32)

    def taps(hc, w_ref, cw_ref, cb_ref):
        a = jnp.dot(hc, w_ref[...], preferred_element_type=F32)
        am = jnp.where(not_first, pltpu.roll(a, 1, 0), 0.0)
        ap = jnp.where(not_last, pltpu.roll(a, rc - 1, 0), 0.0)
        cw = cw_ref[...]
        tap = lambda i: am * cw[3 * i:3 * i + 1] + a * cw[3 * i + 1:3 * i + 2] + ap * cw[3 * i + 2:3 * i + 3]
        mid = tap(1) + cb_ref[...]
        return (tap(0), mid, tap(2)) if rows > 1 else (None, mid, None)

    def finish(prev, cur, nxt):
        t0, mid, t2 = cur
        if rows == 1:
            return mid
        above = jnp.concatenate([prev[0][rc - cols:] if prev is not None else zero, t0[:rc - cols]], axis=0)
        below = jnp.concatenate([t2[cols:], nxt[2][:cols] if nxt is not None else zero], axis=0)
        return mid + above + below

    chunks = []
    for i in range(n + 1):
        if i < n:
            hc = h_ref[i * rc:(i + 1) * rc, :]
            chunks.append((taps(hc, wa_ref, cwa_ref, cba_ref), taps(hc, wb_ref, cwb_ref, cbb_ref)))
        if i >= 1:
            j = i - 1
            pick = lambda k, side: chunks[k][side] if 0 <= k < n else None
            a = finish(pick(j - 1, 0), pick(j, 0), pick(j + 1, 0))
            b = finish(pick(j - 1, 1), pick(j, 1), pick(j + 1, 1))
            o_ref[j * rc:(j + 1) * rc, :] = (jax.nn.gelu(a) * b).astype(BF16)


def _ffn_up(h2, w_up, conv_w, conv_b, rows, cols, tn=256):
    r, d = h2.shape
    f = w_up.shape[1] // 2
    tm = rows * cols if rows > 1 else 4096
    assert tm % _FFN_RC == 0 and _FFN_RC % cols == 0 and cols % 8 == 0
    nt = f // tn
    cw = conv_w.reshape(9, 2 * f)
    cb = conv_b.reshape(1, 2 * f)
    return pl.pallas_call(
        functools.partial(_ffn_up_body, rows, cols),
        grid=(r // tm, nt),
        in_specs=[pl.BlockSpec((tm, d), lambda i, j: (i, 0), pipeline_mode=pl.Buffered(1)),
                  pl.BlockSpec((d, tn), lambda i, j: (0, j)), pl.BlockSpec((d, tn), lambda i, j: (0, nt + j)),
                  pl.BlockSpec((9, tn), lambda i, j: (0, j)), pl.BlockSpec((9, tn), lambda i, j: (0, nt + j)),
                  pl.BlockSpec((1, tn), lambda i, j: (0, j)), pl.BlockSpec((1, tn), lambda i, j: (0, nt + j))],
        out_specs=pl.BlockSpec((tm, tn), lambda i, j: (i, j)),
        out_shape=jax.ShapeDtypeStruct((r, f), BF16),
        compiler_params=_cparams(("parallel", "arbitrary")),
        name=f"ffn_up_{rows}x{cols}",
    )(h2, w_up, w_up, cw, cw, cb, cb)


def _ffn_down_body(a_ref, w_ref, x_ref, g2_ref, fg_ref, o_ref):
    x2 = x_ref[...] + g2_ref[...] * jnp.dot(a_ref[...], w_ref[...], preferred_element_type=F32)
    ms = jnp.mean(x2 * x2, axis=-1, keepdims=True)
    o_ref[...] = x2 * lax.rsqrt(ms + EPS) * fg_ref[...]


def _ffn_down(act, w, x1, mod4, cond_of_tile, final_g, tm=256):
    r, f = act.shape
    d = w.shape[1]
    return pl.pallas_call(
        _ffn_down_body,
        grid=(r // tm,),
        in_specs=[pl.BlockSpec((tm, f), lambda i: (i, 0)),
                  _resident((f, d), lambda i: (0, 0)),
                  pl.BlockSpec((tm, d), lambda i: (i, 0)),
                  pl.BlockSpec((None, None, 1, d), lambda i: (cond_of_tile(i, tm), 5, 0, 0)),
                  _resident((1, d), lambda i: (0, 0))],
        out_specs=pl.BlockSpec((tm, d), lambda i: (i, 0)),
        out_shape=jax.ShapeDtypeStruct((r, d), F32),
        compiler_params=_cparams(("parallel",)),
        name="ffn_down",
    )(act, w, x1, mod4, final_g.reshape(1, d))


def _segment(x, cond_base, per_batch_cond, s0, rows, cols, hy_block, hy_nseq, mod4, p, final_g):
    bsz, l, d = x.shape
    xf = x.reshape(bsz * l, d)
    c_h = p["wbh"].shape[0]
    c_s = p["wbs"].shape[0]

    def cond_of_tile(i, tm):
        return cond_base + (i * tm) // l if per_batch_cond else cond_base

    proj = _in_proj(xf, mod4, cond_of_tile, p["norm1_g"], p["w_in"], p["b_in"])
    yh = _hyena(proj, l, hy_block, hy_nseq, p["hy_short_w"], p["hy_short_b"], p["hy_bias"], p["ghat"][l], c_h,
                ct=128 if l // hy_block > 1 else 256)
    ys, fin_re, fin_im = _s5(proj, 3 * c_h, bsz, l, p["s5_ops"], s0)
    x1, h2 = _merge(yh, ys, proj, xf, mod4, cond_of_tile, p["ssm_d"], p["glu_w"], p["glu_b"],
                    p["wbh"], p["wbs"], p["wo"], p["norm2_g"], c_h, c_s)
    act = _ffn_up(h2, p["ffn_up"], p["ffn_conv_w"], p["ffn_conv_b"], rows, cols)
    y = _ffn_down(act, p["ffn_down"], x1, mod4, cond_of_tile, final_g)
    return y.reshape(bsz, l, d), fin_re, fin_im


def kernel(x_prompt, x_sample, state_ssm_re, state_ssm_im, c, c_ctx, ada_w, ada_b, norm1_g, norm2_g, final_g, w_in, b_in, hy_short_w, hy_short_b, hy_f_w1, hy_f_b1, hy_f_w2, hy_f_b2, hy_f_w3, hy_f_b3, hy_f_w4, hy_f_freq, hy_bias, w_branch_h, ssm_lambda_re, ssm_lambda_im, ssm_log_dt, ssm_b_re, ssm_b_im, ssm_c_re, ssm_c_im, ssm_d, ssm_glu_w, ssm_glu_b, w_branch_s, w_out, ffn_up, ffn_conv_w, ffn_conv_b, ffn_down):
    depth = ada_w.shape[0]
    assert depth == 1, "the final norm is fused into the (single) layer's last kernel"
    d = x_prompt.shape[-1]
    ctx_len = x_prompt.shape[1]
    lat_len = x_sample.shape[1]
    dec_b = x_sample.shape[0]
    c_h = w_branch_h.shape[1]
    assert 1 + dec_b <= 8
    cond = jnp.zeros((8, d), F32).at[0].set(c_ctx).at[1:1 + dec_b].set(c)
    lay = 0
    mod4 = _ada(cond, ada_w[lay], ada_b[lay]).reshape(8, N_MOD, 1, d)
    hy_blocks = {ctx_len: 256, lat_len: 512}
    ghat = {l: _hyena_filters(l, hy_blocks[l], hy_f_w1[lay], hy_f_b1[lay], hy_f_w2[lay], hy_f_b2[lay],
                              hy_f_w3[lay], hy_f_b3[lay], hy_f_w4[lay], hy_f_freq[lay], c_h)
            for l in (ctx_len, lat_len)}
    s5_ops = _s5_operators(ssm_lambda_re[lay], ssm_lambda_im[lay], ssm_log_dt[lay],
                           ssm_b_re[lay], ssm_b_im[lay], ssm_c_re[lay], ssm_c_im[lay])
    p = dict(norm1_g=norm1_g[lay], norm2_g=norm2_g[lay], w_in=w_in[lay].astype(BF16), b_in=b_in[lay],
             hy_short_w=hy_short_w[lay], hy_short_b=hy_short_b[lay], hy_bias=hy_bias[lay], ghat=ghat,
             s5_ops=s5_ops, ssm_d=ssm_d[lay], glu_w=ssm_glu_w[lay].astype(BF16),
             glu_b=ssm_glu_b[lay], wbh=w_branch_h[lay].astype(BF16), wbs=w_branch_s[lay].astype(BF16),
             wo=w_out[lay].astype(BF16), ffn_up=ffn_up[lay].astype(BF16), ffn_conv_w=ffn_conv_w[lay],
             ffn_conv_b=ffn_conv_b[lay], ffn_down=ffn_down[lay].astype(BF16))
    y_prompt, st_re, st_im = _segment(x_prompt, 0, False, None, 1, ctx_len, hy_blocks[ctx_len], 8, mod4, p, final_g)
    s0 = (state_ssm_re[:, lay].astype(F32), state_ssm_im[:, lay].astype(F32))
    y_sample, _, _ = _segment(x_sample, 1, True, s0, lat_len // GRID_W, GRID_W, hy_blocks[lat_len], 1, mod4, p, final_g)
    return y_prompt, y_sample, st_re[:, None], st_im[:, None]
```

```python
import functools
import math

import jax
import jax.numpy as jnp
import numpy as np
from jax import lax
from jax.experimental import pallas as pl
from jax.experimental.pallas import tpu as pltpu

F32 = jnp.float32
BF16 = jnp.bfloat16
HIGHEST = lax.Precision.HIGHEST

GRID_W = 64
EPS = 1e-6
SSM_GROUP = 16
SSM_STATE = 64
FILTER_EMB = 33
FILTER_HIDDEN = 64
HYENA_ORDER = 2
DECAY_FAST = 0.3
DECAY_SLOW = 1.5
DECAY_TARGET = 1e-2
LAMBDA_RE_MAX = -1e-4

S5_CHUNK = 16
S5_TILE_GROUPS = 8
V7X_VMEM_LIMIT = 56 * 2**20
N_MOD = 6


def _cparams(sem):
    return pltpu.CompilerParams(dimension_semantics=sem, vmem_limit_bytes=V7X_VMEM_LIMIT)


def _resident(shape, index_map):
    return pl.BlockSpec(shape, index_map, pipeline_mode=pl.Buffered(1))


def _mod_rmsnorm(x, g, sc, sh):
    ms = jnp.mean(x * x, axis=-1, keepdims=True)
    return x * lax.rsqrt(ms + EPS) * g * (1.0 + sc) + sh


def _ada_body(c_ref, w_ref, b_ref, o_ref):
    c = c_ref[...]
    s = c * jax.nn.sigmoid(c)
    o_ref[...] = jnp.dot(s, w_ref[...], precision=HIGHEST, preferred_element_type=F32) + b_ref[...]


def _ada(cond, w, b):
    d, n = w.shape
    tn = 1024
    return pl.pallas_call(
        _ada_body,
        grid=(n // tn,),
        in_specs=[pl.BlockSpec((8, d), lambda j: (0, 0)),
                  pl.BlockSpec((d, tn), lambda j: (0, j)),
                  pl.BlockSpec((1, tn), lambda j: (0, j))],
        out_specs=pl.BlockSpec((8, tn), lambda j: (0, j)),
        out_shape=jax.ShapeDtypeStruct((8, n), F32),
        compiler_params=_cparams(("parallel",)),
        name="ada",
    )(cond, w, b.reshape(1, n))


def _inproj_body(x_ref, g_ref, sc_ref, sh_ref, w_ref, b_ref, o_ref, h_scr):
    @pl.when(pl.program_id(1) == 0)
    def _():
        h_scr[...] = _mod_rmsnorm(x_ref[...], g_ref[...], sc_ref[...], sh_ref[...]).astype(BF16)

    o_ref[...] = (jnp.dot(h_scr[...], w_ref[...], preferred_element_type=F32) + b_ref[...]).astype(BF16)


def _in_proj(x, mod4, cond_of_tile, g, w, b, tm=1024, tn=1024):
    r, d = x.shape
    n = w.shape[1]
    mod_spec = lambda chunk: pl.BlockSpec((None, None, 1, d), lambda i, j: (cond_of_tile(i, tm), chunk, 0, 0))
    return pl.pallas_call(
        _inproj_body,
        grid=(r // tm, n // tn),
        in_specs=[pl.BlockSpec((tm, d), lambda i, j: (i, 0)),
                  pl.BlockSpec((1, d), lambda i, j: (0, 0)),
                  mod_spec(1), mod_spec(0),
                  pl.BlockSpec((d, tn), lambda i, j: (0, j)),
                  pl.BlockSpec((1, tn), lambda i, j: (0, j))],
        out_specs=pl.BlockSpec((tm, tn), lambda i, j: (i, j)),
        out_shape=jax.ShapeDtypeStruct((r, n), BF16),
        scratch_shapes=[pltpu.VMEM((tm, d), BF16)],
        compiler_params=_cparams(("parallel", "arbitrary")),
        name="in_proj",
    )(x, g.reshape(1, d), mod4, mod4, w, b.reshape(1, n))


def _shifted_dft_tables(p):
    theta = np.pi * (2.0 * np.arange(p) + 1.0) / (2.0 * p)
    n = np.arange(p)
    a = theta[:, None] * n[None, :]
    fwd = np.concatenate([np.cos(a), -np.sin(a)], axis=0)
    m = np.arange(2 * p) - p
    am = theta[:, None] * m[None, :]
    filt = np.concatenate([np.cos(am), -np.sin(am)], axis=0)
    filt[:, 0] = 0.0
    inv = np.concatenate([np.cos(a).T, -np.sin(a).T], axis=1) / p
    return fwd, filt, inv


_FILTER_RC = 1024


def _filter_positions(l):
    pos = np.abs(np.arange(2 * l) - l).astype(np.float64)
    t = pos / (l - 1)
    bands = (FILTER_EMB - 1) // 2
    fr = np.linspace(1e-4, bands - 1, bands)
    ang = (2.0 * math.pi / l) * pos[:, None] * fr[None, :]
    z = np.concatenate([t[:, None], np.cos(ang), -np.sin(ang)], axis=-1)
    zp = np.zeros((2 * l, 128), np.float32)
    zp[:, :FILTER_EMB] = z
    return zp


def _filter_hidden_body(z_ref, w1_ref, b1_ref, w2_ref, b2_ref, w3_ref, b3_ref, fq_ref, o_ref):
    fq = fq_ref[...]
    dot = functools.partial(jnp.dot, precision=HIGHEST, preferred_element_type=F32)
    h = jnp.sin(fq * (dot(z_ref[...], w1_ref[...]) + b1_ref[...]))
    h = jnp.sin(fq * (dot(h, w2_ref[...]) + b2_ref[...]))
    o_ref[...] = jnp.sin(fq * (dot(h, w3_ref[...]) + b3_ref[...]))


def _filter_hidden(zp, w1p, b1, w2, b2, w3, b3, freq):
    rows = zp.shape[0]
    rc = min(rows, _FILTER_RC)
    hd = w2.shape[0]
    full = lambda shape: pl.BlockSpec(shape, lambda i: (0,) * len(shape))
    row = lambda a: a.reshape(1, -1)
    return pl.pallas_call(
        _filter_hidden_body,
        grid=(rows // rc,),
        in_specs=[pl.BlockSpec((rc, 128), lambda i: (i, 0)), full((128, hd)), full((1, hd)), full((hd, hd)),
                  full((1, hd)), full((hd, hd)), full((1, hd)), full((1, hd))],
        out_specs=pl.BlockSpec((rc, hd), lambda i: (i, 0)),
        out_shape=jax.ShapeDtypeStruct((rows, hd), F32),
        compiler_params=_cparams(("parallel",)),
        name=f"hyena_filter_mlp_{rows}",
    )(zp, w1p, row(b1), w2, row(b2), w3, row(b3), row(freq))


def _split_bf16(x):
    hi = x.astype(BF16)
    return hi, (x - hi.astype(F32)).astype(BF16)


def _dot_3pass(x, w_hi, w_lo):
    x_hi, x_lo = _split_bf16(x)
    d = functools.partial(jnp.dot, preferred_element_type=F32)
    return d(x_hi, w_hi) + (d(x_lo, w_hi) + d(x_hi, w_lo))


def _filter_body(l, p, h_ref, w4f_ref, w4b_ref, dl_ref, mg_ref, o_ref, kk_scr, mg_scr):
    mg_scr[...] = mg_ref[...].astype(BF16)
    rc = min(l, _FILTER_RC)
    ct = kk_scr.shape[1]
    nhalf = l // rc

    def taps(w_ref, first):
        w_hi, w_lo = _split_bf16(w_ref[...])

        def body(i, asum):
            r0 = pl.multiple_of(i * rc, rc)
            q = r0 + lax.broadcasted_iota(jnp.int32, (rc, ct), 0)
            t = jnp.abs(q - l).astype(F32) * (1.0 / (l - 1))
            k = _dot_3pass(h_ref[pl.ds(r0, rc), :], w_hi, w_lo) * jnp.exp(-t * dl_ref[...])
            k = jnp.where(q == 0, 0.0, k)
            kk_scr[pl.ds(r0, rc), :] = k.astype(BF16)
            return asum + jnp.sum(jnp.abs(k), axis=0, keepdims=True)

        return lambda asum: lax.fori_loop(first, first + nhalf, body, asum)

    asum = taps(w4f_ref, nhalf)(taps(w4b_ref, 0)(jnp.zeros((1, ct), F32)))
    scale = 1.0 / asum
    nwin = 2 * (l // p) - 1

    def win(w, _):
        seg = kk_scr[pl.ds(pl.multiple_of(w * p, p), 2 * p), :]
        o_ref[w] = jnp.dot(mg_scr[...], seg, preferred_element_type=F32) * scale
        return 0

    lax.fori_loop(0, nwin, win, 0)


def _hyena_filters(l, p, w1, b1, w2, b2, w3, b3, w4, freq, c, ct=256):
    nwin = 2 * (l // p) - 1
    nct = c // ct
    zp = jnp.asarray(_filter_positions(l))
    w1p = jnp.zeros((128, FILTER_HIDDEN), F32).at[:FILTER_EMB].set(w1)
    hid = _filter_hidden(zp, w1p, b1, w2, b2, w3, b3, freq)
    deltas = np.abs(np.linspace(math.log(DECAY_TARGET) / DECAY_SLOW, math.log(DECAY_TARGET) / DECAY_FAST, c))
    mg = jnp.asarray(_shifted_dft_tables(p)[1], F32)
    full = lambda shape: pl.BlockSpec(shape, lambda o, j: (0,) * len(shape))
    return pl.pallas_call(
        functools.partial(_filter_body, l, p),
        grid=(HYENA_ORDER, nct),
        in_specs=[full((2 * l, FILTER_HIDDEN)),
                  pl.BlockSpec((FILTER_HIDDEN, ct), lambda o, j: (0, o * 2 * nct + j)),
                  pl.BlockSpec((FILTER_HIDDEN, ct), lambda o, j: (0, o * 2 * nct + nct + j)),
                  pl.BlockSpec((1, ct), lambda o, j: (0, j)),
                  full((2 * p, 2 * p))],
        out_specs=pl.BlockSpec((None, nwin, 2 * p, ct), lambda o, j: (o, 0, 0, j)),
        out_shape=jax.ShapeDtypeStruct((HYENA_ORDER, nwin, 2 * p, c), F32),
        scratch_shapes=[pltpu.VMEM((2 * l, ct), BF16), pltpu.VMEM((2 * p, 2 * p), BF16)],
        compiler_params=_cparams(("parallel", "parallel")),
        name=f"hyena_filter_{l}",
    )(hid, w4, w4, jnp.asarray(deltas, F32).reshape(1, c), mg)


_HY_PASS_VREGS = 8


def _hyena_body(nseq, l, p, x1_ref, x2_ref, v_ref, w1_ref, w2_ref, wv_ref, b1_ref, b2_ref, bv_ref, hb_ref,
                g_ref, fz_ref, fi_ref, o_ref, z_scr, c_scr, zh_scr, yh_scr, z1_scr, fz_scr, fi_scr):
    fz_scr[...] = fz_ref[...].astype(BF16)
    fi_scr[...] = fi_ref[...].astype(BF16)
    o = pl.program_id(1)
    s = pl.program_id(2)
    rows = nseq * l
    nb = l // p
    ct = o_ref.shape[1]
    hrc = _HY_PASS_VREGS * 8 * 128 // ct
    pos = lax.rem(lax.broadcasted_iota(jnp.int32, (rows, ct), 0), l)

    def short_conv(u_ref, w_ref, b_ref):
        u = u_ref[...].astype(F32)
        um = jnp.where(pos == 0, 0.0, pltpu.roll(u, 1, 0))
        up = jnp.where(pos == l - 1, 0.0, pltpu.roll(u, rows - 1, 0))
        w = w_ref[...]
        return um * w[0:1] + u * w[1:2] + up * w[2:3] + b_ref[...]

    def long_conv():
        for q in range(nseq):
            base = q * l

            def fwd(j, _):
                zb = z_scr[pl.ds(pl.multiple_of(base + j * p, p), p), :].astype(BF16)
                zh_scr[j] = jnp.dot(fz_scr[...], zb, preferred_element_type=F32)
                return 0

            lax.fori_loop(0, nb, fwd, 0)

            def out_block(i, _):
                for rc in range(p // hrc):
                    re = pl.ds(rc * hrc, hrc)
                    im = pl.ds(p + rc * hrc, hrc)

                    def acc(j, carry):
                        ar, ai = carry
                        w = i - j + (nb - 1)
                        gr, gi = g_ref[w, re, :], g_ref[w, im, :]
                        zr, zi = zh_scr[j, re, :], zh_scr[j, im, :]
                        return ar + gr * zr - gi * zi, ai + gr * zi + gi * zr

                    zero = jnp.zeros((hrc, ct), F32)
                    ar, ai = lax.fori_loop(0, nb, acc, (zero, zero), unroll=True)
                    yh_scr[re, :] = ar
                    yh_scr[im, :] = ai
                c_scr[pl.ds(pl.multiple_of(base + i * p, p), p), :] = jnp.dot(
                    fi_scr[...], yh_scr[...].astype(BF16), preferred_element_type=F32)
                return 0

            lax.fori_loop(0, nb, out_block, 0)

    hb = hb_ref[...]

    @pl.when(o == 0)
    def _():
        v = short_conv(v_ref, wv_ref, bv_ref)
        z_scr[...] = v
        long_conv()
        x1 = short_conv(x1_ref, w1_ref, b1_ref)
        z1_scr[s] = x1 * (c_scr[...] + hb[0:1] * v)

    @pl.when(o == 1)
    def _():
        z1 = z1_scr[s]
        z_scr[...] = z1
        long_conv()
        x2 = short_conv(x2_ref, w2_ref, b2_ref)
        o_ref[...] = (x2 * (c_scr[...] + hb[1:2] * z1)).astype(BF16)


def _hyena(proj, l, p, nseq, short_w, short_b, hy_bias, ghat, c, ct=128):
    r = proj.shape[0]
    rows = nseq * l
    nsb = r // rows
    nct = c // ct
    nwin = ghat.shape[1]
    fwd, _, inv = _shifted_dft_tables(p)
    fz = jnp.asarray(fwd, F32)
    fi = jnp.asarray(inv, F32)
    sb = short_b.reshape(1, 3 * c)
    col = lambda k: (lambda j, o, s: (s, k * nct + j))
    wcol = lambda k: (lambda j, o, s: (0, k * nct + j))
    return pl.pallas_call(
        functools.partial(_hyena_body, nseq, l, p),
        grid=(nct, HYENA_ORDER, nsb),
        in_specs=[pl.BlockSpec((rows, ct), col(0)), pl.BlockSpec((rows, ct), col(1)), pl.BlockSpec((rows, ct), col(2)),
                  pl.BlockSpec((3, ct), wcol(0)), pl.BlockSpec((3, ct), wcol(1)), pl.BlockSpec((3, ct), wcol(2)),
                  pl.BlockSpec((1, ct), wcol(0)), pl.BlockSpec((1, ct), wcol(1)), pl.BlockSpec((1, ct), wcol(2)),
                  pl.BlockSpec((HYENA_ORDER, ct), lambda j, o, s: (0, j)),
                  pl.BlockSpec((None, nwin, 2 * p, ct), lambda j, o, s: (o, 0, 0, j)),
                  pl.BlockSpec((2 * p, p), lambda j, o, s: (0, 0)),
                  pl.BlockSpec((p, 2 * p), lambda j, o, s: (0, 0))],
        out_specs=pl.BlockSpec((rows, ct), lambda j, o, s: (s * o, j)),
        out_shape=jax.ShapeDtypeStruct((r, c), BF16),
        scratch_shapes=[pltpu.VMEM((rows, ct), F32), pltpu.VMEM((rows, ct), F32),
                        pltpu.VMEM((l // p, 2 * p, ct), F32), pltpu.VMEM((2 * p, ct), F32),
                        pltpu.VMEM((nsb, rows, ct), F32), pltpu.VMEM((2 * p, p), BF16), pltpu.VMEM((p, 2 * p), BF16)],
        compiler_params=_cparams(("parallel", "arbitrary", "arbitrary")),
        name=f"hyena_{l}",
    )(proj, proj, proj, short_w, short_w, short_w, sb, sb, sb, hy_bias, ghat, fz, fi)


def _s5_param_body(lre_ref, lim_ref, ldt_ref, btr_ref, bti_ref, cr_ref, ci_ref,
                   k_ref, wsr_ref, wsi_ref, wor_ref, woi_ref, etr_ref, eti_ref):
    t = S5_CHUNK
    backward = pl.program_id(0) == 1
    lr = jnp.minimum(lre_ref[...], LAMBDA_RE_MAX)
    li = lim_ref[...]
    dt = jnp.exp(ldt_ref[...])
    ar, ai = lr * dt, li * dt

    def power(e):
        mag = jnp.exp(e * ar)
        return mag * jnp.cos(e * ai), mag * jnp.sin(e * ai)

    per_step = lambda a: a[:, None, :]

    one = jnp.ones((1, 1), F32)
    lbr, lbi = power(one)
    nr, ni = lbr - 1.0, lbi
    den = lr * lr + li * li
    qr, qi = (nr * lr + ni * li) / den, (ni * lr - nr * li) / den
    btr, bti = btr_ref[...], bti_ref[...]
    bbr, bbi = qr * btr - qi * bti, qr * bti + qi * btr
    cr, ci = cr_ref[...], ci_ref[...]

    step = lax.broadcasted_iota(jnp.int32, (t, 1), 0).astype(F32)
    er, ei = map(per_step, power(step))
    cer, cei = cr[None] * er - ci[None] * ei, cr[None] * ei + ci[None] * er
    nt = (((1,), (1,)), ((), ()))
    dg = functools.partial(lax.dot_general, dimension_numbers=nt, precision=HIGHEST, preferred_element_type=F32)
    k_ref[...] = (dg(cer.reshape(t * SSM_GROUP, SSM_STATE), bbr)
                  - dg(cei.reshape(t * SSM_GROUP, SSM_STATE), bbi))
    er, ei = map(per_step, power(jnp.where(backward, step, (t - 1.0) - step)))
    wsr_ref[...] = er * bbr[None] - ei * bbi[None]
    wsi_ref[...] = er * bbi[None] + ei * bbr[None]
    er, ei = map(per_step, power(jnp.where(backward, t - step, step + 1.0)))
    wor_ref[...] = cr[None] * er - ci[None] * ei
    woi_ref[...] = -(cr[None] * ei + ci[None] * er)
    etr, eti = power(one * t)
    etr_ref[...] = etr
    eti_ref[...] = eti


def _s5_param_tile_body(*refs):
    for gi in range(refs[0].shape[0]):
        _s5_param_body(*(r.at[gi] for r in refs))


def _s5_params(lam_re, lam_im, log_dt, b_re, b_im, c_re, c_im):
    g = lam_re.shape[1]
    gb = S5_TILE_GROUPS
    t, cg, ps = S5_CHUNK, SSM_GROUP, SSM_STATE
    vec = lambda a: a.reshape(2, g, 1, ps)
    ldt = jnp.broadcast_to(log_dt[:, :, None, None], (2, g, 1, ps))
    bt = lambda a: jnp.swapaxes(a, -1, -2)
    dg_spec = lambda *shape: pl.BlockSpec((None, gb) + shape, lambda d, i: (d, i) + (0,) * len(shape))
    outs = [jax.ShapeDtypeStruct((2, g, t * cg, cg), F32)] + \
           [jax.ShapeDtypeStruct((2, g, t, cg, ps), F32)] * 4 + [jax.ShapeDtypeStruct((2, g, 1, ps), F32)] * 2
    return pl.pallas_call(
        _s5_param_tile_body,
        grid=(2, g // gb),
        in_specs=[dg_spec(1, ps), dg_spec(1, ps), dg_spec(1, ps), dg_spec(cg, ps), dg_spec(cg, ps),
                  dg_spec(cg, ps), dg_spec(cg, ps)],
        out_specs=[dg_spec(t * cg, cg)] + [dg_spec(t, cg, ps)] * 4 + [dg_spec(1, ps)] * 2,
        out_shape=outs,
        compiler_params=_cparams(("arbitrary", "arbitrary")),
        name="s5_params",
    )(vec(lam_re), vec(lam_im), ldt, bt(b_re), bt(b_im), c_re, c_im)


def _s5_operators(lam_re, lam_im, log_dt, b_re, b_im, c_re, c_im):
    kk, wsr, wsi, wor, woi, etr, eti = _s5_params(lam_re, lam_im, log_dt, b_re, b_im, c_re, c_im)
    kk, wsr, wsi, wor, woi = (a.astype(BF16) for a in (kk, wsr, wsi, wor, woi))
    g = kk.shape[1]
    t, cg, ps = S5_CHUNK, SSM_GROUP, SSM_STATE
    kd = kk.reshape(2, g, t, cg, cg)
    s_idx = np.arange(t)[:, None]
    t_idx = np.arange(t)[None, :]

    def toeplitz(kdir, lag):
        m = jnp.where((lag >= 0)[None, :, :, None, None], kdir[:, np.mod(lag, t)], 0)
        return jnp.transpose(m, (0, 1, 4, 2, 3)).reshape(g, t * cg, t * cg)

    mf = toeplitz(kd[0], t_idx - s_idx)
    mb = toeplitz(kd[1], s_idx - t_idx)
    flat = lambda a: a.reshape(g, t * cg, ps)
    w1 = jnp.concatenate([mf, mb, flat(wsr[0]), flat(wsr[1]), flat(wsi[0]), flat(wsi[1])], axis=-1)
    tr = lambda a: jnp.swapaxes(flat(a), 1, 2)
    w2 = jnp.concatenate([tr(wor[0]), tr(wor[1]), tr(woi[0]), tr(woi[1])], axis=1)
    lam = jnp.concatenate([etr[0], etr[1], eti[0], eti[1]], axis=-1)
    lam = lam.reshape(g // S5_TILE_GROUPS, 1, S5_TILE_GROUPS * 4 * ps)
    return w1, w2, lam


def _chunk_lane_permutation():
    t, gl, cg = S5_CHUNK, S5_TILE_GROUPS, SSM_GROUP
    src = np.arange(t * gl * cg).reshape(t, gl, cg)
    dst = np.transpose(src, (1, 0, 2)).reshape(-1)
    perm = np.zeros((t * gl * cg, t * gl * cg), np.float32)
    perm[dst, np.arange(t * gl * cg)] = 1.0
    return perm


def _s5_body(bsz, nk, chunk_major, u_ref, perm_ref, permt_ref, w1_ref, w2_ref, lam_ref, s0_ref, y_ref, fin_ref,
             tok_scr, u_scr, ug_scr, yl_scr, x_scr, p_scr):
    t_, lt, gl = S5_CHUNK, 128, S5_TILE_GROUPS
    half = SSM_STATE
    sw = 4 * half
    l = nk * t_
    tw = t_ * SSM_GROUP

    def token_rows(t):
        if chunk_major:
            return [(k * bsz, bsz, k * t_ + t, l) for k in range(nk)]
        return [(b * nk, nk, b * l + t, t_) for b in range(bsz)]

    tok_scr[...] = u_ref[...].astype(F32)
    for t in range(t_):
        for r0, n, tok0, stride in token_rows(t):
            u_scr[r0:r0 + n, t * lt:(t + 1) * lt] = tok_scr[pl.ds(tok0, n, stride=stride), :].astype(BF16)
    ug_scr[...] = jnp.dot(u_scr[...], perm_ref[...], preferred_element_type=F32).astype(BF16)
    for gi in range(gl):
        z = jnp.dot(ug_scr[:, gi * tw:(gi + 1) * tw], w1_ref[gi], preferred_element_type=F32)
        yl_scr[:, gi * tw:(gi + 1) * tw] = z[:, :tw] + z[:, tw:2 * tw]
        x_scr[:, gi * sw:(gi + 1) * sw] = z[:, 2 * tw:]
    p_scr[...] = jnp.zeros(p_scr.shape, F32)

    srows = bsz if chunk_major else 8
    nset = 1 if chunk_major else bsz
    s0_of = (lambda q: slice(0, bsz)) if chunk_major else (lambda q: slice(q, q + 1))
    fwd_lane = lax.broadcasted_iota(jnp.int32, (srows, 2 * half), 1) < half
    row_id = lax.broadcasted_iota(jnp.int32, (srows, 2 * half), 0)
    lam = lam_ref[...]
    cols_a = lambda gi: pl.ds(gi * sw, 2 * half)
    cols_b = lambda gi: pl.ds(gi * sw + 2 * half, 2 * half)
    lam_of = lambda gi: (lam[:, gi * sw:gi * sw + 2 * half], lam[:, gi * sw + 2 * half:(gi + 1) * sw])

    def step(t, carry):
        kf = pl.ds(pl.multiple_of(t * bsz, bsz), bsz)
        kb = pl.ds(pl.multiple_of((nk - 1 - t) * bsz, bsz), bsz)
        new = []
        for gi in range(gl):
            ca, cb = cols_a(gi), cols_b(gi)
            sr, si = carry[gi]
            p_scr[kf, ca] = jnp.where(fwd_lane, sr, p_scr[kf, ca])
            p_scr[kb, ca] = jnp.where(fwd_lane, p_scr[kb, ca], sr)
            p_scr[kf, cb] = jnp.where(fwd_lane, si, p_scr[kf, cb])
            p_scr[kb, cb] = jnp.where(fwd_lane, p_scr[kb, cb], si)
            xr = jnp.where(fwd_lane, x_scr[kf, ca], x_scr[kb, ca])
            xi = jnp.where(fwd_lane, x_scr[kf, cb], x_scr[kb, cb])
            lr, li = lam_of(gi)
            new.append((lr * sr - li * si + xr, lr * si + li * sr + xi))
        return tuple(new)

    def step8(t8, carry):
        nb8 = nk // 8
        new = []
        for q in range(bsz):
            kf = pl.ds(pl.multiple_of(q * nk + t8 * 8, 8), 8)
            kb = pl.ds(pl.multiple_of(q * nk + (nb8 - 1 - t8) * 8, 8), 8)
            for gi in range(gl):
                ca, cb = cols_a(gi), cols_b(gi)
                sr, si = carry[q * gl + gi]
                xrf, xrb, xif, xib = x_scr[kf, ca], x_scr[kb, ca], x_scr[kf, cb], x_scr[kb, cb]
                prf, prb, pif, pib = p_scr[kf, ca], p_scr[kb, ca], p_scr[kf, cb], p_scr[kb, cb]
                lr, li = lam_of(gi)
                for j in range(8):
                    at_f = (row_id == j) & fwd_lane
                    at_b = (row_id == 7 - j) & jnp.logical_not(fwd_lane)
                    prf, prb = jnp.where(at_f, sr, prf), jnp.where(at_b, sr, prb)
                    pif, pib = jnp.where(at_f, si, pif), jnp.where(at_b, si, pib)
                    row = lambda a, i: jnp.broadcast_to(a[i:i + 1], a.shape)
                    xr = jnp.where(fwd_lane, row(xrf, j), row(xrb, 7 - j))
                    xi = jnp.where(fwd_lane, row(xif, j), row(xib, 7 - j))
                    sr, si = lr * sr - li * si + xr, lr * si + li * sr + xi
                p_scr[kf, ca], p_scr[kb, ca], p_scr[kf, cb], p_scr[kb, cb] = prf, prb, pif, pib
                new.append((sr, si))
        return tuple(new)

    bcast = lambda a: jnp.broadcast_to(a, (srows, 2 * half))
    init = tuple((bcast(s0_ref[s0_of(q), gi * sw:gi * sw + 2 * half]),
                  bcast(s0_ref[s0_of(q), gi * sw + 2 * half:(gi + 1) * sw]))
                 for q in range(nset) for gi in range(gl))
    if chunk_major:
        fin = lax.fori_loop(0, nk, step, init)
    else:
        fin = lax.fori_loop(0, nk // 8, step8, init)
    nfin = bsz if chunk_major else 1
    for q in range(nset):
        for gi in range(gl):
            sr, si = fin[q * gl + gi]
            fin_ref[s0_of(q), gi * sw:gi * sw + 2 * half] = sr[0:nfin]
            fin_ref[s0_of(q), gi * sw + 2 * half:(gi + 1) * sw] = si[0:nfin]

    for gi in range(gl):
        y = yl_scr[:, gi * tw:(gi + 1) * tw] + jnp.dot(p_scr[:, gi * sw:(gi + 1) * sw].astype(BF16), w2_ref[gi],
                                                       preferred_element_type=F32)
        u_scr[:, gi * tw:(gi + 1) * tw] = y.astype(BF16)
    yt = jnp.dot(u_scr[...], permt_ref[...], preferred_element_type=F32)
    for t in range(t_):
        for r0, n, tok0, stride in token_rows(t):
            tok_scr[pl.ds(tok0, n, stride=stride), :] = yt[r0:r0 + n, t * lt:(t + 1) * lt]
    y_ref[...] = tok_scr[...].astype(BF16)


def _s5(proj, col0, bsz, l, ops, s0):
    w1, w2, lam = ops
    g = w1.shape[0]
    t, ps, gl, lt = S5_CHUNK, SSM_STATE, S5_TILE_GROUPS, 128
    nb = g // gl
    nk = l // t
    r = bsz * l
    nrows = bsz * nk
    sw = 4 * ps
    cw = t * lt
    tw = t * SSM_GROUP
    chunk_major = bsz % 8 == 0
    assert chunk_major or nk % 8 == 0
    if s0 is None:
        s0p = jnp.zeros((bsz, g * sw), F32)
    else:
        s0_re, s0_im = s0
        s0p = jnp.concatenate([s0_re[:, 0], s0_re[:, 1], s0_im[:, 0], s0_im[:, 1]], axis=-1).reshape(bsz, g * sw)
    perm = _chunk_lane_permutation()
    const = lambda shape: _resident(shape, lambda i: (0,) * len(shape))
    y, fin = pl.pallas_call(
        functools.partial(_s5_body, bsz, nk, chunk_major),
        grid=(nb,),
        in_specs=[pl.BlockSpec((r, lt), lambda i: (0, col0 // lt + i), pipeline_mode=pl.Buffered(1)),
                  const((cw, cw)), const((cw, cw)),
                  pl.BlockSpec((gl, tw, 3 * tw), lambda i: (i, 0, 0)), pl.BlockSpec((gl, tw, tw), lambda i: (i, 0, 0)),
                  pl.BlockSpec((None, 1, gl * sw), lambda i: (i, 0, 0)),
                  pl.BlockSpec((bsz, gl * sw), lambda i: (0, i))],
        out_specs=[pl.BlockSpec((r, lt), lambda i: (0, i)), pl.BlockSpec((bsz, gl * sw), lambda i: (0, i))],
        out_shape=[jax.ShapeDtypeStruct((r, g * SSM_GROUP), BF16), jax.ShapeDtypeStruct((bsz, g * sw), F32)],
        scratch_shapes=[pltpu.VMEM((r, lt), F32), pltpu.VMEM((nrows, cw), BF16), pltpu.VMEM((nrows, cw), BF16),
                        pltpu.VMEM((nrows, cw), F32), pltpu.VMEM((nrows, gl * sw), F32),
                        pltpu.VMEM((nrows, gl * sw), F32)],
        compiler_params=_cparams(("parallel",)),
        name=f"s5_scan_{nk}",
    )(proj, jnp.asarray(perm, BF16), jnp.asarray(perm.T, BF16), w1, w2, lam, s0p)
    fin = jnp.transpose(fin.reshape(bsz, g, 2, 2, ps), (2, 0, 3, 1, 4))
    return y, fin[0], fin[1]


def _merge_body(yh_ref, ys_ref, u_ref, gh_ref, gs_ref, x_ref, d_ref, gw_ref, gbias_ref, wbh_ref, wbs_ref, wo_ref,
                g1_ref, n2_ref, sc2_ref, sh2_ref, x1_ref, h2_ref):
    y = ys_ref[...].astype(F32) + u_ref[...].astype(F32) * d_ref[...]
    g = jax.nn.gelu(y)
    gl = jnp.dot(g.astype(BF16), gw_ref[...], preferred_element_type=F32) + gbias_ref[...]
    ys = (g * jax.nn.sigmoid(gl)).astype(BF16)
    bh = jnp.dot(yh_ref[...], wbh_ref[...], preferred_element_type=F32)
    bs = jnp.dot(ys, wbs_ref[...], preferred_element_type=F32)
    merged = jax.nn.sigmoid(gh_ref[...].astype(F32)) * bh + jax.nn.sigmoid(gs_ref[...].astype(F32)) * bs
    x1 = x_ref[...] + g1_ref[...] * jnp.dot(merged.astype(BF16), wo_ref[...], preferred_element_type=F32)
    x1_ref[...] = x1
    h2_ref[...] = _mod_rmsnorm(x1, n2_ref[...], sc2_ref[...], sh2_ref[...]).astype(BF16)


def _merge(yh, ys, proj, x, mod4, cond_of_tile, ssm_d, glu_w, glu_b, wbh, wbs, wo, n2g, c_h, c_s, tm=256):
    r, d = x.shape
    row_blk = lambda width, k: pl.BlockSpec((tm, width), lambda i: (i, k))
    mod_spec = lambda chunk: pl.BlockSpec((None, None, 1, d), lambda i: (cond_of_tile(i, tm), chunk, 0, 0))
    const = lambda shape: _resident(shape, lambda i: (0,) * len(shape))
    u_col = (3 * c_h) // c_s
    gate0 = (3 * c_h + c_s) // d
    return pl.pallas_call(
        _merge_body,
        grid=(r // tm,),
        in_specs=[row_blk(c_h, 0), row_blk(c_s, 0), row_blk(c_s, u_col), row_blk(d, gate0), row_blk(d, gate0 + 1),
                  row_blk(d, 0), const((1, c_s)), const((c_s, c_s)), const((1, c_s)), const((c_h, d)),
                  const((c_s, d)), const((d, d)), mod_spec(2), const((1, d)), mod_spec(4), mod_spec(3)],
        out_specs=[row_blk(d, 0), row_blk(d, 0)],
        out_shape=[jax.ShapeDtypeStruct((r, d), F32), jax.ShapeDtypeStruct((r, d), BF16)],
        compiler_params=_cparams(("parallel",)),
        name="merge",
    )(yh, ys, proj, proj, proj, x, ssm_d.reshape(1, c_s), glu_w, glu_b.reshape(1, c_s), wbh, wbs, wo,
      mod4, n2g.reshape(1, d), mod4, mod4)


_FFN_RC = 512


def _ffn_up_body(rows, cols, h_ref, wa_ref, wb_ref, cwa_ref, cwb_ref, cba_ref, cbb_ref, o_ref):
    tm, tn = o_ref.shape
    rc = _FFN_RC
    n = tm // rc
    col = lax.rem(lax.broadcasted_iota(jnp.int32, (rc, tn), 0), cols)
    not_first = col != 0
    not_last = col != cols - 1
    zero = jnp.zeros((cols, tn), F32)

    def taps(hc, w_ref, cw_ref, cb_ref):
        a = jnp.dot(hc, w_ref[...], preferred_element_type=F32)
        am = jnp.where(not_first, pltpu.roll(a, 1, 0), 0.0)
        ap = jnp.where(not_last, pltpu.roll(a, rc - 1, 0), 0.0)
        cw = cw_ref[...]
        tap = lambda i: am * cw[3 * i:3 * i + 1] + a * cw[3 * i + 1:3 * i + 2] + ap * cw[3 * i + 2:3 * i + 3]
        mid = tap(1) + cb_ref[...]
        return (tap(0), mid, tap(2)) if rows > 1 else (None, mid, None)

    def finish(prev, cur, nxt):
        t0, mid, t2 = cur
        if rows == 1:
            return mid
        above = jnp.concatenate([prev[0][rc - cols:] if prev is not None else zero, t0[:rc - cols]], axis=0)
        below = jnp.concatenate([t2[cols:], nxt[2][:cols] if nxt is not None else zero], axis=0)
        return mid + above + below

    chunks = []
    for i in range(n + 1):
        if i < n:
            hc = h_ref[i * rc:(i + 1) * rc, :]
            chunks.append((taps(hc, wa_ref, cwa_ref, cba_ref), taps(hc, wb_ref, cwb_ref, cbb_ref)))
        if i >= 1:
            j = i - 1
            pick = lambda k, side: chunks[k][side] if 0 <= k < n else None
            a = finish(pick(j - 1, 0), pick(j, 0), pick(j + 1, 0))
            b = finish(pick(j - 1, 1), pick(j, 1), pick(j + 1, 1))
            o_ref[j * rc:(j + 1) * rc, :] = (jax.nn.gelu(a) * b).astype(BF16)


def _ffn_up(h2, w_up, conv_w, conv_b, rows, cols, tn=256):
    r, d = h2.shape
    f = w_up.shape[1] // 2
    tm = rows * cols if rows > 1 else 4096
    assert tm % _FFN_RC == 0 and _FFN_RC % cols == 0 and cols % 8 == 0
    nt = f // tn
    cw = conv_w.reshape(9, 2 * f)
    cb = conv_b.reshape(1, 2 * f)
    return pl.pallas_call(
        functools.partial(_ffn_up_body, rows, cols),
        grid=(r // tm, nt),
        in_specs=[pl.BlockSpec((tm, d), lambda i, j: (i, 0), pipeline_mode=pl.Buffered(1)),
                  pl.BlockSpec((d, tn), lambda i, j: (0, j)), pl.BlockSpec((d, tn), lambda i, j: (0, nt + j)),
                  pl.BlockSpec((9, tn), lambda i, j: (0, j)), pl.BlockSpec((9, tn), lambda i, j: (0, nt + j)),
                  pl.BlockSpec((1, tn), lambda i, j: (0, j)), pl.BlockSpec((1, tn), lambda i, j: (0, nt + j))],
        out_specs=pl.BlockSpec((tm, tn), lambda i, j: (i, j)),
        out_shape=jax.ShapeDtypeStruct((r, f), BF16),
        compiler_params=_cparams(("parallel", "arbitrary")),
        name=f"ffn_up_{rows}x{cols}",
    )(h2, w_up, w_up, cw, cw, cb, cb)


def _ffn_down_body(a_ref, w_ref, x_ref, g2_ref, fg_ref, o_ref):
    x2 = x_ref[...] + g2_ref[...] * jnp.dot(a_ref[...], w_ref[...], preferred_element_type=F32)
    ms = jnp.mean(x2 * x2, axis=-1, keepdims=True)
    o_ref[...] = x2 * lax.rsqrt(ms + EPS) * fg_ref[...]


def _ffn_down(act, w, x1, mod4, cond_of_tile, final_g, tm=256):
    r, f = act.shape
    d = w.shape[1]
    return pl.pallas_call(
        _ffn_down_body,
        grid=(r // tm,),
        in_specs=[pl.BlockSpec((tm, f), lambda i: (i, 0)),
                  _resident((f, d), lambda i: (0, 0)),
                  pl.BlockSpec((tm, d), lambda i: (i, 0)),
                  pl.BlockSpec((None, None, 1, d), lambda i: (cond_of_tile(i, tm), 5, 0, 0)),
                  _resident((1, d), lambda i: (0, 0))],
        out_specs=pl.BlockSpec((tm, d), lambda i: (i, 0)),
        out_shape=jax.ShapeDtypeStruct((r, d), F32),
        compiler_params=_cparams(("parallel",)),
        name="ffn_down",
    )(act, w, x1, mod4, final_g.reshape(1, d))


def _segment(x, cond_base, per_batch_cond, s0, rows, cols, hy_block, hy_nseq, mod4, p, final_g):
    bsz, l, d = x.shape
    xf = x.reshape(bsz * l, d)
    c_h = p["wbh"].shape[0]
    c_s = p["wbs"].shape[0]

    def cond_of_tile(i, tm):
        return cond_base + (i * tm) // l if per_batch_cond else cond_base

    proj = _in_proj(xf, mod4, cond_of_tile, p["norm1_g"], p["w_in"], p["b_in"])
    yh = _hyena(proj, l, hy_block, hy_nseq, p["hy_short_w"], p["hy_short_b"], p["hy_bias"], p["ghat"][l], c_h,
                ct=128 if l // hy_block > 1 else 256)
    ys, fin_re, fin_im = _s5(proj, 3 * c_h, bsz, l, p["s5_ops"], s0)
    x1, h2 = _merge(yh, ys, proj, xf, mod4, cond_of_tile, p["ssm_d"], p["glu_w"], p["glu_b"],
                    p["wbh"], p["wbs"], p["wo"], p["norm2_g"], c_h, c_s)
    act = _ffn_up(h2, p["ffn_up"], p["ffn_conv_w"], p["ffn_conv_b"], rows, cols)
    y = _ffn_down(act, p["ffn_down"], x1, mod4, cond_of_tile, final_g)
    return y.reshape(bsz, l, d), fin_re, fin_im


def kernel(x_prompt, x_sample, state_ssm_re, state_ssm_im, c, c_ctx, ada_w, ada_b, norm1_g, norm2_g, final_g, w_in, b_in, hy_short_w, hy_short_b, hy_f_w1, hy_f_b1, hy_f_w2, hy_f_b2, hy_f_w3, hy_f_b3, hy_f_w4, hy_f_freq, hy_bias, w_branch_h, ssm_lambda_re, ssm_lambda_im, ssm_log_dt, ssm_b_re, ssm_b_im, ssm_c_re, ssm_c_im, ssm_d, ssm_glu_w, ssm_glu_b, w_branch_s, w_out, ffn_up, ffn_conv_w, ffn_conv_b, ffn_down):
    depth = ada_w.shape[0]
    assert depth == 1, "the final norm is fused into the (single) layer's last kernel"
    d = x_prompt.shape[-1]
    ctx_len = x_prompt.shape[1]
    lat_len = x_sample.shape[1]
    dec_b = x_sample.shape[0]
    c_h = w_branch_h.shape[1]
    assert 1 + dec_b <= 8
    cond = jnp.zeros((8, d), F32).at[0].set(c_ctx).at[1:1 + dec_b].set(c)
    lay = 0
    mod4 = _ada(cond, ada_w[lay], ada_b[lay]).reshape(8, N_MOD, 1, d)
    hy_blocks = {ctx_len: 256, lat_len: 512}
    ghat = {l: _hyena_filters(l, hy_blocks[l], hy_f_w1[lay], hy_f_b1[lay], hy_f_w2[lay], hy_f_b2[lay],
                              hy_f_w3[lay], hy_f_b3[lay], hy_f_w4[lay], hy_f_freq[lay], c_h)
            for l in (ctx_len, lat_len)}
    s5_ops = _s5_operators(ssm_lambda_re[lay], ssm_lambda_im[lay], ssm_log_dt[lay],
                           ssm_b_re[lay], ssm_b_im[lay], ssm_c_re[lay], ssm_c_im[lay])
    p = dict(norm1_g=norm1_g[lay], norm2_g=norm2_g[lay], w_in=w_in[lay].astype(BF16), b_in=b_in[lay],
             hy_short_w=hy_short_w[lay], hy_short_b=hy_short_b[lay], hy_bias=hy_bias[lay], ghat=ghat,
             s5_ops=s5_ops, ssm_d=ssm_d[lay], glu_w=ssm_glu_w[lay].astype(BF16),
             glu_b=ssm_glu_b[lay], wbh=w_branch_h[lay].astype(BF16), wbs=w_branch_s[lay].astype(BF16),
             wo=w_out[lay].astype(BF16), ffn_up=ffn_up[lay].astype(BF16), ffn_conv_w=ffn_conv_w[lay],
             ffn_conv_b=ffn_conv_b[lay], ffn_down=ffn_down[lay].astype(BF16))
    y_prompt, st_re, st_im = _segment(x_prompt, 0, False, None, 1, ctx_len, hy_blocks[ctx_len], 8, mod4, p, final_g)
    s0 = (state_ssm_re[:, lay].astype(F32), state_ssm_im[:, lay].astype(F32))
    y_sample, _, _ = _segment(x_sample, 1, True, s0, lat_len // GRID_W, GRID_W, hy_blocks[lat_len], 1, mod4, p, final_g)
    return y_prompt, y_sample, st_re[:, None], st_im[:, None]
```

```python
import functools
import math

import jax
import jax.numpy as jnp
import numpy as np
from jax import lax
from jax.experimental import pallas as pl
from jax.experimental.pallas import tpu as pltpu

F32 = jnp.float32
BF16 = jnp.bfloat16
HIGHEST = lax.Precision.HIGHEST

GRID_W = 64
EPS = 1e-6
SSM_GROUP = 16
SSM_STATE = 64
FILTER_EMB = 33
FILTER_HIDDEN = 64
HYENA_ORDER = 2
DECAY_FAST = 0.3
DECAY_SLOW = 1.5
DECAY_TARGET = 1e-2
LAMBDA_RE_MAX = -1e-4

S5_CHUNK = 16
S5_TILE_GROUPS = 8
V7X_VMEM_LIMIT = 56 * 2**20
N_MOD = 6


def _cparams(sem):
    return pltpu.CompilerParams(dimension_semantics=sem, vmem_limit_bytes=V7X_VMEM_LIMIT)


def _resident(shape, index_map):
    return pl.BlockSpec(shape, index_map, pipeline_mode=pl.Buffered(1))


def _mod_rmsnorm(x, g, sc, sh):
    ms = jnp.mean(x * x, axis=-1, keepdims=True)
    return x * lax.rsqrt(ms + EPS) * g * (1.0 + sc) + sh


def _ada_body(c_ref, w_ref, b_ref, o_ref):
    c = c_ref[...]
    s = c * jax.nn.sigmoid(c)
    o_ref[...] = jnp.dot(s, w_ref[...], precision=HIGHEST, preferred_element_type=F32) + b_ref[...]


def _ada(cond, w, b):
    d, n = w.shape
    tn = 1024
    return pl.pallas_call(
        _ada_body,
        grid=(n // tn,),
        in_specs=[pl.BlockSpec((8, d), lambda j: (0, 0)),
                  pl.BlockSpec((d, tn), lambda j: (0, j)),
                  pl.BlockSpec((1, tn), lambda j: (0, j))],
        out_specs=pl.BlockSpec((8, tn), lambda j: (0, j)),
        out_shape=jax.ShapeDtypeStruct((8, n), F32),
        compiler_params=_cparams(("parallel",)),
        name="ada",
    )(cond, w, b.reshape(1, n))


def _inproj_body(x_ref, g_ref, sc_ref, sh_ref, w_ref, b_ref, o_ref, h_scr):
    @pl.when(pl.program_id(1) == 0)
    def _():
        h_scr[...] = _mod_rmsnorm(x_ref[...], g_ref[...], sc_ref[...], sh_ref[...]).astype(BF16)

    o_ref[...] = (jnp.dot(h_scr[...], w_ref[...], preferred_element_type=F32) + b_ref[...]).astype(BF16)


def _in_proj(x, mod4, cond_of_tile, g, w, b, tm=1024, tn=1024):
    r, d = x.shape
    n = w.shape[1]
    mod_spec = lambda chunk: pl.BlockSpec((None, None, 1, d), lambda i, j: (cond_of_tile(i, tm), chunk, 0, 0))
    return pl.pallas_call(
        _inproj_body,
        grid=(r // tm, n // tn),
        in_specs=[pl.BlockSpec((tm, d), lambda i, j: (i, 0)),
                  pl.BlockSpec((1, d), lambda i, j: (0, 0)),
                  mod_spec(1), mod_spec(0),
                  pl.BlockSpec((d, tn), lambda i, j: (0, j)),
                  pl.BlockSpec((1, tn), lambda i, j: (0, j))],
        out_specs=pl.BlockSpec((tm, tn), lambda i, j: (i, j)),
        out_shape=jax.ShapeDtypeStruct((r, n), BF16),
        scratch_shapes=[pltpu.VMEM((tm, d), BF16)],
        compiler_params=_cparams(("parallel", "arbitrary")),
        name="in_proj",
    )(x, g.reshape(1, d), mod4, mod4, w, b.reshape(1, n))


def _shifted_dft_tables(p):
    theta = np.pi * (2.0 * np.arange(p) + 1.0) / (2.0 * p)
    n = np.arange(p)
    a = theta[:, None] * n[None, :]
    fwd = np.concatenate([np.cos(a), -np.sin(a)], axis=0)
    m = np.arange(2 * p) - p
    am = theta[:, None] * m[None, :]
    filt = np.concatenate([np.cos(am), -np.sin(am)], axis=0)
    filt[:, 0] = 0.0
    inv = np.concatenate([np.cos(a).T, -np.sin(a).T], axis=1) / p
    return fwd, filt, inv


_FILTER_RC = 1024


def _filter_positions(l):
    pos = np.abs(np.arange(2 * l) - l).astype(np.float64)
    t = pos / (l - 1)
    bands = (FILTER_EMB - 1) // 2
    fr = np.linspace(1e-4, bands - 1, bands)
    ang = (2.0 * math.pi / l) * pos[:, None] * fr[None, :]
    z = np.concatenate([t[:, None], np.cos(ang), -np.sin(ang)], axis=-1)
    zp = np.zeros((2 * l, 128), np.float32)
    zp[:, :FILTER_EMB] = z
    return zp


def _filter_hidden_body(z_ref, w1_ref, b1_ref, w2_ref, b2_ref, w3_ref, b3_ref, fq_ref, o_ref):
    fq = fq_ref[...]
    dot = functools.partial(jnp.dot, precision=HIGHEST, preferred_element_type=F32)
    h = jnp.sin(fq * (dot(z_ref[...], w1_ref[...]) + b1_ref[...]))
    h = jnp.sin(fq * (dot(h, w2_ref[...]) + b2_ref[...]))
    o_ref[...] = jnp.sin(fq * (dot(h, w3_ref[...]) + b3_ref[...]))


def _filter_hidden(zp, w1p, b1, w2, b2, w3, b3, freq):
    rows = zp.shape[0]
    rc = min(rows, _FILTER_RC)
    hd = w2.shape[0]
    full = lambda shape: pl.BlockSpec(shape, lambda i: (0,) * len(shape))
    row = lambda a: a.reshape(1, -1)
    return pl.pallas_call(
        _filter_hidden_body,
        grid=(rows // rc,),
        in_specs=[pl.BlockSpec((rc, 128), lambda i: (i, 0)), full((128, hd)), full((1, hd)), full((hd, hd)),
                  full((1, hd)), full((hd, hd)), full((1, hd)), full((1, hd))],
        out_specs=pl.BlockSpec((rc, hd), lambda i: (i, 0)),
        out_shape=jax.ShapeDtypeStruct((rows, hd), F32),
        compiler_params=_cparams(("parallel",)),
        name=f"hyena_filter_mlp_{rows}",
    )(zp, w1p, row(b1), w2, row(b2), w3, row(b3), row(freq))


def _split_bf16(x):
    hi = x.astype(BF16)
    return hi, (x - hi.astype(F32)).astype(BF16)


def _dot_3pass(x, w_hi, w_lo):
    x_hi, x_lo = _split_bf16(x)
    d = functools.partial(jnp.dot, preferred_element_type=F32)
    return d(x_hi, w_hi) + (d(x_lo, w_hi) + d(x_hi, w_lo))


def _filter_body(l, p, h_ref, w4f_ref, w4b_ref, dl_ref, mg_ref, o_ref, kk_scr, mg_scr):
    mg_scr[...] = mg_ref[...].astype(BF16)
    rc = min(l, _FILTER_RC)
    ct = kk_scr.shape[1]
    nhalf = l // rc

    def taps(w_ref, first):
        w_hi, w_lo = _split_bf16(w_ref[...])

        def body(i, asum):
            r0 = pl.multiple_of(i * rc, rc)
            q = r0 + lax.broadcasted_iota(jnp.int32, (rc, ct), 0)
            t = jnp.abs(q - l).astype(F32) * (1.0 / (l - 1))
            k = _dot_3pass(h_ref[pl.ds(r0, rc), :], w_hi, w_lo) * jnp.exp(-t * dl_ref[...])
            k = jnp.where(q == 0, 0.0, k)
            kk_scr[pl.ds(r0, rc), :] = k.astype(BF16)
            return asum + jnp.sum(jnp.abs(k), axis=0, keepdims=True)

        return lambda asum: lax.fori_loop(first, first + nhalf, body, asum)

    asum = taps(w4f_ref, nhalf)(taps(w4b_ref, 0)(jnp.zeros((1, ct), F32)))
    scale = 1.0 / asum
    nwin = 2 * (l // p) - 1

    def win(w, _):
        seg = kk_scr[pl.ds(pl.multiple_of(w * p, p), 2 * p), :]
        o_ref[w] = jnp.dot(mg_scr[...], seg, preferred_element_type=F32) * scale
        return 0

    lax.fori_loop(0, nwin, win, 0)


def _hyena_filters(l, p, w1, b1, w2, b2, w3, b3, w4, freq, c, ct=256):
    nwin = 2 * (l // p) - 1
    nct = c // ct
    zp = jnp.asarray(_filter_positions(l))
    w1p = jnp.zeros((128, FILTER_HIDDEN), F32).at[:FILTER_EMB].set(w1)
    hid = _filter_hidden(zp, w1p, b1, w2, b2, w3, b3, freq)
    deltas = np.abs(np.linspace(math.log(DECAY_TARGET) / DECAY_SLOW, math.log(DECAY_TARGET) / DECAY_FAST, c))
    mg = jnp.asarray(_shifted_dft_tables(p)[1], F32)
    full = lambda shape: pl.BlockSpec(shape, lambda o, j: (0,) * len(shape))
    return pl.pallas_call(
        functools.partial(_filter_body, l, p),
        grid=(HYENA_ORDER, nct),
        in_specs=[full((2 * l, FILTER_HIDDEN)),
                  pl.BlockSpec((FILTER_HIDDEN, ct), lambda o, j: (0, o * 2 * nct + j)),
                  pl.BlockSpec((FILTER_HIDDEN, ct), lambda o, j: (0, o * 2 * nct + nct + j)),
                  pl.BlockSpec((1, ct), lambda o, j: (0, j)),
                  full((2 * p, 2 * p))],
        out_specs=pl.BlockSpec((None, nwin, 2 * p, ct), lambda o, j: (o, 0, 0, j)),
        out_shape=jax.ShapeDtypeStruct((HYENA_ORDER, nwin, 2 * p, c), F32),
        scratch_shapes=[pltpu.VMEM((2 * l, ct), BF16), pltpu.VMEM((2 * p, 2 * p), BF16)],
        compiler_params=_cparams(("parallel", "parallel")),
        name=f"hyena_filter_{l}",
    )(hid, w4, w4, jnp.asarray(deltas, F32).reshape(1, c), mg)


_HY_PASS_VREGS = 8


def _hyena_body(nseq, l, p, x1_ref, x2_ref, v_ref, w1_ref, w2_ref, wv_ref, b1_ref, b2_ref, bv_ref, hb_ref,
                g_ref, fz_ref, fi_ref, o_ref, z_scr, c_scr, zh_scr, yh_scr, z1_scr, fz_scr, fi_scr):
    fz_scr[...] = fz_ref[...].astype(BF16)
    fi_scr[...] = fi_ref[...].astype(BF16)
    o = pl.program_id(1)
    s = pl.program_id(2)
    rows = nseq * l
    nb = l // p
    ct = o_ref.shape[1]
    hrc = _HY_PASS_VREGS * 8 * 128 // ct
    pair = yh_scr.shape[1] // ct
    pos = lax.rem(lax.broadcasted_iota(jnp.int32, (rows, ct), 0), l)

    def short_conv(u_ref, w_ref, b_ref):
        u = u_ref[...].astype(F32)
        um = jnp.where(pos == 0, 0.0, pltpu.roll(u, 1, 0))
        up = jnp.where(pos == l - 1, 0.0, pltpu.roll(u, rows - 1, 0))
        w = w_ref[...]
        return um * w[0:1] + u * w[1:2] + up * w[2:3] + b_ref[...]

    def long_conv():
        for q in range(nseq):
            base = q * l

            def block(k):
                return pl.ds(pl.multiple_of(base + k * p, p), p)

            def fwd(jj, _):
                zb = jnp.concatenate([z_scr[block(jj * pair + k), :] for k in range(pair)], axis=1).astype(BF16)
                zh = jnp.dot(fz_scr[...], zb, preferred_element_type=F32)
                for k in range(pair):
                    zh_scr[jj * pair + k] = zh[:, k * ct:(k + 1) * ct]
                return 0

            lax.fori_loop(0, nb // pair, fwd, 0)

            def out_block(ii, _):
                for k in range(pair):
                    i = ii * pair + k
                    for rc in range(p // hrc):
                        re = pl.ds(rc * hrc, hrc)
                        im = pl.ds(p + rc * hrc, hrc)

                        def acc(j, carry):
                            ar, ai = carry
                            w = i - j + (nb - 1)
                            gr, gi = g_ref[w, re, :], g_ref[w, im, :]
                            zr, zi = zh_scr[j, re, :], zh_scr[j, im, :]
                            return ar + gr * zr - gi * zi, ai + gr * zi + gi * zr

                        zero = jnp.zeros((hrc, ct), F32)
                        ar, ai = lax.fori_loop(0, nb, acc, (zero, zero), unroll=True)
                        yh_scr[re, k * ct:(k + 1) * ct] = ar
                        yh_scr[im, k * ct:(k + 1) * ct] = ai
                c = jnp.dot(fi_scr[...], yh_scr[...].astype(BF16), preferred_element_type=F32)
                for k in range(pair):
                    c_scr[block(ii * pair + k), :] = c[:, k * ct:(k + 1) * ct]
                return 0

            lax.fori_loop(0, nb // pair, out_block, 0)

    hb = hb_ref[...]

    @pl.when(o == 0)
    def _():
        v = short_conv(v_ref, wv_ref, bv_ref)
        z_scr[...] = v
        long_conv()
        x1 = short_conv(x1_ref, w1_ref, b1_ref)
        z1_scr[s] = x1 * (c_scr[...] + hb[0:1] * v)

    @pl.when(o == 1)
    def _():
        z1 = z1_scr[s]
        z_scr[...] = z1
        long_conv()
        x2 = short_conv(x2_ref, w2_ref, b2_ref)
        o_ref[...] = (x2 * (c_scr[...] + hb[1:2] * z1)).astype(BF16)


def _hyena(proj, l, p, nseq, short_w, short_b, hy_bias, ghat, c, ct=128):
    r = proj.shape[0]
    rows = nseq * l
    nsb = r // rows
    nct = c // ct
    nwin = ghat.shape[1]
    fwd, _, inv = _shifted_dft_tables(p)
    pair = 2 if (ct < 256 and (l // p) % 2 == 0) else 1
    fz = jnp.asarray(fwd, F32)
    fi = jnp.asarray(inv, F32)
    sb = short_b.reshape(1, 3 * c)
    col = lambda k: (lambda j, o, s: (s, k * nct + j))
    wcol = lambda k: (lambda j, o, s: (0, k * nct + j))
    return pl.pallas_call(
        functools.partial(_hyena_body, nseq, l, p),
        grid=(nct, HYENA_ORDER, nsb),
        in_specs=[pl.BlockSpec((rows, ct), col(0)), pl.BlockSpec((rows, ct), col(1)), pl.BlockSpec((rows, ct), col(2)),
                  pl.BlockSpec((3, ct), wcol(0)), pl.BlockSpec((3, ct), wcol(1)), pl.BlockSpec((3, ct), wcol(2)),
                  pl.BlockSpec((1, ct), wcol(0)), pl.BlockSpec((1, ct), wcol(1)), pl.BlockSpec((1, ct), wcol(2)),
                  pl.BlockSpec((HYENA_ORDER, ct), lambda j, o, s: (0, j)),
                  pl.BlockSpec((None, nwin, 2 * p, ct), lambda j, o, s: (o, 0, 0, j)),
                  pl.BlockSpec((2 * p, p), lambda j, o, s: (0, 0)),
                  pl.BlockSpec((p, 2 * p), lambda j, o, s: (0, 0))],
        out_specs=pl.BlockSpec((rows, ct), lambda j, o, s: (s * o, j)),
        out_shape=jax.ShapeDtypeStruct((r, c), BF16),
        scratch_shapes=[pltpu.VMEM((rows, ct), F32), pltpu.VMEM((rows, ct), F32),
                        pltpu.VMEM((l // p, 2 * p, ct), F32), pltpu.VMEM((2 * p, pair * ct), F32),
                        pltpu.VMEM((nsb, rows, ct), F32), pltpu.VMEM((2 * p, p), BF16), pltpu.VMEM((p, 2 * p), BF16)],
        compiler_params=_cparams(("parallel", "arbitrary", "arbitrary")),
        name=f"hyena_{l}",
    )(proj, proj, proj, short_w, short_w, short_w, sb, sb, sb, hy_bias, ghat, fz, fi)


def _s5_param_body(lre_ref, lim_ref, ldt_ref, btr_ref, bti_ref, cr_ref, ci_ref,
                   k_ref, wsr_ref, wsi_ref, wor_ref, woi_ref, etr_ref, eti_ref):
    t = S5_CHUNK
    backward = pl.program_id(0) == 1
    lr = jnp.minimum(lre_ref[...], LAMBDA_RE_MAX)
    li = lim_ref[...]
    dt = jnp.exp(ldt_ref[...])
    ar, ai = lr * dt, li * dt

    def power(e):
        mag = jnp.exp(e * ar)
        return mag * jnp.cos(e * ai), mag * jnp.sin(e * ai)

    per_step = lambda a: a[:, None, :]

    one = jnp.ones((1, 1), F32)
    lbr, lbi = power(one)
    nr, ni = lbr - 1.0, lbi
    den = lr * lr + li * li
    qr, qi = (nr * lr + ni * li) / den, (ni * lr - nr * li) / den
    btr, bti = btr_ref[...], bti_ref[...]
    bbr, bbi = qr * btr - qi * bti, qr * bti + qi * btr
    cr, ci = cr_ref[...], ci_ref[...]

    step = lax.broadcasted_iota(jnp.int32, (t, 1), 0).astype(F32)
    er, ei = map(per_step, power(step))
    cer, cei = cr[None] * er - ci[None] * ei, cr[None] * ei + ci[None] * er
    nt = (((1,), (1,)), ((), ()))
    dg = functools.partial(lax.dot_general, dimension_numbers=nt, precision=HIGHEST, preferred_element_type=F32)
    k_ref[...] = (dg(cer.reshape(t * SSM_GROUP, SSM_STATE), bbr)
                  - dg(cei.reshape(t * SSM_GROUP, SSM_STATE), bbi))
    er, ei = map(per_step, power(jnp.where(backward, step, (t - 1.0) - step)))
    wsr_ref[...] = er * bbr[None] - ei * bbi[None]
    wsi_ref[...] = er * bbi[None] + ei * bbr[None]
    er, ei = map(per_step, power(jnp.where(backward, t - step, step + 1.0)))
    wor_ref[...] = cr[None] * er - ci[None] * ei
    woi_ref[...] = -(cr[None] * ei + ci[None] * er)
    etr, eti = power(one * t)
    etr_ref[...] = etr
    eti_ref[...] = eti


def _s5_param_tile_body(*refs):
    for gi in range(refs[0].shape[0]):
        _s5_param_body(*(r.at[gi] for r in refs))


def _s5_params(lam_re, lam_im, log_dt, b_re, b_im, c_re, c_im):
    g = lam_re.shape[1]
    gb = S5_TILE_GROUPS
    t, cg, ps = S5_CHUNK, SSM_GROUP, SSM_STATE
    vec = lambda a: a.reshape(2, g, 1, ps)
    ldt = jnp.broadcast_to(log_dt[:, :, None, None], (2, g, 1, ps))
    bt = lambda a: jnp.swapaxes(a, -1, -2)
    dg_spec = lambda *shape: pl.BlockSpec((None, gb) + shape, lambda d, i: (d, i) + (0,) * len(shape))
    outs = [jax.ShapeDtypeStruct((2, g, t * cg, cg), F32)] + \
           [jax.ShapeDtypeStruct((2, g, t, cg, ps), F32)] * 4 + [jax.ShapeDtypeStruct((2, g, 1, ps), F32)] * 2
    return pl.pallas_call(
        _s5_param_tile_body,
        grid=(2, g // gb),
        in_specs=[dg_spec(1, ps), dg_spec(1, ps), dg_spec(1, ps), dg_spec(cg, ps), dg_spec(cg, ps),
                  dg_spec(cg, ps), dg_spec(cg, ps)],
        out_specs=[dg_spec(t * cg, cg)] + [dg_spec(t, cg, ps)] * 4 + [dg_spec(1, ps)] * 2,
        out_shape=outs,
        compiler_params=_cparams(("arbitrary", "arbitrary")),
        name="s5_params",
    )(vec(lam_re), vec(lam_im), ldt, bt(b_re), bt(b_im), c_re, c_im)


def _s5_operators(lam_re, lam_im, log_dt, b_re, b_im, c_re, c_im):
    kk, wsr, wsi, wor, woi, etr, eti = _s5_params(lam_re, lam_im, log_dt, b_re, b_im, c_re, c_im)
    kk, wsr, wsi, wor, woi = (a.astype(BF16) for a in (kk, wsr, wsi, wor, woi))
    g = kk.shape[1]
    t, cg, ps = S5_CHUNK, SSM_GROUP, SSM_STATE
    kd = kk.reshape(2, g, t, cg, cg)
    s_idx = np.arange(t)[:, None]
    t_idx = np.arange(t)[None, :]

    def toeplitz(kdir, lag):
        m = jnp.where((lag >= 0)[None, :, :, None, None], kdir[:, np.mod(lag, t)], 0)
        return jnp.transpose(m, (0, 1, 4, 2, 3)).reshape(g, t * cg, t * cg)

    mf = toeplitz(kd[0], t_idx - s_idx)
    mb = toeplitz(kd[1], s_idx - t_idx)
    flat = lambda a: a.reshape(g, t * cg, ps)
    w1 = jnp.concatenate([mf, mb, flat(wsr[0]), flat(wsr[1]), flat(wsi[0]), flat(wsi[1])], axis=-1)
    tr = lambda a: jnp.swapaxes(flat(a), 1, 2)
    w2 = jnp.concatenate([tr(wor[0]), tr(wor[1]), tr(woi[0]), tr(woi[1])], axis=1)
    lam = jnp.concatenate([etr[0], etr[1], eti[0], eti[1]], axis=-1)
    lam = lam.reshape(g // S5_TILE_GROUPS, 1, S5_TILE_GROUPS * 4 * ps)
    return w1, w2, lam


def _chunk_lane_permutation():
    t, gl, cg = S5_CHUNK, S5_TILE_GROUPS, SSM_GROUP
    src = np.arange(t * gl * cg).reshape(t, gl, cg)
    dst = np.transpose(src, (1, 0, 2)).reshape(-1)
    perm = np.zeros((t * gl * cg, t * gl * cg), np.float32)
    perm[dst, np.arange(t * gl * cg)] = 1.0
    return perm


def _s5_body(bsz, nk, chunk_major, u_ref, perm_ref, permt_ref, w1_ref, w2_ref, lam_ref, s0_ref, y_ref, fin_ref,
             tok_scr, u_scr, ug_scr, yl_scr, x_scr, p_scr):
    t_, lt, gl = S5_CHUNK, 128, S5_TILE_GROUPS
    half = SSM_STATE
    sw = 4 * half
    l = nk * t_
    tw = t_ * SSM_GROUP

    def token_rows(t):
        if chunk_major:
            return [(k * bsz, bsz, k * t_ + t, l) for k in range(nk)]
        return [(b * nk, nk, b * l + t, t_) for b in range(bsz)]

    tok_scr[...] = u_ref[...].astype(F32)
    for t in range(t_):
        for r0, n, tok0, stride in token_rows(t):
            u_scr[r0:r0 + n, t * lt:(t + 1) * lt] = tok_scr[pl.ds(tok0, n, stride=stride), :].astype(BF16)
    ug_scr[...] = jnp.dot(u_scr[...], perm_ref[...], preferred_element_type=F32).astype(BF16)
    for gi in range(gl):
        z = jnp.dot(ug_scr[:, gi * tw:(gi + 1) * tw], w1_ref[gi], preferred_element_type=F32)
        yl_scr[:, gi * tw:(gi + 1) * tw] = z[:, :tw] + z[:, tw:2 * tw]
        x_scr[:, gi * sw:(gi + 1) * sw] = z[:, 2 * tw:]
    p_scr[...] = jnp.zeros(p_scr.shape, F32)

    srows = bsz if chunk_major else 8
    nset = 1 if chunk_major else bsz
    s0_of = (lambda q: slice(0, bsz)) if chunk_major else (lambda q: slice(q, q + 1))
    fwd_lane = lax.broadcasted_iota(jnp.int32, (srows, 2 * half), 1) < half
    row_id = lax.broadcasted_iota(jnp.int32, (srows, 2 * half), 0)
    lam = lam_ref[...]
    cols_a = lambda gi: pl.ds(gi * sw, 2 * half)
    cols_b = lambda gi: pl.ds(gi * sw + 2 * half, 2 * half)
    lam_of = lambda gi: (lam[:, gi * sw:gi * sw + 2 * half], lam[:, gi * sw + 2 * half:(gi + 1) * sw])

    def step(t, carry):
        kf = pl.ds(pl.multiple_of(t * bsz, bsz), bsz)
        kb = pl.ds(pl.multiple_of((nk - 1 - t) * bsz, bsz), bsz)
        new = []
        for gi in range(gl):
            ca, cb = cols_a(gi), cols_b(gi)
            sr, si = carry[gi]
            p_scr[kf, ca] = jnp.where(fwd_lane, sr, p_scr[kf, ca])
            p_scr[kb, ca] = jnp.where(fwd_lane, p_scr[kb, ca], sr)
            p_scr[kf, cb] = jnp.where(fwd_lane, si, p_scr[kf, cb])
            p_scr[kb, cb] = jnp.where(fwd_lane, p_scr[kb, cb], si)
            xr = jnp.where(fwd_lane, x_scr[kf, ca], x_scr[kb, ca])
            xi = jnp.where(fwd_lane, x_scr[kf, cb], x_scr[kb, cb])
            lr, li = lam_of(gi)
            new.append((lr * sr - li * si + xr, lr * si + li * sr + xi))
        return tuple(new)

    def step8(t8, carry):
        nb8 = nk // 8
        new = []
        for q in range(bsz):
            kf = pl.ds(pl.multiple_of(q * nk + t8 * 8, 8), 8)
            kb = pl.ds(pl.multiple_of(q * nk + (nb8 - 1 - t8) * 8, 8), 8)
            for gi in range(gl):
                ca, cb = cols_a(gi), cols_b(gi)
                sr, si = carry[q * gl + gi]
                xrf, xrb, xif, xib = x_scr[kf, ca], x_scr[kb, ca], x_scr[kf, cb], x_scr[kb, cb]
                prf, prb, pif, pib = p_scr[kf, ca], p_scr[kb, ca], p_scr[kf, cb], p_scr[kb, cb]
                lr, li = lam_of(gi)
                for j in range(8):
                    at_f = (row_id == j) & fwd_lane
                    at_b = (row_id == 7 - j) & jnp.logical_not(fwd_lane)
                    prf, prb = jnp.where(at_f, sr, prf), jnp.where(at_b, sr, prb)
                    pif, pib = jnp.where(at_f, si, pif), jnp.where(at_b, si, pib)
                    row = lambda a, i: jnp.broadcast_to(a[i:i + 1], a.shape)
                    xr = jnp.where(fwd_lane, row(xrf, j), row(xrb, 7 - j))
                    xi = jnp.where(fwd_lane, row(xif, j), row(xib, 7 - j))
                    sr, si = lr * sr - li * si + xr, lr * si + li * sr + xi
                p_scr[kf, ca], p_scr[kb, ca], p_scr[kf, cb], p_scr[kb, cb] = prf, prb, pif, pib
                new.append((sr, si))
        return tuple(new)

    bcast = lambda a: jnp.broadcast_to(a, (srows, 2 * half))
    init = tuple((bcast(s0_ref[s0_of(q), gi * sw:gi * sw + 2 * half]),
                  bcast(s0_ref[s0_of(q), gi * sw + 2 * half:(gi + 1) * sw]))
                 for q in range(nset) for gi in range(gl))
    if chunk_major:
        fin = lax.fori_loop(0, nk, step, init)
    else:
        fin = lax.fori_loop(0, nk // 8, step8, init)
    nfin = bsz if chunk_major else 1
    for q in range(nset):
        for gi in range(gl):
            sr, si = fin[q * gl + gi]
            fin_ref[s0_of(q), gi * sw:gi * sw + 2 * half] = sr[0:nfin]
            fin_ref[s0_of(q), gi * sw + 2 * half:(gi + 1) * sw] = si[0:nfin]

    for gi in range(gl):
        y = yl_scr[:, gi * tw:(gi + 1) * tw] + jnp.dot(p_scr[:, gi * sw:(gi + 1) * sw].astype(BF16), w2_ref[gi],
                                                       preferred_element_type=F32)
        u_scr[:, gi * tw:(gi + 1) * tw] = y.astype(BF16)
    yt = jnp.dot(u_scr[...], permt_ref[...], preferred_element_type=F32)
    for t in range(t_):
        for r0, n, tok0, stride in token_rows(t):
            tok_scr[pl.ds(tok0, n, stride=stride), :] = yt[r0:r0 + n, t * lt:(t + 1) * lt]
    y_ref[...] = tok_scr[...].astype(BF16)


def _s5(proj, col0, bsz, l, ops, s0):
    w1, w2, lam = ops
    g = w1.shape[0]
    t, ps, gl, lt = S5_CHUNK, SSM_STATE, S5_TILE_GROUPS, 128
    nb = g // gl
    nk = l // t
    r = bsz * l
    nrows = bsz * nk
    sw = 4 * ps
    cw = t * lt
    tw = t * SSM_GROUP
    chunk_major = bsz % 8 == 0
    assert chunk_major or nk % 8 == 0
    if s0 is None:
        s0p = jnp.zeros((bsz, g * sw), F32)
    else:
        s0_re, s0_im = s0
        s0p = jnp.concatenate([s0_re[:, 0], s0_re[:, 1], s0_im[:, 0], s0_im[:, 1]], axis=-1).reshape(bsz, g * sw)
    perm = _chunk_lane_permutation()
    const = lambda shape: _resident(shape, lambda i: (0,) * len(shape))
    y, fin = pl.pallas_call(
        functools.partial(_s5_body, bsz, nk, chunk_major),
        grid=(nb,),
        in_specs=[pl.BlockSpec((r, lt), lambda i: (0, col0 // lt + i), pipeline_mode=pl.Buffered(1)),
                  const((cw, cw)), const((cw, cw)),
                  pl.BlockSpec((gl, tw, 3 * tw), lambda i: (i, 0, 0)), pl.BlockSpec((gl, tw, tw), lambda i: (i, 0, 0)),
                  pl.BlockSpec((None, 1, gl * sw), lambda i: (i, 0, 0)),
                  pl.BlockSpec((bsz, gl * sw), lambda i: (0, i))],
        out_specs=[pl.BlockSpec((r, lt), lambda i: (0, i)), pl.BlockSpec((bsz, gl * sw), lambda i: (0, i))],
        out_shape=[jax.ShapeDtypeStruct((r, g * SSM_GROUP), BF16), jax.ShapeDtypeStruct((bsz, g * sw), F32)],
        scratch_shapes=[pltpu.VMEM((r, lt), F32), pltpu.VMEM((nrows, cw), BF16), pltpu.VMEM((nrows, cw), BF16),
                        pltpu.VMEM((nrows, cw), F32), pltpu.VMEM((nrows, gl * sw), F32),
                        pltpu.VMEM((nrows, gl * sw), F32)],
        compiler_params=_cparams(("parallel",)),
        name=f"s5_scan_{nk}",
    )(proj, jnp.asarray(perm, BF16), jnp.asarray(perm.T, BF16), w1, w2, lam, s0p)
    fin = jnp.transpose(fin.reshape(bsz, g, 2, 2, ps), (2, 0, 3, 1, 4))
    return y, fin[0], fin[1]


def _merge_body(yh_ref, ys_ref, u_ref, gh_ref, gs_ref, x_ref, d_ref, gw_ref, gbias_ref, wbh_ref, wbs_ref, wo_ref,
                g1_ref, n2_ref, sc2_ref, sh2_ref, x1_ref, h2_ref):
    y = ys_ref[...].astype(F32) + u_ref[...].astype(F32) * d_ref[...]
    g = jax.nn.gelu(y)
    gl = jnp.dot(g.astype(BF16), gw_ref[...], preferred_element_type=F32) + gbias_ref[...]
    ys = (g * jax.nn.sigmoid(gl)).astype(BF16)
    bh = jnp.dot(yh_ref[...], wbh_ref[...], preferred_element_type=F32)
    bs = jnp.dot(ys, wbs_ref[...], preferred_element_type=F32)
    merged = jax.nn.sigmoid(gh_ref[...].astype(F32)) * bh + jax.nn.sigmoid(gs_ref[...].astype(F32)) * bs
    x1 = x_ref[...] + g1_ref[...] * jnp.dot(merged.astype(BF16), wo_ref[...], preferred_element_type=F32)
    x1_ref[...] = x1
    h2_ref[...] = _mod_rmsnorm(x1, n2_ref[...], sc2_ref[...], sh2_ref[...]).astype(BF16)


def _merge(yh, ys, proj, x, mod4, cond_of_tile, ssm_d, glu_w, glu_b, wbh, wbs, wo, n2g, c_h, c_s, tm=256):
    r, d = x.shape
    row_blk = lambda width, k: pl.BlockSpec((tm, width), lambda i: (i, k))
    mod_spec = lambda chunk: pl.BlockSpec((None, None, 1, d), lambda i: (cond_of_tile(i, tm), chunk, 0, 0))
    const = lambda shape: _resident(shape, lambda i: (0,) * len(shape))
    u_col = (3 * c_h) // c_s
    gate0 = (3 * c_h + c_s) // d
    return pl.pallas_call(
        _merge_body,
        grid=(r // tm,),
        in_specs=[row_blk(c_h, 0), row_blk(c_s, 0), row_blk(c_s, u_col), row_blk(d, gate0), row_blk(d, gate0 + 1),
                  row_blk(d, 0), const((1, c_s)), const((c_s, c_s)), const((1, c_s)), const((c_h, d)),
                  const((c_s, d)), const((d, d)), mod_spec(2), const((1, d)), mod_spec(4), mod_spec(3)],
        out_specs=[row_blk(d, 0), row_blk(d, 0)],
        out_shape=[jax.ShapeDtypeStruct((r, d), F32), jax.ShapeDtypeStruct((r, d), BF16)],
        compiler_params=_cparams(("parallel",)),
        name="merge",
    )(yh, ys, proj, proj, proj, x, ssm_d.reshape(1, c_s), glu_w, glu_b.reshape(1, c_s), wbh, wbs, wo,
      mod4, n2g.reshape(1, d), mod4, mod4)


_FFN_RC = 512


def _ffn_up_body(rows, cols, h_ref, wa_ref, wb_ref, cwa_ref, cwb_ref, cba_ref, cbb_ref, o_ref):
    tm, tn = o_ref.shape
    rc = _FFN_RC
    n = tm // rc
    col = lax.rem(lax.broadcasted_iota(jnp.int32, (rc, tn), 0), cols)
    not_first = col != 0
    not_last = col != cols - 1
    tdt = BF16 if rows > 1 else F32
    zero = jnp.zeros((cols, tn), tdt)

    def taps(hc, w_ref, cw_ref, cb_ref):
        a = jnp.dot(hc, w_ref[...], preferred_element_type=F32)
        am = jnp.where(not_first, pltpu.roll(a, 1, 0), 0.0).astype(tdt)
        ap = jnp.where(not_last, pltpu.roll(a, rc - 1, 0), 0.0).astype(tdt)
        a = a.astype(tdt)
        cw = cw_ref[...].astype(tdt)
        tap = lambda i: am * cw[3 * i:3 * i + 1] + a * cw[3 * i + 1:3 * i + 2] + ap * cw[3 * i + 2:3 * i + 3]
        mid = tap(1) + cb_ref[...].astype(tdt)
        return (tap(0), mid, tap(2)) if rows > 1 else (None, mid, None)

    def finish(prev, cur, nxt):
        t0, mid, t2 = cur
        if rows == 1:
            return mid
        above = jnp.concatenate([prev[0][rc - cols:] if prev is not None else zero, t0[:rc - cols]], axis=0)
        below = jnp.concatenate([t2[cols:], nxt[2][:cols] if nxt is not None else zero], axis=0)
        return mid + above + below

    chunks = []
    for i in range(n + 1):
        if i < n:
            hc = h_ref[i * rc:(i + 1) * rc, :]
            chunks.append((taps(hc, wa_ref, cwa_ref, cba_ref), taps(hc, wb_ref, cwb_ref, cbb_ref)))
        if i >= 1:
            j = i - 1
            pick = lambda k, side: chunks[k][side] if 0 <= k < n else None
            a = finish(pick(j - 1, 0), pick(j, 0), pick(j + 1, 0))
            b = finish(pick(j - 1, 1), pick(j, 1), pick(j + 1, 1))
            o_ref[j * rc:(j + 1) * rc, :] = (jax.nn.gelu(a.astype(F32)) * b.astype(F32)).astype(BF16)


def _ffn_up(h2, w_up, conv_w, conv_b, rows, cols, tn=256):
    r, d = h2.shape
    f = w_up.shape[1] // 2
    tm = rows * cols if rows > 1 else 4096
    assert tm % _FFN_RC == 0 and _FFN_RC % cols == 0 and cols % 8 == 0
    nt = f // tn
    cw = conv_w.reshape(9, 2 * f)
    cb = conv_b.reshape(1, 2 * f)
    return pl.pallas_call(
        functools.partial(_ffn_up_body, rows, cols),
        grid=(r // tm, nt),
        in_specs=[pl.BlockSpec((tm, d), lambda i, j: (i, 0), pipeline_mode=pl.Buffered(1)),
                  pl.BlockSpec((d, tn), lambda i, j: (0, j)), pl.BlockSpec((d, tn), lambda i, j: (0, nt + j)),
                  pl.BlockSpec((9, tn), lambda i, j: (0, j)), pl.BlockSpec((9, tn), lambda i, j: (0, nt + j)),
                  pl.BlockSpec((1, tn), lambda i, j: (0, j)), pl.BlockSpec((1, tn), lambda i, j: (0, nt + j))],
        out_specs=pl.BlockSpec((tm, tn), lambda i, j: (i, j)),
        out_shape=jax.ShapeDtypeStruct((r, f), BF16),
        compiler_params=_cparams(("parallel", "arbitrary")),
        name=f"ffn_up_{rows}x{cols}",
    )(h2, w_up, w_up, cw, cw, cb, cb)


def _ffn_down_body(a_ref, w_ref, x_ref, g2_ref, fg_ref, o_ref):
    x2 = x_ref[...] + g2_ref[...] * jnp.dot(a_ref[...], w_ref[...], preferred_element_type=F32)
    ms = jnp.mean(x2 * x2, axis=-1, keepdims=True)
    o_ref[...] = x2 * lax.rsqrt(ms + EPS) * fg_ref[...]


def _ffn_down(act, w, x1, mod4, cond_of_tile, final_g, tm=256):
    r, f = act.shape
    d = w.shape[1]
    return pl.pallas_call(
        _ffn_down_body,
        grid=(r // tm,),
        in_specs=[pl.BlockSpec((tm, f), lambda i: (i, 0)),
                  _resident((f, d), lambda i: (0, 0)),
                  pl.BlockSpec((tm, d), lambda i: (i, 0)),
                  pl.BlockSpec((None, None, 1, d), lambda i: (cond_of_tile(i, tm), 5, 0, 0)),
                  _resident((1, d), lambda i: (0, 0))],
        out_specs=pl.BlockSpec((tm, d), lambda i: (i, 0)),
        out_shape=jax.ShapeDtypeStruct((r, d), F32),
        compiler_params=_cparams(("parallel",)),
        name="ffn_down",
    )(act, w, x1, mod4, final_g.reshape(1, d))


def _segment(x, cond_base, per_batch_cond, s0, rows, cols, hy_block, hy_nseq, mod4, p, final_g):
    bsz, l, d = x.shape
    xf = x.reshape(bsz * l, d)
    c_h = p["wbh"].shape[0]
    c_s = p["wbs"].shape[0]

    def cond_of_tile(i, tm):
        return cond_base + (i * tm) // l if per_batch_cond else cond_base

    proj = _in_proj(xf, mod4, cond_of_tile, p["norm1_g"], p["w_in"], p["b_in"])
    yh = _hyena(proj, l, hy_block, hy_nseq, p["hy_short_w"], p["hy_short_b"], p["hy_bias"], p["ghat"][l], c_h,
                ct=128 if l // hy_block > 1 else 256)
    ys, fin_re, fin_im = _s5(proj, 3 * c_h, bsz, l, p["s5_ops"], s0)
    x1, h2 = _merge(yh, ys, proj, xf, mod4, cond_of_tile, p["ssm_d"], p["glu_w"], p["glu_b"],
                    p["wbh"], p["wbs"], p["wo"], p["norm2_g"], c_h, c_s)
    act = _ffn_up(h2, p["ffn_up"], p["ffn_conv_w"], p["ffn_conv_b"], rows, cols)
    y = _ffn_down(act, p["ffn_down"], x1, mod4, cond_of_tile, final_g)
    return y.reshape(bsz, l, d), fin_re, fin_im


def kernel(x_prompt, x_sample, state_ssm_re, state_ssm_im, c, c_ctx, ada_w, ada_b, norm1_g, norm2_g, final_g, w_in, b_in, hy_short_w, hy_short_b, hy_f_w1, hy_f_b1, hy_f_w2, hy_f_b2, hy_f_w3, hy_f_b3, hy_f_w4, hy_f_freq, hy_bias, w_branch_h, ssm_lambda_re, ssm_lambda_im, ssm_log_dt, ssm_b_re, ssm_b_im, ssm_c_re, ssm_c_im, ssm_d, ssm_glu_w, ssm_glu_b, w_branch_s, w_out, ffn_up, ffn_conv_w, ffn_conv_b, ffn_down):
    depth = ada_w.shape[0]
    assert depth == 1, "the final norm is fused into the (single) layer's last kernel"
    d = x_prompt.shape[-1]
    ctx_len = x_prompt.shape[1]
    lat_len = x_sample.shape[1]
    dec_b = x_sample.shape[0]
    c_h = w_branch_h.shape[1]
    assert 1 + dec_b <= 8
    cond = jnp.zeros((8, d), F32).at[0].set(c_ctx).at[1:1 + dec_b].set(c)
    lay = 0
    mod4 = _ada(cond, ada_w[lay], ada_b[lay]).reshape(8, N_MOD, 1, d)
    hy_blocks = {ctx_len: 256, lat_len: 512}
    ghat = {l: _hyena_filters(l, hy_blocks[l], hy_f_w1[lay], hy_f_b1[lay], hy_f_w2[lay], hy_f_b2[lay],
                              hy_f_w3[lay], hy_f_b3[lay], hy_f_w4[lay], hy_f_freq[lay], c_h)
            for l in (ctx_len, lat_len)}
    s5_ops = _s5_operators(ssm_lambda_re[lay], ssm_lambda_im[lay], ssm_log_dt[lay],
                           ssm_b_re[lay], ssm_b_im[lay], ssm_c_re[lay], ssm_c_im[lay])
    p = dict(norm1_g=norm1_g[lay], norm2_g=norm2_g[lay], w_in=w_in[lay].astype(BF16), b_in=b_in[lay],
             hy_short_w=hy_short_w[lay], hy_short_b=hy_short_b[lay], hy_bias=hy_bias[lay], ghat=ghat,
             s5_ops=s5_ops, ssm_d=ssm_d[lay], glu_w=ssm_glu_w[lay].astype(BF16),
             glu_b=ssm_glu_b[lay], wbh=w_branch_h[lay].astype(BF16), wbs=w_branch_s[lay].astype(BF16),
             wo=w_out[lay].astype(BF16), ffn_up=ffn_up[lay].astype(BF16), ffn_conv_w=ffn_conv_w[lay],
             ffn_conv_b=ffn_conv_b[lay], ffn_down=ffn_down[lay].astype(BF16))
    y_prompt, st_re, st_im = _segment(x_prompt, 0, False, None, 1, ctx_len, hy_blocks[ctx_len], 8, mod4, p, final_g)
    s0 = (state_ssm_re[:, lay].astype(F32), state_ssm_im[:, lay].astype(F32))
    y_sample, _, _ = _segment(x_sample, 1, True, s0, lat_len // GRID_W, GRID_W, hy_blocks[lat_len], 1, mod4, p, final_g)
    return y_prompt, y_sample, st_re[:, None], st_im[:, None]
```

```python
import functools
import math

import jax
import jax.numpy as jnp
import numpy as np
from jax import lax
from jax.experimental import pallas as pl
from jax.experimental.pallas import tpu as pltpu

F32 = jnp.float32
BF16 = jnp.bfloat16
HIGHEST = lax.Precision.HIGHEST

GRID_W = 64
EPS = 1e-6
SSM_GROUP = 16
SSM_STATE = 64
FILTER_EMB = 33
FILTER_HIDDEN = 64
HYENA_ORDER = 2
DECAY_FAST = 0.3
DECAY_SLOW = 1.5
DECAY_TARGET = 1e-2
LAMBDA_RE_MAX = -1e-4

S5_CHUNK = 16
S5_TILE_GROUPS = 8
V7X_VMEM_LIMIT = 56 * 2**20
N_MOD = 6


def _cparams(sem):
    return pltpu.CompilerParams(dimension_semantics=sem, vmem_limit_bytes=V7X_VMEM_LIMIT)


def _resident(shape, index_map):
    return pl.BlockSpec(shape, index_map, pipeline_mode=pl.Buffered(1))


def _mod_rmsnorm(x, g, sc, sh):
    ms = jnp.mean(x * x, axis=-1, keepdims=True)
    return x * lax.rsqrt(ms + EPS) * g * (1.0 + sc) + sh


def _ada_body(c_ref, w_ref, b_ref, o_ref):
    c = c_ref[...]
    s = c * jax.nn.sigmoid(c)
    o_ref[...] = jnp.dot(s, w_ref[...], precision=HIGHEST, preferred_element_type=F32) + b_ref[...]


def _ada(cond, w, b):
    d, n = w.shape
    tn = 1024
    return pl.pallas_call(
        _ada_body,
        grid=(n // tn,),
        in_specs=[pl.BlockSpec((8, d), lambda j: (0, 0)),
                  pl.BlockSpec((d, tn), lambda j: (0, j)),
                  pl.BlockSpec((1, tn), lambda j: (0, j))],
        out_specs=pl.BlockSpec((8, tn), lambda j: (0, j)),
        out_shape=jax.ShapeDtypeStruct((8, n), F32),
        compiler_params=_cparams(("parallel",)),
        name="ada",
    )(cond, w, b.reshape(1, n))


def _inproj_body(x_ref, g_ref, sc_ref, sh_ref, w_ref, b_ref, o_ref, h_scr):
    @pl.when(pl.program_id(1) == 0)
    def _():
        h_scr[...] = _mod_rmsnorm(x_ref[...], g_ref[...], sc_ref[...], sh_ref[...]).astype(BF16)

    o_ref[...] = (jnp.dot(h_scr[...], w_ref[...], preferred_element_type=F32) + b_ref[...]).astype(BF16)


def _in_proj(x, mod4, cond_of_tile, g, w, b, tm=1024, tn=1024):
    r, d = x.shape
    n = w.shape[1]
    mod_spec = lambda chunk: pl.BlockSpec((None, None, 1, d), lambda i, j: (cond_of_tile(i, tm), chunk, 0, 0))
    return pl.pallas_call(
        _inproj_body,
        grid=(r // tm, n // tn),
        in_specs=[pl.BlockSpec((tm, d), lambda i, j: (i, 0)),
                  pl.BlockSpec((1, d), lambda i, j: (0, 0)),
                  mod_spec(1), mod_spec(0),
                  pl.BlockSpec((d, tn), lambda i, j: (0, j)),
                  pl.BlockSpec((1, tn), lambda i, j: (0, j))],
        out_specs=pl.BlockSpec((tm, tn), lambda i, j: (i, j)),
        out_shape=jax.ShapeDtypeStruct((r, n), BF16),
        scratch_shapes=[pltpu.VMEM((tm, d), BF16)],
        compiler_params=_cparams(("parallel", "arbitrary")),
        name="in_proj",
    )(x, g.reshape(1, d), mod4, mod4, w, b.reshape(1, n))


def _shifted_dft_tables(p):
    theta = np.pi * (2.0 * np.arange(p) + 1.0) / (2.0 * p)
    n = np.arange(p)
    a = theta[:, None] * n[None, :]
    fwd = np.concatenate([np.cos(a), -np.sin(a)], axis=0)
    m = np.arange(2 * p) - p
    am = theta[:, None] * m[None, :]
    filt = np.concatenate([np.cos(am), -np.sin(am)], axis=0)
    filt[:, 0] = 0.0
    inv = np.concatenate([np.cos(a).T, -np.sin(a).T], axis=1) / p
    return fwd, filt, inv


_FILTER_RC = 1024


def _filter_positions(l):
    pos = np.abs(np.arange(2 * l) - l).astype(np.float64)
    t = pos / (l - 1)
    bands = (FILTER_EMB - 1) // 2
    fr = np.linspace(1e-4, bands - 1, bands)
    ang = (2.0 * math.pi / l) * pos[:, None] * fr[None, :]
    z = np.concatenate([t[:, None], np.cos(ang), -np.sin(ang)], axis=-1)
    zp = np.zeros((2 * l, 128), np.float32)
    zp[:, :FILTER_EMB] = z
    return zp


def _filter_hidden_body(z_ref, w1_ref, b1_ref, w2_ref, b2_ref, w3_ref, b3_ref, fq_ref, o_ref):
    fq = fq_ref[...]
    dot = functools.partial(jnp.dot, precision=HIGHEST, preferred_element_type=F32)
    h = jnp.sin(fq * (dot(z_ref[...], w1_ref[...]) + b1_ref[...]))
    h = jnp.sin(fq * (dot(h, w2_ref[...]) + b2_ref[...]))
    o_ref[...] = jnp.sin(fq * (dot(h, w3_ref[...]) + b3_ref[...]))


def _filter_hidden(zp, w1p, b1, w2, b2, w3, b3, freq):
    rows, kin = zp.shape
    rc = min(rows, _FILTER_RC)
    hd = w2.shape[0]
    full = lambda shape: pl.BlockSpec(shape, lambda i: (0,) * len(shape))
    row = lambda a: a.reshape(1, -1)
    return pl.pallas_call(
        _filter_hidden_body,
        grid=(rows // rc,),
        in_specs=[pl.BlockSpec((rc, kin), lambda i: (i, 0)), full((kin, hd)), full((1, hd)), full((hd, hd)),
                  full((1, hd)), full((hd, hd)), full((1, hd)), full((1, hd))],
        out_specs=pl.BlockSpec((rc, hd), lambda i: (i, 0)),
        out_shape=jax.ShapeDtypeStruct((rows, hd), F32),
        compiler_params=_cparams(("parallel",)),
        name=f"hyena_filter_mlp_{rows}",
    )(zp, w1p, row(b1), w2, row(b2), w3, row(b3), row(freq))


def _split_bf16(x):
    hi = x.astype(BF16)
    return hi, (x - hi.astype(F32)).astype(BF16)


def _dot_3pass(x, w_hi, w_lo):
    x_hi, x_lo = _split_bf16(x)
    d = functools.partial(jnp.dot, preferred_element_type=F32)
    return d(x_hi, w_hi) + (d(x_lo, w_hi) + d(x_hi, w_lo))


def _filter_body(l, p, h_ref, w4f_ref, w4b_ref, dl_ref, mg_ref, o_ref, kk_scr, mg_scr):
    mg_scr[...] = mg_ref[...].astype(BF16)
    rc = min(l, _FILTER_RC)
    ct = kk_scr.shape[1]
    nhalf = l // rc

    def taps(w_ref, first):
        w_hi, w_lo = _split_bf16(w_ref[...])

        def body(i, asum):
            r0 = pl.multiple_of(i * rc, rc)
            q = r0 + lax.broadcasted_iota(jnp.int32, (rc, ct), 0)
            t = jnp.abs(q - l).astype(F32) * (1.0 / (l - 1))
            k = _dot_3pass(h_ref[pl.ds(r0, rc), :], w_hi, w_lo) * jnp.exp(-t * dl_ref[...])
            k = jnp.where(q == 0, 0.0, k)
            kk_scr[pl.ds(r0, rc), :] = k.astype(BF16)
            return asum + jnp.sum(jnp.abs(k), axis=0, keepdims=True)

        return lambda asum: lax.fori_loop(first, first + nhalf, body, asum)

    asum = taps(w4f_ref, nhalf)(taps(w4b_ref, 0)(jnp.zeros((1, ct), F32)))
    scale = 1.0 / asum
    nwin = 2 * (l // p) - 1

    def win(w, _):
        seg = kk_scr[pl.ds(pl.multiple_of(w * p, p), 2 * p), :]
        o_ref[w] = jnp.dot(mg_scr[...], seg, preferred_element_type=F32) * scale
        return 0

    lax.fori_loop(0, nwin, win, 0)


def _hyena_filters(l, p, w1, b1, w2, b2, w3, b3, w4, freq, c, ct=256):
    nwin = 2 * (l // p) - 1
    nct = c // ct
    zp = _filter_positions(l)
    zp2 = jnp.asarray(np.concatenate([zp[:l], zp[l:]], axis=1))
    w1p = jnp.zeros((128, FILTER_HIDDEN), F32).at[:FILTER_EMB].set(w1)
    twice = lambda w: jnp.zeros((2 * w.shape[0], 2 * w.shape[1]), F32).at[:w.shape[0], :w.shape[1]].set(w) \
        .at[w.shape[0]:, w.shape[1]:].set(w)
    both = lambda v: jnp.concatenate([v, v])
    hid2 = _filter_hidden(zp2, twice(w1p), both(b1), twice(w2), both(b2), twice(w3), both(b3), both(freq))
    hid = jnp.concatenate([hid2[:, :FILTER_HIDDEN], hid2[:, FILTER_HIDDEN:]], axis=0)
    deltas = np.abs(np.linspace(math.log(DECAY_TARGET) / DECAY_SLOW, math.log(DECAY_TARGET) / DECAY_FAST, c))
    mg = jnp.asarray(_shifted_dft_tables(p)[1], F32)
    full = lambda shape: pl.BlockSpec(shape, lambda o, j: (0,) * len(shape))
    return pl.pallas_call(
        functools.partial(_filter_body, l, p),
        grid=(HYENA_ORDER, nct),
        in_specs=[full((2 * l, FILTER_HIDDEN)),
                  pl.BlockSpec((FILTER_HIDDEN, ct), lambda o, j: (0, o * 2 * nct + j)),
                  pl.BlockSpec((FILTER_HIDDEN, ct), lambda o, j: (0, o * 2 * nct + nct + j)),
                  pl.BlockSpec((1, ct), lambda o, j: (0, j)),
                  full((2 * p, 2 * p))],
        out_specs=pl.BlockSpec((None, nwin, 2 * p, ct), lambda o, j: (o, 0, 0, j)),
        out_shape=jax.ShapeDtypeStruct((HYENA_ORDER, nwin, 2 * p, c), F32),
        scratch_shapes=[pltpu.VMEM((2 * l, ct), BF16), pltpu.VMEM((2 * p, 2 * p), BF16)],
        compiler_params=_cparams(("parallel", "parallel")),
        name=f"hyena_filter_{l}",
    )(hid, w4, w4, jnp.asarray(deltas, F32).reshape(1, c), mg)


_HY_PASS_VREGS = 8


def _hyena_body(nseq, l, p, x1_ref, x2_ref, v_ref, w1_ref, w2_ref, wv_ref, b1_ref, b2_ref, bv_ref, hb_ref,
                g_ref, fz_ref, fi_ref, o_ref, z_scr, c_scr, zh_scr, yh_scr, z1_scr, fz_scr, fi_scr):
    fz_scr[...] = fz_ref[...].astype(BF16)
    fi_scr[...] = fi_ref[...].astype(BF16)
    o = pl.program_id(1)
    s = pl.program_id(2)
    rows = nseq * l
    nb = l // p
    ct = o_ref.shape[1]
    hrc = _HY_PASS_VREGS * 8 * 128 // ct
    pair = yh_scr.shape[1] // ct
    pos = lax.rem(lax.broadcasted_iota(jnp.int32, (rows, ct), 0), l)

    def short_conv(u_ref, w_ref, b_ref):
        u = u_ref[...].astype(F32)
        um = jnp.where(pos == 0, 0.0, pltpu.roll(u, 1, 0))
        up = jnp.where(pos == l - 1, 0.0, pltpu.roll(u, rows - 1, 0))
        w = w_ref[...]
        return um * w[0:1] + u * w[1:2] + up * w[2:3] + b_ref[...]

    def long_conv():
        for q in range(nseq):
            base = q * l

            def block(k):
                return pl.ds(pl.multiple_of(base + k * p, p), p)

            def fwd(jj, _):
                zb = jnp.concatenate([z_scr[block(jj * pair + k), :] for k in range(pair)], axis=1).astype(BF16)
                zh = jnp.dot(fz_scr[...], zb, preferred_element_type=F32)
                for k in range(pair):
                    zh_scr[jj * pair + k] = zh[:, k * ct:(k + 1) * ct]
                return 0

            lax.fori_loop(0, nb // pair, fwd, 0)

            def out_block(ii, _):
                for k in range(pair):
                    i = ii * pair + k
                    for rc in range(p // hrc):
                        re = pl.ds(rc * hrc, hrc)
                        im = pl.ds(p + rc * hrc, hrc)

                        def acc(j, carry):
                            ar, ai = carry
                            w = i - j + (nb - 1)
                            gr, gi = g_ref[w, re, :], g_ref[w, im, :]
                            zr, zi = zh_scr[j, re, :], zh_scr[j, im, :]
                            return ar + gr * zr - gi * zi, ai + gr * zi + gi * zr

                        zero = jnp.zeros((hrc, ct), F32)
                        ar, ai = lax.fori_loop(0, nb, acc, (zero, zero), unroll=True)
                        yh_scr[re, k * ct:(k + 1) * ct] = ar
                        yh_scr[im, k * ct:(k + 1) * ct] = ai
                c = jnp.dot(fi_scr[...], yh_scr[...].astype(BF16), preferred_element_type=F32)
                for k in range(pair):
                    c_scr[block(ii * pair + k), :] = c[:, k * ct:(k + 1) * ct]
                return 0

            lax.fori_loop(0, nb // pair, out_block, 0)

    hb = hb_ref[...]

    @pl.when(o == 0)
    def _():
        v = short_conv(v_ref, wv_ref, bv_ref)
        z_scr[...] = v
        long_conv()
        x1 = short_conv(x1_ref, w1_ref, b1_ref)
        z1_scr[s] = x1 * (c_scr[...] + hb[0:1] * v)

    @pl.when(o == 1)
    def _():
        z1 = z1_scr[s]
        z_scr[...] = z1
        long_conv()
        x2 = short_conv(x2_ref, w2_ref, b2_ref)
        o_ref[...] = (x2 * (c_scr[...] + hb[1:2] * z1)).astype(BF16)


def _hyena(proj, l, p, nseq, short_w, short_b, hy_bias, ghat, c, ct=128):
    r = proj.shape[0]
    rows = nseq * l
    nsb = r // rows
    nct = c // ct
    nwin = ghat.shape[1]
    fwd, _, inv = _shifted_dft_tables(p)
    pair = 2 if (ct < 256 and (l // p) % 2 == 0) else 1
    fz = jnp.asarray(fwd, F32)
    fi = jnp.asarray(inv, F32)
    sb = short_b.reshape(1, 3 * c)
    col = lambda k: (lambda j, o, s: (s, k * nct + j))
    wcol = lambda k: (lambda j, o, s: (0, k * nct + j))
    return pl.pallas_call(
        functools.partial(_hyena_body, nseq, l, p),
        grid=(nct, HYENA_ORDER, nsb),
        in_specs=[pl.BlockSpec((rows, ct), col(0)), pl.BlockSpec((rows, ct), col(1)), pl.BlockSpec((rows, ct), col(2)),
                  pl.BlockSpec((3, ct), wcol(0)), pl.BlockSpec((3, ct), wcol(1)), pl.BlockSpec((3, ct), wcol(2)),
                  pl.BlockSpec((1, ct), wcol(0)), pl.BlockSpec((1, ct), wcol(1)), pl.BlockSpec((1, ct), wcol(2)),
                  pl.BlockSpec((HYENA_ORDER, ct), lambda j, o, s: (0, j)),
                  pl.BlockSpec((None, nwin, 2 * p, ct), lambda j, o, s: (o, 0, 0, j)),
                  pl.BlockSpec((2 * p, p), lambda j, o, s: (0, 0)),
                  pl.BlockSpec((p, 2 * p), lambda j, o, s: (0, 0))],
        out_specs=pl.BlockSpec((rows, ct), lambda j, o, s: (s * o, j)),
        out_shape=jax.ShapeDtypeStruct((r, c), BF16),
        scratch_shapes=[pltpu.VMEM((rows, ct), F32), pltpu.VMEM((rows, ct), F32),
                        pltpu.VMEM((l // p, 2 * p, ct), F32), pltpu.VMEM((2 * p, pair * ct), F32),
                        pltpu.VMEM((nsb, rows, ct), F32), pltpu.VMEM((2 * p, p), BF16), pltpu.VMEM((p, 2 * p), BF16)],
        compiler_params=_cparams(("parallel", "arbitrary", "arbitrary")),
        name=f"hyena_{l}",
    )(proj, proj, proj, short_w, short_w, short_w, sb, sb, sb, hy_bias, ghat, fz, fi)


def _s5_param_body(backward, lre_ref, lim_ref, ldt_ref, btr_ref, bti_ref, cr_ref, ci_ref,
                   k_ref, wsr_ref, wsi_ref, wor_ref, woi_ref, etr_ref, eti_ref):
    t = S5_CHUNK
    lr = jnp.minimum(lre_ref[...], LAMBDA_RE_MAX)
    li = lim_ref[...]
    dt = jnp.exp(ldt_ref[...])
    ar, ai = lr * dt, li * dt

    def power(e):
        mag = jnp.exp(e * ar)
        return mag * jnp.cos(e * ai), mag * jnp.sin(e * ai)

    per_step = lambda a: a[:, None, :]

    one = jnp.ones((1, 1), F32)
    lbr, lbi = power(one)
    nr, ni = lbr - 1.0, lbi
    den = lr * lr + li * li
    qr, qi = (nr * lr + ni * li) / den, (ni * lr - nr * li) / den
    btr, bti = btr_ref[...], bti_ref[...]
    bbr, bbi = qr * btr - qi * bti, qr * bti + qi * btr
    cr, ci = cr_ref[...], ci_ref[...]

    step = lax.broadcasted_iota(jnp.int32, (t, 1), 0).astype(F32)
    er, ei = map(per_step, power(step))
    cer, cei = cr[None] * er - ci[None] * ei, cr[None] * ei + ci[None] * er
    nt = (((1,), (1,)), ((), ()))
    dg = functools.partial(lax.dot_general, dimension_numbers=nt, precision=HIGHEST, preferred_element_type=F32)
    k_ref[...] = (dg(cer.reshape(t * SSM_GROUP, SSM_STATE), bbr)
                  - dg(cei.reshape(t * SSM_GROUP, SSM_STATE), bbi))
    er, ei = map(per_step, power(jnp.where(backward, step, (t - 1.0) - step)))
    wsr_ref[...] = er * bbr[None] - ei * bbi[None]
    wsi_ref[...] = er * bbi[None] + ei * bbr[None]
    er, ei = map(per_step, power(jnp.where(backward, t - step, step + 1.0)))
    wor_ref[...] = cr[None] * er - ci[None] * ei
    woi_ref[...] = -(cr[None] * ei + ci[None] * er)
    etr, eti = power(one * t)
    etr_ref[...] = etr
    eti_ref[...] = eti


def _s5_param_tile_body(lre, lim, ldt, btr, bti, cr, ci, k_ref, ws_ref, wo_ref, et_ref):
    ps = SSM_STATE
    for d in range(2):
        for gi in range(lre.shape[1]):
            lanes = lambda part: pl.ds((2 * part + d) * ps, ps)
            _s5_param_body(d == 1, *(r.at[d, gi] for r in (lre, lim, ldt, btr, bti, cr, ci)), k_ref.at[d, gi],
                           ws_ref.at[gi, :, :, lanes(0)], ws_ref.at[gi, :, :, lanes(1)],
                           wo_ref.at[gi, :, :, lanes(0)], wo_ref.at[gi, :, :, lanes(1)],
                           et_ref.at[gi, :, lanes(0)], et_ref.at[gi, :, lanes(1)])


def _s5_params(lam_re, lam_im, log_dt, b_re, b_im, c_re, c_im):
    g = lam_re.shape[1]
    gb = S5_TILE_GROUPS
    t, cg, ps = S5_CHUNK, SSM_GROUP, SSM_STATE
    vec = lambda a: a.reshape(2, g, 1, ps)
    ldt = jnp.broadcast_to(log_dt[:, :, None, None], (2, g, 1, ps))
    bt = lambda a: jnp.swapaxes(a, -1, -2)
    in_spec = lambda *shape: pl.BlockSpec((2, gb) + shape, lambda i: (0, i) + (0,) * len(shape))
    out_spec = lambda *shape: pl.BlockSpec((gb,) + shape, lambda i: (i,) + (0,) * len(shape))
    outs = [jax.ShapeDtypeStruct((2, g, t * cg, cg), F32)] + \
           [jax.ShapeDtypeStruct((g, t, cg, 4 * ps), F32)] * 2 + [jax.ShapeDtypeStruct((g, 1, 4 * ps), F32)]
    return pl.pallas_call(
        _s5_param_tile_body,
        grid=(g // gb,),
        in_specs=[in_spec(1, ps), in_spec(1, ps), in_spec(1, ps), in_spec(cg, ps), in_spec(cg, ps),
                  in_spec(cg, ps), in_spec(cg, ps)],
        out_specs=[in_spec(t * cg, cg), out_spec(t, cg, 4 * ps), out_spec(t, cg, 4 * ps), out_spec(1, 4 * ps)],
        out_shape=outs,
        compiler_params=_cparams(("parallel",)),
        name="s5_params",
    )(vec(lam_re), vec(lam_im), ldt, bt(b_re), bt(b_im), c_re, c_im)


def _s5_operators(lam_re, lam_im, log_dt, b_re, b_im, c_re, c_im):
    kk, ws, wo, et = _s5_params(lam_re, lam_im, log_dt, b_re, b_im, c_re, c_im)
    kk, ws, wo = (a.astype(BF16) for a in (kk, ws, wo))
    g = kk.shape[1]
    t, cg, ps = S5_CHUNK, SSM_GROUP, SSM_STATE
    kd = kk.reshape(2, g, t, cg, cg)
    s_idx = np.arange(t)[:, None]
    t_idx = np.arange(t)[None, :]

    def toeplitz(kdir, lag):
        m = jnp.where((lag >= 0)[None, :, :, None, None], kdir[:, np.mod(lag, t)], 0)
        return jnp.transpose(m, (0, 1, 4, 2, 3)).reshape(g, t * cg, t * cg)

    mf = toeplitz(kd[0], t_idx - s_idx)
    mb = toeplitz(kd[1], s_idx - t_idx)
    w1 = jnp.concatenate([mf, mb, ws.reshape(g, t * cg, 4 * ps)], axis=-1)
    w2 = jnp.swapaxes(wo.reshape(g, t * cg, 4 * ps), 1, 2)
    lam = et.reshape(g // S5_TILE_GROUPS, 1, S5_TILE_GROUPS * 4 * ps)
    return w1, w2, lam


def _chunk_lane_permutation():
    t, gl, cg = S5_CHUNK, S5_TILE_GROUPS, SSM_GROUP
    src = np.arange(t * gl * cg).reshape(t, gl, cg)
    dst = np.transpose(src, (1, 0, 2)).reshape(-1)
    perm = np.zeros((t * gl * cg, t * gl * cg), np.float32)
    perm[dst, np.arange(t * gl * cg)] = 1.0
    return perm


def _s5_body(bsz, nk, chunk_major, u_ref, perm_ref, permt_ref, w1_ref, w2_ref, lam_ref, s0_ref, y_ref, fin_ref,
             tok_scr, u_scr, ug_scr, yl_scr, x_scr, p_scr):
    t_, lt, gl = S5_CHUNK, 128, S5_TILE_GROUPS
    half = SSM_STATE
    sw = 4 * half
    l = nk * t_
    tw = t_ * SSM_GROUP
    nrows = bsz * nk

    def token_rows(t):
        if chunk_major:
            return [(k * bsz, bsz, k * t_ + t, l) for k in range(nk)]
        return [(b * nk, nk, b * l + t, t_) for b in range(bsz)]

    tok_scr[...] = u_ref[...].astype(F32)
    for t in range(t_):
        for r0, n, tok0, stride in token_rows(t):
            u_scr[r0:r0 + n, t * lt:(t + 1) * lt] = tok_scr[pl.ds(tok0, n, stride=stride), :].astype(BF16)
    ug_scr[...] = jnp.dot(u_scr[...], perm_ref[...], preferred_element_type=F32).astype(BF16)
    for gi in range(gl):
        z = jnp.dot(ug_scr[:, gi * tw:(gi + 1) * tw], w1_ref[gi], preferred_element_type=F32)
        yl_scr[:, gi * tw:(gi + 1) * tw] = z[:, :tw] + z[:, tw:2 * tw]
        if chunk_major:
            x_scr[:, gi * sw:(gi + 1) * sw] = z[:, 2 * tw:]
        else:
            x_scr[0, pl.ds(gi, nrows, stride=gl), :] = z[:, 2 * tw:2 * tw + 2 * half]
            x_scr[1, pl.ds(gi, nrows, stride=gl), :] = z[:, 2 * tw + 2 * half:]
    p_scr[...] = jnp.zeros(p_scr.shape, F32)

    srows = bsz if chunk_major else gl
    fwd_lane = lax.broadcasted_iota(jnp.int32, (srows, 2 * half), 1) < half
    bwd_lane = jnp.logical_not(fwd_lane)
    lam = lam_ref[...]

    def advance(sr, si, lr, li, kf, kb, xr_ref, xi_ref, pr_ref, pi_ref):
        for ref, val in ((pr_ref, sr), (pi_ref, si)):
            pltpu.store(ref.at[kf, :], val, mask=fwd_lane)
            pltpu.store(ref.at[kb, :], val, mask=bwd_lane)
        xr = jnp.where(fwd_lane, xr_ref[kf, :], xr_ref[kb, :])
        xi = jnp.where(fwd_lane, xi_ref[kf, :], xi_ref[kb, :])
        return lr * sr - li * si + xr, lr * si + li * sr + xi

    if chunk_major:
        def step(t, carry):
            kf = pl.ds(pl.multiple_of(t * bsz, bsz), bsz)
            kb = pl.ds(pl.multiple_of((nk - 1 - t) * bsz, bsz), bsz)
            out = []
            for gi in range(gl):
                ca, cb = pl.ds(gi * sw, 2 * half), pl.ds(gi * sw + 2 * half, 2 * half)
                out.append(advance(*carry[gi], lam[:, gi * sw:gi * sw + 2 * half],
                                   lam[:, gi * sw + 2 * half:(gi + 1) * sw], kf, kb,
                                   x_scr.at[:, ca], x_scr.at[:, cb], p_scr.at[:, ca], p_scr.at[:, cb]))
            return tuple(out)

        init = tuple((s0_ref[:, gi * sw:gi * sw + 2 * half], s0_ref[:, gi * sw + 2 * half:(gi + 1) * sw])
                     for gi in range(gl))
        fin = lax.fori_loop(0, nk, step, init)
        for gi in range(gl):
            fin_ref[:, gi * sw:gi * sw + 2 * half] = fin[gi][0]
            fin_ref[:, gi * sw + 2 * half:(gi + 1) * sw] = fin[gi][1]
    else:
        by_group = lambda row: jnp.concatenate([row[:, gi * sw:(gi + 1) * sw] for gi in range(gl)], axis=0)
        lam8 = by_group(lam)

        def step(t, carry):
            out = []
            for q in range(bsz):
                kf = pl.ds(pl.multiple_of((q * nk + t) * gl, gl), gl)
                kb = pl.ds(pl.multiple_of((q * nk + nk - 1 - t) * gl, gl), gl)
                out.append(advance(*carry[q], lam8[:, 0:2 * half], lam8[:, 2 * half:], kf, kb,
                                   x_scr.at[0], x_scr.at[1], p_scr.at[0], p_scr.at[1]))
            return tuple(out)

        s0 = [by_group(s0_ref[q:q + 1, :]) for q in range(bsz)]
        fin = lax.fori_loop(0, nk, step, tuple((a[:, 0:2 * half], a[:, 2 * half:]) for a in s0), unroll=8)
        for q in range(bsz):
            for gi in range(gl):
                fin_ref[q:q + 1, gi * sw:gi * sw + 2 * half] = fin[q][0][gi:gi + 1]
                fin_ref[q:q + 1, gi * sw + 2 * half:(gi + 1) * sw] = fin[q][1][gi:gi + 1]

    for gi in range(gl):
        if chunk_major:
            pg = p_scr[:, gi * sw:(gi + 1) * sw]
        else:
            pg = jnp.concatenate([p_scr[0, pl.ds(gi, nrows, stride=gl), :], p_scr[1, pl.ds(gi, nrows, stride=gl), :]],
                                 axis=1)
        y = yl_scr[:, gi * tw:(gi + 1) * tw] + jnp.dot(pg.astype(BF16), w2_ref[gi], preferred_element_type=F32)
        u_scr[:, gi * tw:(gi + 1) * tw] = y.astype(BF16)
    yt = jnp.dot(u_scr[...], permt_ref[...], preferred_element_type=F32)
    for t in range(t_):
        for r0, n, tok0, stride in token_rows(t):
            tok_scr[pl.ds(tok0, n, stride=stride), :] = yt[r0:r0 + n, t * lt:(t + 1) * lt]
    y_ref[...] = tok_scr[...].astype(BF16)


def _s5(proj, col0, bsz, l, ops, s0):
    w1, w2, lam = ops
    g = w1.shape[0]
    t, ps, gl, lt = S5_CHUNK, SSM_STATE, S5_TILE_GROUPS, 128
    nb = g // gl
    nk = l // t
    r = bsz * l
    nrows = bsz * nk
    sw = 4 * ps
    cw = t * lt
    tw = t * SSM_GROUP
    chunk_major = bsz % 8 == 0
    state_shape = (nrows, gl * sw) if chunk_major else (2, nrows * gl, sw // 2)
    if s0 is None:
        s0p = jnp.zeros((bsz, g * sw), F32)
    else:
        s0_re, s0_im = s0
        s0p = jnp.concatenate([s0_re[:, 0], s0_re[:, 1], s0_im[:, 0], s0_im[:, 1]], axis=-1).reshape(bsz, g * sw)
    perm = _chunk_lane_permutation()
    const = lambda shape: _resident(shape, lambda i: (0,) * len(shape))
    y, fin = pl.pallas_call(
        functools.partial(_s5_body, bsz, nk, chunk_major),
        grid=(nb,),
        in_specs=[pl.BlockSpec((r, lt), lambda i: (0, col0 // lt + i), pipeline_mode=pl.Buffered(1)),
                  const((cw, cw)), const((cw, cw)),
                  pl.BlockSpec((gl, tw, 3 * tw), lambda i: (i, 0, 0)), pl.BlockSpec((gl, tw, tw), lambda i: (i, 0, 0)),
                  pl.BlockSpec((None, 1, gl * sw), lambda i: (i, 0, 0)),
                  pl.BlockSpec((bsz, gl * sw), lambda i: (0, i))],
        out_specs=[pl.BlockSpec((r, lt), lambda i: (0, i)), pl.BlockSpec((bsz, gl * sw), lambda i: (0, i))],
        out_shape=[jax.ShapeDtypeStruct((r, g * SSM_GROUP), BF16), jax.ShapeDtypeStruct((bsz, g * sw), F32)],
        scratch_shapes=[pltpu.VMEM((r, lt), F32), pltpu.VMEM((nrows, cw), BF16), pltpu.VMEM((nrows, cw), BF16),
                        pltpu.VMEM((nrows, cw), F32)] + [pltpu.VMEM(state_shape, F32)] * 2,
        compiler_params=_cparams(("parallel",)),
        name=f"s5_scan_{nk}",
    )(proj, jnp.asarray(perm, BF16), jnp.asarray(perm.T, BF16), w1, w2, lam, s0p)
    fin = jnp.transpose(fin.reshape(bsz, g, 2, 2, ps), (2, 0, 3, 1, 4))
    return y, fin[0], fin[1]


def _merge_body(yh_ref, ys_ref, u_ref, gh_ref, gs_ref, x_ref, d_ref, gw_ref, gbias_ref, wbh_ref, wbs_ref, wo_ref,
                g1_ref, n2_ref, sc2_ref, sh2_ref, x1_ref, h2_ref):
    y = ys_ref[...].astype(F32) + u_ref[...].astype(F32) * d_ref[...]
    g = jax.nn.gelu(y)
    gl = jnp.dot(g.astype(BF16), gw_ref[...], preferred_element_type=F32) + gbias_ref[...]
    ys = (g * jax.nn.sigmoid(gl)).astype(BF16)
    bh = jnp.dot(yh_ref[...], wbh_ref[...], preferred_element_type=F32)
    bs = jnp.dot(ys, wbs_ref[...], preferred_element_type=F32)
    merged = jax.nn.sigmoid(gh_ref[...].astype(F32)) * bh + jax.nn.sigmoid(gs_ref[...].astype(F32)) * bs
    x1 = x_ref[...] + g1_ref[...] * jnp.dot(merged.astype(BF16), wo_ref[...], preferred_element_type=F32)
    x1_ref[...] = x1
    h2_ref[...] = _mod_rmsnorm(x1, n2_ref[...], sc2_ref[...], sh2_ref[...]).astype(BF16)


def _merge(yh, ys, proj, x, mod4, cond_of_tile, ssm_d, glu_w, glu_b, wbh, wbs, wo, n2g, c_h, c_s, tm=256):
    r, d = x.shape
    row_blk = lambda width, k: pl.BlockSpec((tm, width), lambda i: (i, k))
    mod_spec = lambda chunk: pl.BlockSpec((None, None, 1, d), lambda i: (cond_of_tile(i, tm), chunk, 0, 0))
    const = lambda shape: _resident(shape, lambda i: (0,) * len(shape))
    u_col = (3 * c_h) // c_s
    gate0 = (3 * c_h + c_s) // d
    return pl.pallas_call(
        _merge_body,
        grid=(r // tm,),
        in_specs=[row_blk(c_h, 0), row_blk(c_s, 0), row_blk(c_s, u_col), row_blk(d, gate0), row_blk(d, gate0 + 1),
                  row_blk(d, 0), const((1, c_s)), const((c_s, c_s)), const((1, c_s)), const((c_h, d)),
                  const((c_s, d)), const((d, d)), mod_spec(2), const((1, d)), mod_spec(4), mod_spec(3)],
        out_specs=[row_blk(d, 0), row_blk(d, 0)],
        out_shape=[jax.ShapeDtypeStruct((r, d), F32), jax.ShapeDtypeStruct((r, d), BF16)],
        compiler_params=_cparams(("parallel",)),
        name="merge",
    )(yh, ys, proj, proj, proj, x, ssm_d.reshape(1, c_s), glu_w, glu_b.reshape(1, c_s), wbh, wbs, wo,
      mod4, n2g.reshape(1, d), mod4, mod4)


_FFN_RC = 512


def _ffn_up_body(rows, cols, h_ref, wa_ref, wb_ref, cwa_ref, cwb_ref, cba_ref, cbb_ref, o_ref):
    tm, tn = o_ref.shape
    rc = _FFN_RC
    n = tm // rc
    col = lax.rem(lax.broadcasted_iota(jnp.int32, (rc, tn), 0), cols)
    not_first = col != 0
    not_last = col != cols - 1
    tdt = BF16 if rows > 1 else F32
    zero = jnp.zeros((cols, tn), tdt)

    def taps(hc, w_ref, cw_ref, cb_ref):
        a = jnp.dot(hc, w_ref[...], preferred_element_type=F32)
        am = jnp.where(not_first, pltpu.roll(a, 1, 0), 0.0).astype(tdt)
        ap = jnp.where(not_last, pltpu.roll(a, rc - 1, 0), 0.0).astype(tdt)
        a = a.astype(tdt)
        cw = cw_ref[...].astype(tdt)
        tap = lambda i: am * cw[3 * i:3 * i + 1] + a * cw[3 * i + 1:3 * i + 2] + ap * cw[3 * i + 2:3 * i + 3]
        mid = tap(1) + cb_ref[...].astype(tdt)
        return (tap(0), mid, tap(2)) if rows > 1 else (None, mid, None)

    def finish(prev, cur, nxt):
        t0, mid, t2 = cur
        if rows == 1:
            return mid
        above = jnp.concatenate([prev[0][rc - cols:] if prev is not None else zero, t0[:rc - cols]], axis=0)
        below = jnp.concatenate([t2[cols:], nxt[2][:cols] if nxt is not None else zero], axis=0)
        return mid + above + below

    chunks = []
    for i in range(n + 1):
        if i < n:
            hc = h_ref[i * rc:(i + 1) * rc, :]
            chunks.append((taps(hc, wa_ref, cwa_ref, cba_ref), taps(hc, wb_ref, cwb_ref, cbb_ref)))
        if i >= 1:
            j = i - 1
            pick = lambda k, side: chunks[k][side] if 0 <= k < n else None
            a = finish(pick(j - 1, 0), pick(j, 0), pick(j + 1, 0))
            b = finish(pick(j - 1, 1), pick(j, 1), pick(j + 1, 1))
            o_ref[j * rc:(j + 1) * rc, :] = (jax.nn.gelu(a.astype(F32)) * b.astype(F32)).astype(BF16)


def _ffn_up(h2, w_up, conv_w, conv_b, rows, cols, tn=256):
    r, d = h2.shape
    f = w_up.shape[1] // 2
    tm = rows * cols if rows > 1 else 4096
    assert tm % _FFN_RC == 0 and _FFN_RC % cols == 0 and cols % 8 == 0
    nt = f // tn
    cw = conv_w.reshape(9, 2 * f)
    cb = conv_b.reshape(1, 2 * f)
    return pl.pallas_call(
        functools.partial(_ffn_up_body, rows, cols),
        grid=(r // tm, nt),
        in_specs=[pl.BlockSpec((tm, d), lambda i, j: (i, 0), pipeline_mode=pl.Buffered(1)),
                  pl.BlockSpec((d, tn), lambda i, j: (0, j)), pl.BlockSpec((d, tn), lambda i, j: (0, nt + j)),
                  pl.BlockSpec((9, tn), lambda i, j: (0, j)), pl.BlockSpec((9, tn), lambda i, j: (0, nt + j)),
                  pl.BlockSpec((1, tn), lambda i, j: (0, j)), pl.BlockSpec((1, tn), lambda i, j: (0, nt + j))],
        out_specs=pl.BlockSpec((tm, tn), lambda i, j: (i, j)),
        out_shape=jax.ShapeDtypeStruct((r, f), BF16),
        compiler_params=_cparams(("parallel", "arbitrary")),
        name=f"ffn_up_{rows}x{cols}",
    )(h2, w_up, w_up, cw, cw, cb, cb)


def _ffn_down_body(a_ref, w_ref, x_ref, g2_ref, fg_ref, o_ref):
    x2 = x_ref[...] + g2_ref[...] * jnp.dot(a_ref[...], w_ref[...], preferred_element_type=F32)
    ms = jnp.mean(x2 * x2, axis=-1, keepdims=True)
    o_ref[...] = x2 * lax.rsqrt(ms + EPS) * fg_ref[...]


def _ffn_down(act, w, x1, mod4, cond_of_tile, final_g, tm=256):
    r, f = act.shape
    d = w.shape[1]
    return pl.pallas_call(
        _ffn_down_body,
        grid=(r // tm,),
        in_specs=[pl.BlockSpec((tm, f), lambda i: (i, 0)),
                  _resident((f, d), lambda i: (0, 0)),
                  pl.BlockSpec((tm, d), lambda i: (i, 0)),
                  pl.BlockSpec((None, None, 1, d), lambda i: (cond_of_tile(i, tm), 5, 0, 0)),
                  _resident((1, d), lambda i: (0, 0))],
        out_specs=pl.BlockSpec((tm, d), lambda i: (i, 0)),
        out_shape=jax.ShapeDtypeStruct((r, d), F32),
        compiler_params=_cparams(("parallel",)),
        name="ffn_down",
    )(act, w, x1, mod4, final_g.reshape(1, d))


def _segment(x, cond_base, per_batch_cond, s0, rows, cols, hy_block, hy_nseq, mod4, p, final_g):
    bsz, l, d = x.shape
    xf = x.reshape(bsz * l, d)
    c_h = p["wbh"].shape[0]
    c_s = p["wbs"].shape[0]

    def cond_of_tile(i, tm):
        return cond_base + (i * tm) // l if per_batch_cond else cond_base

    proj = _in_proj(xf, mod4, cond_of_tile, p["norm1_g"], p["w_in"], p["b_in"])
    yh = _hyena(proj, l, hy_block, hy_nseq, p["hy_short_w"], p["hy_short_b"], p["hy_bias"], p["ghat"][l], c_h,
                ct=128 if l // hy_block > 1 else 256)
    ys, fin_re, fin_im = _s5(proj, 3 * c_h, bsz, l, p["s5_ops"], s0)
    x1, h2 = _merge(yh, ys, proj, xf, mod4, cond_of_tile, p["ssm_d"], p["glu_w"], p["glu_b"],
                    p["wbh"], p["wbs"], p["wo"], p["norm2_g"], c_h, c_s)
    act = _ffn_up(h2, p["ffn_up"], p["ffn_conv_w"], p["ffn_conv_b"], rows, cols)
    y = _ffn_down(act, p["ffn_down"], x1, mod4, cond_of_tile, final_g)
    return y.reshape(bsz, l, d), fin_re, fin_im


def kernel(x_prompt, x_sample, state_ssm_re, state_ssm_im, c, c_ctx, ada_w, ada_b, norm1_g, norm2_g, final_g, w_in, b_in, hy_short_w, hy_short_b, hy_f_w1, hy_f_b1, hy_f_w2, hy_f_b2, hy_f_w3, hy_f_b3, hy_f_w4, hy_f_freq, hy_bias, w_branch_h, ssm_lambda_re, ssm_lambda_im, ssm_log_dt, ssm_b_re, ssm_b_im, ssm_c_re, ssm_c_im, ssm_d, ssm_glu_w, ssm_glu_b, w_branch_s, w_out, ffn_up, ffn_conv_w, ffn_conv_b, ffn_down):
    depth = ada_w.shape[0]
    assert depth == 1, "the final norm is fused into the (single) layer's last kernel"
    d = x_prompt.shape[-1]
    ctx_len = x_prompt.shape[1]
    lat_len = x_sample.shape[1]
    dec_b = x_sample.shape[0]
    c_h = w_branch_h.shape[1]
    assert 1 + dec_b <= 8
    cond = jnp.zeros((8, d), F32).at[0].set(c_ctx).at[1:1 + dec_b].set(c)
    lay = 0
    mod4 = _ada(cond, ada_w[lay], ada_b[lay]).reshape(8, N_MOD, 1, d)
    hy_blocks = {ctx_len: 256, lat_len: 512}
    ghat = {l: _hyena_filters(l, hy_blocks[l], hy_f_w1[lay], hy_f_b1[lay], hy_f_w2[lay], hy_f_b2[lay],
                              hy_f_w3[lay], hy_f_b3[lay], hy_f_w4[lay], hy_f_freq[lay], c_h)
            for l in (ctx_len, lat_len)}
    s5_ops = _s5_operators(ssm_lambda_re[lay], ssm_lambda_im[lay], ssm_log_dt[lay],
                           ssm_b_re[lay], ssm_b_im[lay], ssm_c_re[lay], ssm_c_im[lay])
    p = dict(norm1_g=norm1_g[lay], norm2_g=norm2_g[lay], w_in=w_in[lay].astype(BF16), b_in=b_in[lay],
             hy_short_w=hy_short_w[lay], hy_short_b=hy_short_b[lay], hy_bias=hy_bias[lay], ghat=ghat,
             s5_ops=s5_ops, ssm_d=ssm_d[lay], glu_w=ssm_glu_w[lay].astype(BF16),
             glu_b=ssm_glu_b[lay], wbh=w_branch_h[lay].astype(BF16), wbs=w_branch_s[lay].astype(BF16),
             wo=w_out[lay].astype(BF16), ffn_up=ffn_up[lay].astype(BF16), ffn_conv_w=ffn_conv_w[lay],
             ffn_conv_b=ffn_conv_b[lay], ffn_down=ffn_down[lay].astype(BF16))
    y_prompt, st_re, st_im = _segment(x_prompt, 0, False, None, 1, ctx_len, hy_blocks[ctx_len], 8, mod4, p, final_g)
    s0 = (state_ssm_re[:, lay].astype(F32), state_ssm_im[:, lay].astype(F32))
    y_sample, _, _ = _segment(x_sample, 1, True, s0, lat_len // GRID_W, GRID_W, hy_blocks[lat_len], 1, mod4, p, final_g)
    return y_prompt, y_sample, st_re[:, None], st_im[:, None]
```

```python
import functools
import math

import jax
import jax.numpy as jnp
import numpy as np
from jax import lax
from jax.experimental import pallas as pl
from jax.experimental.pallas import tpu as pltpu

F32 = jnp.float32
BF16 = jnp.bfloat16
HIGHEST = lax.Precision.HIGHEST

GRID_W = 64
EPS = 1e-6
SSM_GROUP = 16
SSM_STATE = 64
FILTER_EMB = 33
FILTER_HIDDEN = 64
HYENA_ORDER = 2
DECAY_FAST = 0.3
DECAY_SLOW = 1.5
DECAY_TARGET = 1e-2
LAMBDA_RE_MAX = -1e-4

S5_CHUNK = 16
S5_TILE_GROUPS = 8
V7X_VMEM_LIMIT = 56 * 2**20
N_MOD = 6


def _cparams(sem):
    return pltpu.CompilerParams(dimension_semantics=sem, vmem_limit_bytes=V7X_VMEM_LIMIT)


def _resident(shape, index_map):
    return pl.BlockSpec(shape, index_map, pipeline_mode=pl.Buffered(1))


def _mod_rmsnorm(x, g, sc, sh):
    ms = jnp.mean(x * x, axis=-1, keepdims=True)
    return x * lax.rsqrt(ms + EPS) * g * (1.0 + sc) + sh


def _ada_body(c_ref, w_ref, b_ref, o_ref):
    c = c_ref[...]
    s = c * jax.nn.sigmoid(c)
    o_ref[...] = _dot_3pass(s, *_split_bf16(w_ref[...])) + b_ref[...]


def _ada(cond, w, b):
    d, n = w.shape
    tn = 1024
    return pl.pallas_call(
        _ada_body,
        grid=(n // tn,),
        in_specs=[pl.BlockSpec((8, d), lambda j: (0, 0)),
                  pl.BlockSpec((d, tn), lambda j: (0, j)),
                  pl.BlockSpec((1, tn), lambda j: (0, j))],
        out_specs=pl.BlockSpec((8, tn), lambda j: (0, j)),
        out_shape=jax.ShapeDtypeStruct((8, n), F32),
        compiler_params=_cparams(("parallel",)),
        name="ada",
    )(cond, w, b.reshape(1, n))


def _inproj_body(x_ref, g_ref, sc_ref, sh_ref, w_ref, b_ref, o_ref, h_scr):
    @pl.when(pl.program_id(1) == 0)
    def _():
        h_scr[...] = _mod_rmsnorm(x_ref[...], g_ref[...], sc_ref[...], sh_ref[...]).astype(BF16)

    o_ref[...] = (jnp.dot(h_scr[...], w_ref[...], preferred_element_type=F32) + b_ref[...]).astype(BF16)


def _in_proj(x, mod4, cond_of_tile, g, w, b, tm=1024, tn=1024):
    r, d = x.shape
    n = w.shape[1]
    mod_spec = lambda chunk: pl.BlockSpec((None, None, 1, d), lambda i, j: (cond_of_tile(i, tm), chunk, 0, 0))
    return pl.pallas_call(
        _inproj_body,
        grid=(r // tm, n // tn),
        in_specs=[pl.BlockSpec((tm, d), lambda i, j: (i, 0)),
                  pl.BlockSpec((1, d), lambda i, j: (0, 0)),
                  mod_spec(1), mod_spec(0),
                  pl.BlockSpec((d, tn), lambda i, j: (0, j)),
                  pl.BlockSpec((1, tn), lambda i, j: (0, j))],
        out_specs=pl.BlockSpec((tm, tn), lambda i, j: (i, j)),
        out_shape=jax.ShapeDtypeStruct((r, n), BF16),
        scratch_shapes=[pltpu.VMEM((tm, d), BF16)],
        compiler_params=_cparams(("parallel", "arbitrary")),
        name="in_proj",
    )(x, g.reshape(1, d), mod4, mod4, w, b.reshape(1, n))


def _shifted_dft_tables(p):
    theta = np.pi * (2.0 * np.arange(p) + 1.0) / (2.0 * p)
    n = np.arange(p)
    a = theta[:, None] * n[None, :]
    fwd = np.concatenate([np.cos(a), -np.sin(a)], axis=0)
    m = np.arange(2 * p) - p
    am = theta[:, None] * m[None, :]
    filt = np.concatenate([np.cos(am), -np.sin(am)], axis=0)
    filt[:, 0] = 0.0
    inv = np.concatenate([np.cos(a).T, -np.sin(a).T], axis=1) / p
    return fwd, filt, inv


_FILTER_RC = 1024


def _filter_positions(l):
    pos = np.abs(np.arange(2 * l) - l).astype(np.float64)
    t = pos / (l - 1)
    bands = (FILTER_EMB - 1) // 2
    fr = np.linspace(1e-4, bands - 1, bands)
    ang = (2.0 * math.pi / l) * pos[:, None] * fr[None, :]
    z = np.concatenate([t[:, None], np.cos(ang), -np.sin(ang)], axis=-1)
    zp = np.zeros((2 * l, 128), np.float32)
    zp[:, :FILTER_EMB] = z
    return zp


def _filter_hidden_body(z_ref, w1_ref, b1_ref, w2_ref, b2_ref, w3_ref, b3_ref, fq_ref, o_ref):
    fq = fq_ref[...]
    dot = functools.partial(jnp.dot, precision=HIGHEST, preferred_element_type=F32)
    h = jnp.sin(fq * (dot(z_ref[...], w1_ref[...]) + b1_ref[...]))
    h = jnp.sin(fq * (dot(h, w2_ref[...]) + b2_ref[...]))
    o_ref[...] = jnp.sin(fq * (dot(h, w3_ref[...]) + b3_ref[...]))


def _filter_hidden(zp, w1p, b1, w2, b2, w3, b3, freq):
    rows, kin = zp.shape
    rc = min(rows, _FILTER_RC)
    hd = w2.shape[0]
    full = lambda shape: pl.BlockSpec(shape, lambda i: (0,) * len(shape))
    row = lambda a: a.reshape(1, -1)
    return pl.pallas_call(
        _filter_hidden_body,
        grid=(rows // rc,),
        in_specs=[pl.BlockSpec((rc, kin), lambda i: (i, 0)), full((kin, hd)), full((1, hd)), full((hd, hd)),
                  full((1, hd)), full((hd, hd)), full((1, hd)), full((1, hd))],
        out_specs=pl.BlockSpec((rc, hd), lambda i: (i, 0)),
        out_shape=jax.ShapeDtypeStruct((rows, hd), F32),
        compiler_params=_cparams(("parallel",)),
        name=f"hyena_filter_mlp_{rows}",
    )(zp, w1p, row(b1), w2, row(b2), w3, row(b3), row(freq))


def _split_bf16(x):
    hi = x.astype(BF16)
    return hi, (x - hi.astype(F32)).astype(BF16)


def _dot_3pass(x, w_hi, w_lo):
    x_hi, x_lo = _split_bf16(x)
    d = functools.partial(jnp.dot, preferred_element_type=F32)
    return d(x_hi, w_hi) + (d(x_lo, w_hi) + d(x_hi, w_lo))


def _filter_body(l, p, h_ref, w4f_ref, w4b_ref, dl_ref, mg_ref, o_ref, kk_scr, mg_scr):
    mg_scr[...] = mg_ref[...].astype(BF16)
    rc = min(l, _FILTER_RC)
    ct = kk_scr.shape[1]
    nhalf = l // rc

    def taps(w_ref, first):
        w_hi, w_lo = _split_bf16(w_ref[...])

        def body(i, asum):
            r0 = pl.multiple_of(i * rc, rc)
            q = r0 + lax.broadcasted_iota(jnp.int32, (rc, ct), 0)
            t = jnp.abs(q - l).astype(F32) * (1.0 / (l - 1))
            k = _dot_3pass(h_ref[pl.ds(r0, rc), :], w_hi, w_lo) * jnp.exp(-t * dl_ref[...])
            k = jnp.where(q == 0, 0.0, k)
            kk_scr[pl.ds(r0, rc), :] = k.astype(BF16)
            return asum + jnp.sum(jnp.abs(k), axis=0, keepdims=True)

        return lambda asum: lax.fori_loop(first, first + nhalf, body, asum)

    asum = taps(w4f_ref, nhalf)(taps(w4b_ref, 0)(jnp.zeros((1, ct), F32)))
    scale = 1.0 / asum
    nwin = 2 * (l // p) - 1

    def win(w, _):
        seg = kk_scr[pl.ds(pl.multiple_of(w * p, p), 2 * p), :]
        o_ref[w] = jnp.dot(mg_scr[...], seg, preferred_element_type=F32) * scale
        return 0

    lax.fori_loop(0, nwin, win, 0)


def _hyena_filters(l, p, w1, b1, w2, b2, w3, b3, w4, freq, c, ct=256):
    nwin = 2 * (l // p) - 1
    nct = c // ct
    zp = _filter_positions(l)
    zp2 = jnp.asarray(np.concatenate([zp[:l], zp[l:]], axis=1))
    w1p = jnp.zeros((128, FILTER_HIDDEN), F32).at[:FILTER_EMB].set(w1)
    twice = lambda w: jnp.zeros((2 * w.shape[0], 2 * w.shape[1]), F32).at[:w.shape[0], :w.shape[1]].set(w) \
        .at[w.shape[0]:, w.shape[1]:].set(w)
    both = lambda v: jnp.concatenate([v, v])
    hid2 = _filter_hidden(zp2, twice(w1p), both(b1), twice(w2), both(b2), twice(w3), both(b3), both(freq))
    hid = jnp.concatenate([hid2[:, :FILTER_HIDDEN], hid2[:, FILTER_HIDDEN:]], axis=0)
    deltas = np.abs(np.linspace(math.log(DECAY_TARGET) / DECAY_SLOW, math.log(DECAY_TARGET) / DECAY_FAST, c))
    mg = jnp.asarray(_shifted_dft_tables(p)[1], F32)
    full = lambda shape: pl.BlockSpec(shape, lambda o, j: (0,) * len(shape))
    return pl.pallas_call(
        functools.partial(_filter_body, l, p),
        grid=(HYENA_ORDER, nct),
        in_specs=[full((2 * l, FILTER_HIDDEN)),
                  pl.BlockSpec((FILTER_HIDDEN, ct), lambda o, j: (0, o * 2 * nct + j)),
                  pl.BlockSpec((FILTER_HIDDEN, ct), lambda o, j: (0, o * 2 * nct + nct + j)),
                  pl.BlockSpec((1, ct), lambda o, j: (0, j)),
                  full((2 * p, 2 * p))],
        out_specs=pl.BlockSpec((None, nwin, 2 * p, ct), lambda o, j: (o, 0, 0, j)),
        out_shape=jax.ShapeDtypeStruct((HYENA_ORDER, nwin, 2 * p, c), F32),
        scratch_shapes=[pltpu.VMEM((2 * l, ct), BF16), pltpu.VMEM((2 * p, 2 * p), BF16)],
        compiler_params=_cparams(("parallel", "parallel")),
        name=f"hyena_filter_{l}",
    )(hid, w4, w4, jnp.asarray(deltas, F32).reshape(1, c), mg)


_HY_PASS_VREGS = 8


def _hyena_body(nseq, l, p, x1_ref, x2_ref, v_ref, w1_ref, w2_ref, wv_ref, b1_ref, b2_ref, bv_ref, hb_ref,
                g_ref, fz_ref, fi_ref, o_ref, z_scr, c_scr, zh_scr, yh_scr, z1_scr, fz_scr, fi_scr):
    fz_scr[...] = fz_ref[...].astype(BF16)
    fi_scr[...] = fi_ref[...].astype(BF16)
    o = pl.program_id(1)
    s = pl.program_id(2)
    rows = nseq * l
    nb = l // p
    ct = o_ref.shape[1]
    hrc = _HY_PASS_VREGS * 8 * 128 // ct
    pair = yh_scr.shape[1] // ct
    pos = lax.rem(lax.broadcasted_iota(jnp.int32, (rows, ct), 0), l)

    def short_conv(u_ref, w_ref, b_ref):
        u = u_ref[...].astype(F32)
        um = jnp.where(pos == 0, 0.0, pltpu.roll(u, 1, 0))
        up = jnp.where(pos == l - 1, 0.0, pltpu.roll(u, rows - 1, 0))
        w = w_ref[...]
        return um * w[0:1] + u * w[1:2] + up * w[2:3] + b_ref[...]

    def long_conv():
        for q in range(nseq):
            base = q * l

            def block(k):
                return pl.ds(pl.multiple_of(base + k * p, p), p)

            def fwd(jj, _):
                zb = jnp.concatenate([z_scr[block(jj * pair + k), :] for k in range(pair)], axis=1).astype(BF16)
                zh = jnp.dot(fz_scr[...], zb, preferred_element_type=F32)
                for k in range(pair):
                    zh_scr[jj * pair + k] = zh[:, k * ct:(k + 1) * ct]
                return 0

            lax.fori_loop(0, nb // pair, fwd, 0)

            def out_block(ii, _):
                for k in range(pair):
                    i = ii * pair + k
                    for rc in range(p // hrc):
                        re = pl.ds(rc * hrc, hrc)
                        im = pl.ds(p + rc * hrc, hrc)

                        def acc(j, carry):
                            ar, ai = carry
                            w = i - j + (nb - 1)
                            gr, gi = g_ref[w, re, :], g_ref[w, im, :]
                            zr, zi = zh_scr[j, re, :], zh_scr[j, im, :]
                            return ar + gr * zr - gi * zi, ai + gr * zi + gi * zr

                        zero = jnp.zeros((hrc, ct), F32)
                        ar, ai = lax.fori_loop(0, nb, acc, (zero, zero), unroll=True)
                        yh_scr[re, k * ct:(k + 1) * ct] = ar
                        yh_scr[im, k * ct:(k + 1) * ct] = ai
                c = jnp.dot(fi_scr[...], yh_scr[...].astype(BF16), preferred_element_type=F32)
                for k in range(pair):
                    c_scr[block(ii * pair + k), :] = c[:, k * ct:(k + 1) * ct]
                return 0

            lax.fori_loop(0, nb // pair, out_block, 0)

    hb = hb_ref[...]

    @pl.when(o == 0)
    def _():
        v = short_conv(v_ref, wv_ref, bv_ref)
        z_scr[...] = v
        long_conv()
        x1 = short_conv(x1_ref, w1_ref, b1_ref)
        z1_scr[s] = x1 * (c_scr[...] + hb[0:1] * v)

    @pl.when(o == 1)
    def _():
        z1 = z1_scr[s]
        z_scr[...] = z1
        long_conv()
        x2 = short_conv(x2_ref, w2_ref, b2_ref)
        o_ref[...] = (x2 * (c_scr[...] + hb[1:2] * z1)).astype(BF16)


def _hyena(proj, l, p, nseq, short_w, short_b, hy_bias, ghat, c, ct=128):
    r = proj.shape[0]
    rows = nseq * l
    nsb = r // rows
    nct = c // ct
    nwin = ghat.shape[1]
    fwd, _, inv = _shifted_dft_tables(p)
    pair = 2 if (ct < 256 and (l // p) % 2 == 0) else 1
    fz = jnp.asarray(fwd, F32)
    fi = jnp.asarray(inv, F32)
    sb = short_b.reshape(1, 3 * c)
    col = lambda k: (lambda j, o, s: (s, k * nct + j))
    wcol = lambda k: (lambda j, o, s: (0, k * nct + j))
    return pl.pallas_call(
        functools.partial(_hyena_body, nseq, l, p),
        grid=(nct, HYENA_ORDER, nsb),
        in_specs=[pl.BlockSpec((rows, ct), col(0)), pl.BlockSpec((rows, ct), col(1)), pl.BlockSpec((rows, ct), col(2)),
                  pl.BlockSpec((3, ct), wcol(0)), pl.BlockSpec((3, ct), wcol(1)), pl.BlockSpec((3, ct), wcol(2)),
                  pl.BlockSpec((1, ct), wcol(0)), pl.BlockSpec((1, ct), wcol(1)), pl.BlockSpec((1, ct), wcol(2)),
                  pl.BlockSpec((HYENA_ORDER, ct), lambda j, o, s: (0, j)),
                  pl.BlockSpec((None, nwin, 2 * p, ct), lambda j, o, s: (o, 0, 0, j)),
                  pl.BlockSpec((2 * p, p), lambda j, o, s: (0, 0)),
                  pl.BlockSpec((p, 2 * p), lambda j, o, s: (0, 0))],
        out_specs=pl.BlockSpec((rows, ct), lambda j, o, s: (s * o, j)),
        out_shape=jax.ShapeDtypeStruct((r, c), BF16),
        scratch_shapes=[pltpu.VMEM((rows, ct), F32), pltpu.VMEM((rows, ct), F32),
                        pltpu.VMEM((l // p, 2 * p, ct), F32), pltpu.VMEM((2 * p, pair * ct), F32),
                        pltpu.VMEM((nsb, rows, ct), F32), pltpu.VMEM((2 * p, p), BF16), pltpu.VMEM((p, 2 * p), BF16)],
        compiler_params=_cparams(("parallel", "arbitrary", "arbitrary")),
        name=f"hyena_{l}",
    )(proj, proj, proj, short_w, short_w, short_w, sb, sb, sb, hy_bias, ghat, fz, fi)


def _s5_param_body(backward, lre_ref, lim_ref, ldt_ref, btr_ref, bti_ref, cr_ref, ci_ref,
                   k_ref, wsr_ref, wsi_ref, wor_ref, woi_ref, etr_ref, eti_ref):
    t = S5_CHUNK
    lr = jnp.minimum(lre_ref[...], LAMBDA_RE_MAX)
    li = lim_ref[...]
    dt = jnp.exp(ldt_ref[...])
    ar, ai = lr * dt, li * dt

    def power(e):
        mag = jnp.exp(e * ar)
        return mag * jnp.cos(e * ai), mag * jnp.sin(e * ai)

    per_step = lambda a: a[:, None, :]

    one = jnp.ones((1, 1), F32)
    lbr, lbi = power(one)
    nr, ni = lbr - 1.0, lbi
    den = lr * lr + li * li
    qr, qi = (nr * lr + ni * li) / den, (ni * lr - nr * li) / den
    btr, bti = btr_ref[...], bti_ref[...]
    bbr, bbi = qr * btr - qi * bti, qr * bti + qi * btr
    cr, ci = cr_ref[...], ci_ref[...]

    step = lax.broadcasted_iota(jnp.int32, (t, 1), 0).astype(F32)
    er, ei = map(per_step, power(step))
    cer, cei = cr[None] * er - ci[None] * ei, cr[None] * ei + ci[None] * er
    nt = (((1,), (1,)), ((), ()))
    dg = functools.partial(lax.dot_general, dimension_numbers=nt, precision=HIGHEST, preferred_element_type=F32)
    k_ref[...] = (dg(cer.reshape(t * SSM_GROUP, SSM_STATE), bbr)
                  - dg(cei.reshape(t * SSM_GROUP, SSM_STATE), bbi))
    er, ei = map(per_step, power(jnp.where(backward, step, (t - 1.0) - step)))
    wsr_ref[...] = er * bbr[None] - ei * bbi[None]
    wsi_ref[...] = er * bbi[None] + ei * bbr[None]
    er, ei = map(per_step, power(jnp.where(backward, t - step, step + 1.0)))
    wor_ref[...] = cr[None] * er - ci[None] * ei
    woi_ref[...] = -(cr[None] * ei + ci[None] * er)
    etr, eti = power(one * t)
    etr_ref[...] = etr
    eti_ref[...] = eti


def _s5_param_tile_body(lre, lim, ldt, btr, bti, cr, ci, k_ref, ws_ref, wo_ref, et_ref):
    ps = SSM_STATE
    for d in range(2):
        for gi in range(lre.shape[1]):
            lanes = lambda part: pl.ds((2 * part + d) * ps, ps)
            _s5_param_body(d == 1, *(r.at[d, gi] for r in (lre, lim, ldt, btr, bti, cr, ci)), k_ref.at[d, gi],
                           ws_ref.at[gi, :, :, lanes(0)], ws_ref.at[gi, :, :, lanes(1)],
                           wo_ref.at[gi, :, :, lanes(0)], wo_ref.at[gi, :, :, lanes(1)],
                           et_ref.at[gi, :, lanes(0)], et_ref.at[gi, :, lanes(1)])


def _s5_params(lam_re, lam_im, log_dt, b_re, b_im, c_re, c_im):
    g = lam_re.shape[1]
    gb = S5_TILE_GROUPS
    t, cg, ps = S5_CHUNK, SSM_GROUP, SSM_STATE
    vec = lambda a: a.reshape(2, g, 1, ps)
    ldt = jnp.broadcast_to(log_dt[:, :, None, None], (2, g, 1, ps))
    bt = lambda a: jnp.swapaxes(a, -1, -2)
    in_spec = lambda *shape: pl.BlockSpec((2, gb) + shape, lambda i: (0, i) + (0,) * len(shape))
    out_spec = lambda *shape: pl.BlockSpec((gb,) + shape, lambda i: (i,) + (0,) * len(shape))
    outs = [jax.ShapeDtypeStruct((2, g, t * cg, cg), F32)] + \
           [jax.ShapeDtypeStruct((g, t, cg, 4 * ps), F32)] * 2 + [jax.ShapeDtypeStruct((g, 1, 4 * ps), F32)]
    return pl.pallas_call(
        _s5_param_tile_body,
        grid=(g // gb,),
        in_specs=[in_spec(1, ps), in_spec(1, ps), in_spec(1, ps), in_spec(cg, ps), in_spec(cg, ps),
                  in_spec(cg, ps), in_spec(cg, ps)],
        out_specs=[in_spec(t * cg, cg), out_spec(t, cg, 4 * ps), out_spec(t, cg, 4 * ps), out_spec(1, 4 * ps)],
        out_shape=outs,
        compiler_params=_cparams(("parallel",)),
        name="s5_params",
    )(vec(lam_re), vec(lam_im), ldt, bt(b_re), bt(b_im), c_re, c_im)


def _s5_operators(lam_re, lam_im, log_dt, b_re, b_im, c_re, c_im):
    kk, ws, wo, et = _s5_params(lam_re, lam_im, log_dt, b_re, b_im, c_re, c_im)
    kk, ws, wo = (a.astype(BF16) for a in (kk, ws, wo))
    g = kk.shape[1]
    t, cg, ps = S5_CHUNK, SSM_GROUP, SSM_STATE
    kd = kk.reshape(2, g, t, cg, cg)
    s_idx = np.arange(t)[:, None]
    t_idx = np.arange(t)[None, :]

    def toeplitz(kdir, lag):
        m = jnp.where((lag >= 0)[None, :, :, None, None], kdir[:, np.mod(lag, t)], 0)
        return jnp.transpose(m, (0, 1, 4, 2, 3)).reshape(g, t * cg, t * cg)

    mf = toeplitz(kd[0], t_idx - s_idx)
    mb = toeplitz(kd[1], s_idx - t_idx)
    w1 = jnp.concatenate([mf, mb, ws.reshape(g, t * cg, 4 * ps)], axis=-1)
    w2 = jnp.swapaxes(wo.reshape(g, t * cg, 4 * ps), 1, 2)
    lam = et.reshape(g // S5_TILE_GROUPS, 1, S5_TILE_GROUPS * 4 * ps)
    return w1, w2, lam


def _chunk_lane_permutation():
    t, gl, cg = S5_CHUNK, S5_TILE_GROUPS, SSM_GROUP
    src = np.arange(t * gl * cg).reshape(t, gl, cg)
    dst = np.transpose(src, (1, 0, 2)).reshape(-1)
    perm = np.zeros((t * gl * cg, t * gl * cg), np.float32)
    perm[dst, np.arange(t * gl * cg)] = 1.0
    return perm


def _s5_body(bsz, nk, chunk_major, u_ref, perm_ref, permt_ref, w1_ref, w2_ref, lam_ref, s0_ref, y_ref, fin_ref,
             tok_scr, u_scr, ug_scr, yl_scr, x_scr, p_scr):
    t_, lt, gl = S5_CHUNK, 128, S5_TILE_GROUPS
    half = SSM_STATE
    sw = 4 * half
    l = nk * t_
    tw = t_ * SSM_GROUP
    nrows = bsz * nk

    def token_rows(t):
        if chunk_major:
            return [(k * bsz, bsz, k * t_ + t, l) for k in range(nk)]
        return [(b * nk, nk, b * l + t, t_) for b in range(bsz)]

    tok_scr[...] = u_ref[...].astype(F32)
    for t in range(t_):
        for r0, n, tok0, stride in token_rows(t):
            u_scr[r0:r0 + n, t * lt:(t + 1) * lt] = tok_scr[pl.ds(tok0, n, stride=stride), :].astype(BF16)
    ug_scr[...] = jnp.dot(u_scr[...], perm_ref[...], preferred_element_type=F32).astype(BF16)
    for gi in range(gl):
        z = jnp.dot(ug_scr[:, gi * tw:(gi + 1) * tw], w1_ref[gi], preferred_element_type=F32)
        yl_scr[:, gi * tw:(gi + 1) * tw] = z[:, :tw] + z[:, tw:2 * tw]
        if chunk_major:
            x_scr[:, gi * sw:(gi + 1) * sw] = z[:, 2 * tw:]
        else:
            x_scr[0, pl.ds(gi, nrows, stride=gl), :] = z[:, 2 * tw:2 * tw + 2 * half]
            x_scr[1, pl.ds(gi, nrows, stride=gl), :] = z[:, 2 * tw + 2 * half:]
    p_scr[...] = jnp.zeros(p_scr.shape, F32)

    srows = bsz if chunk_major else gl
    fwd_lane = lax.broadcasted_iota(jnp.int32, (srows, 2 * half), 1) < half
    bwd_lane = jnp.logical_not(fwd_lane)
    lam = lam_ref[...]

    def advance(sr, si, lr, li, kf, kb, xr_ref, xi_ref, pr_ref, pi_ref):
        for ref, val in ((pr_ref, sr), (pi_ref, si)):
            pltpu.store(ref.at[kf, :], val, mask=fwd_lane)
            pltpu.store(ref.at[kb, :], val, mask=bwd_lane)
        xr = jnp.where(fwd_lane, xr_ref[kf, :], xr_ref[kb, :])
        xi = jnp.where(fwd_lane, xi_ref[kf, :], xi_ref[kb, :])
        return lr * sr - li * si + xr, lr * si + li * sr + xi

    if chunk_major:
        def step(t, carry):
            kf = pl.ds(pl.multiple_of(t * bsz, bsz), bsz)
            kb = pl.ds(pl.multiple_of((nk - 1 - t) * bsz, bsz), bsz)
            out = []
            for gi in range(gl):
                ca, cb = pl.ds(gi * sw, 2 * half), pl.ds(gi * sw + 2 * half, 2 * half)
                out.append(advance(*carry[gi], lam[:, gi * sw:gi * sw + 2 * half],
                                   lam[:, gi * sw + 2 * half:(gi + 1) * sw], kf, kb,
                                   x_scr.at[:, ca], x_scr.at[:, cb], p_scr.at[:, ca], p_scr.at[:, cb]))
            return tuple(out)

        init = tuple((s0_ref[:, gi * sw:gi * sw + 2 * half], s0_ref[:, gi * sw + 2 * half:(gi + 1) * sw])
                     for gi in range(gl))
        fin = lax.fori_loop(0, nk, step, init)
        for gi in range(gl):
            fin_ref[:, gi * sw:gi * sw + 2 * half] = fin[gi][0]
            fin_ref[:, gi * sw + 2 * half:(gi + 1) * sw] = fin[gi][1]
    else:
        by_group = lambda row: jnp.concatenate([row[:, gi * sw:(gi + 1) * sw] for gi in range(gl)], axis=0)
        lam8 = by_group(lam)

        def step(t, carry):
            out = []
            for q in range(bsz):
                kf = pl.ds(pl.multiple_of((q * nk + t) * gl, gl), gl)
                kb = pl.ds(pl.multiple_of((q * nk + nk - 1 - t) * gl, gl), gl)
                out.append(advance(*carry[q], lam8[:, 0:2 * half], lam8[:, 2 * half:], kf, kb,
                                   x_scr.at[0], x_scr.at[1], p_scr.at[0], p_scr.at[1]))
            return tuple(out)

        s0 = [by_group(s0_ref[q:q + 1, :]) for q in range(bsz)]
        fin = lax.fori_loop(0, nk, step, tuple((a[:, 0:2 * half], a[:, 2 * half:]) for a in s0), unroll=8)
        for q in range(bsz):
            for gi in range(gl):
                fin_ref[q:q + 1, gi * sw:gi * sw + 2 * half] = fin[q][0][gi:gi + 1]
                fin_ref[q:q + 1, gi * sw + 2 * half:(gi + 1) * sw] = fin[q][1][gi:gi + 1]

    for gi in range(gl):
        if chunk_major:
            pg = p_scr[:, gi * sw:(gi + 1) * sw]
        else:
            pg = jnp.concatenate([p_scr[0, pl.ds(gi, nrows, stride=gl), :], p_scr[1, pl.ds(gi, nrows, stride=gl), :]],
                                 axis=1)
        y = yl_scr[:, gi * tw:(gi + 1) * tw] + jnp.dot(pg.astype(BF16), w2_ref[gi], preferred_element_type=F32)
        u_scr[:, gi * tw:(gi + 1) * tw] = y.astype(BF16)
    yt = jnp.dot(u_scr[...], permt_ref[...], preferred_element_type=F32)
    for t in range(t_):
        for r0, n, tok0, stride in token_rows(t):
            tok_scr[pl.ds(tok0, n, stride=stride), :] = yt[r0:r0 + n, t * lt:(t + 1) * lt]
    y_ref[...] = tok_scr[...].astype(BF16)


def _s5(proj, col0, bsz, l, ops, s0):
    w1, w2, lam = ops
    g = w1.shape[0]
    t, ps, gl, lt = S5_CHUNK, SSM_STATE, S5_TILE_GROUPS, 128
    nb = g // gl
    nk = l // t
    r = bsz * l
    nrows = bsz * nk
    sw = 4 * ps
    cw = t * lt
    tw = t * SSM_GROUP
    chunk_major = bsz % 8 == 0
    state_shape = (nrows, gl * sw) if chunk_major else (2, nrows * gl, sw // 2)
    if s0 is None:
        s0p = jnp.zeros((bsz, g * sw), F32)
    else:
        s0_re, s0_im = s0
        s0p = jnp.concatenate([s0_re[:, 0], s0_re[:, 1], s0_im[:, 0], s0_im[:, 1]], axis=-1).reshape(bsz, g * sw)
    perm = _chunk_lane_permutation()
    const = lambda shape: _resident(shape, lambda i: (0,) * len(shape))
    y, fin = pl.pallas_call(
        functools.partial(_s5_body, bsz, nk, chunk_major),
        grid=(nb,),
        in_specs=[pl.BlockSpec((r, lt), lambda i: (0, col0 // lt + i), pipeline_mode=pl.Buffered(1)),
                  const((cw, cw)), const((cw, cw)),
                  pl.BlockSpec((gl, tw, 3 * tw), lambda i: (i, 0, 0)), pl.BlockSpec((gl, tw, tw), lambda i: (i, 0, 0)),
                  pl.BlockSpec((None, 1, gl * sw), lambda i: (i, 0, 0)),
                  pl.BlockSpec((bsz, gl * sw), lambda i: (0, i))],
        out_specs=[pl.BlockSpec((r, lt), lambda i: (0, i)), pl.BlockSpec((bsz, gl * sw), lambda i: (0, i))],
        out_shape=[jax.ShapeDtypeStruct((r, g * SSM_GROUP), BF16), jax.ShapeDtypeStruct((bsz, g * sw), F32)],
        scratch_shapes=[pltpu.VMEM((r, lt), F32), pltpu.VMEM((nrows, cw), BF16), pltpu.VMEM((nrows, cw), BF16),
                        pltpu.VMEM((nrows, cw), F32)] + [pltpu.VMEM(state_shape, F32)] * 2,
        compiler_params=_cparams(("parallel",)),
        name=f"s5_scan_{nk}",
    )(proj, jnp.asarray(perm, BF16), jnp.asarray(perm.T, BF16), w1, w2, lam, s0p)
    fin = jnp.transpose(fin.reshape(bsz, g, 2, 2, ps), (2, 0, 3, 1, 4))
    return y, fin[0], fin[1]


def _merge_body(yh_ref, ys_ref, u_ref, gh_ref, gs_ref, x_ref, d_ref, gw_ref, gbias_ref, wbh_ref, wbs_ref, wo_ref,
                g1_ref, n2_ref, sc2_ref, sh2_ref, x1_ref, h2_ref):
    y = ys_ref[...].astype(F32) + u_ref[...].astype(F32) * d_ref[...]
    g = jax.nn.gelu(y)
    gl = jnp.dot(g.astype(BF16), gw_ref[...], preferred_element_type=F32) + gbias_ref[...]
    ys = (g * jax.nn.sigmoid(gl)).astype(BF16)
    bh = jnp.dot(yh_ref[...], wbh_ref[...], preferred_element_type=F32)
    bs = jnp.dot(ys, wbs_ref[...], preferred_element_type=F32)
    merged = jax.nn.sigmoid(gh_ref[...].astype(F32)) * bh + jax.nn.sigmoid(gs_ref[...].astype(F32)) * bs
    x1 = x_ref[...] + g1_ref[...] * jnp.dot(merged.astype(BF16), wo_ref[...], preferred_element_type=F32)
    x1_ref[...] = x1
    h2_ref[...] = _mod_rmsnorm(x1, n2_ref[...], sc2_ref[...], sh2_ref[...]).astype(BF16)


def _merge(yh, ys, proj, x, mod4, cond_of_tile, ssm_d, glu_w, glu_b, wbh, wbs, wo, n2g, c_h, c_s, tm=256):
    r, d = x.shape
    row_blk = lambda width, k: pl.BlockSpec((tm, width), lambda i: (i, k))
    mod_spec = lambda chunk: pl.BlockSpec((None, None, 1, d), lambda i: (cond_of_tile(i, tm), chunk, 0, 0))
    const = lambda shape: _resident(shape, lambda i: (0,) * len(shape))
    u_col = (3 * c_h) // c_s
    gate0 = (3 * c_h + c_s) // d
    return pl.pallas_call(
        _merge_body,
        grid=(r // tm,),
        in_specs=[row_blk(c_h, 0), row_blk(c_s, 0), row_blk(c_s, u_col), row_blk(d, gate0), row_blk(d, gate0 + 1),
                  row_blk(d, 0), const((1, c_s)), const((c_s, c_s)), const((1, c_s)), const((c_h, d)),
                  const((c_s, d)), const((d, d)), mod_spec(2), const((1, d)), mod_spec(4), mod_spec(3)],
        out_specs=[row_blk(d, 0), row_blk(d, 0)],
        out_shape=[jax.ShapeDtypeStruct((r, d), F32), jax.ShapeDtypeStruct((r, d), BF16)],
        compiler_params=_cparams(("parallel",)),
        name="merge",
    )(yh, ys, proj, proj, proj, x, ssm_d.reshape(1, c_s), glu_w, glu_b.reshape(1, c_s), wbh, wbs, wo,
      mod4, n2g.reshape(1, d), mod4, mod4)


_FFN_RC = 512


def _ffn_up_body(rows, cols, h_ref, wa_ref, wb_ref, cwa_ref, cwb_ref, cba_ref, cbb_ref, o_ref):
    tm, tn = o_ref.shape
    rc = _FFN_RC
    n = tm // rc
    col = lax.rem(lax.broadcasted_iota(jnp.int32, (rc, tn), 0), cols)
    not_first = col != 0
    not_last = col != cols - 1
    tdt = BF16 if rows > 1 else F32
    zero = jnp.zeros((cols, tn), tdt)

    def taps(hc, w_ref, cw_ref, cb_ref):
        a = jnp.dot(hc, w_ref[...], preferred_element_type=F32)
        am = jnp.where(not_first, pltpu.roll(a, 1, 0), 0.0).astype(tdt)
        ap = jnp.where(not_last, pltpu.roll(a, rc - 1, 0), 0.0).astype(tdt)
        a = a.astype(tdt)
        cw = cw_ref[...].astype(tdt)
        tap = lambda i: am * cw[3 * i:3 * i + 1] + a * cw[3 * i + 1:3 * i + 2] + ap * cw[3 * i + 2:3 * i + 3]
        mid = tap(1) + cb_ref[...].astype(tdt)
        return (tap(0), mid, tap(2)) if rows > 1 else (None, mid, None)

    def finish(prev, cur, nxt):
        t0, mid, t2 = cur
        if rows == 1:
            return mid
        above = jnp.concatenate([prev[0][rc - cols:] if prev is not None else zero, t0[:rc - cols]], axis=0)
        below = jnp.concatenate([t2[cols:], nxt[2][:cols] if nxt is not None else zero], axis=0)
        return mid + above + below

    chunks = []
    for i in range(n + 1):
        if i < n:
            hc = h_ref[i * rc:(i + 1) * rc, :]
            chunks.append((taps(hc, wa_ref, cwa_ref, cba_ref), taps(hc, wb_ref, cwb_ref, cbb_ref)))
        if i >= 1:
            j = i - 1
            pick = lambda k, side: chunks[k][side] if 0 <= k < n else None
            a = finish(pick(j - 1, 0), pick(j, 0), pick(j + 1, 0))
            b = finish(pick(j - 1, 1), pick(j, 1), pick(j + 1, 1))
            o_ref[j * rc:(j + 1) * rc, :] = (jax.nn.gelu(a.astype(F32)) * b.astype(F32)).astype(BF16)


def _ffn_up(h2, w_up, conv_w, conv_b, rows, cols, tn=256):
    r, d = h2.shape
    f = w_up.shape[1] // 2
    tm = rows * cols if rows > 1 else 4096
    assert tm % _FFN_RC == 0 and _FFN_RC % cols == 0 and cols % 8 == 0
    nt = f // tn
    cw = conv_w.reshape(9, 2 * f)
    cb = conv_b.reshape(1, 2 * f)
    return pl.pallas_call(
        functools.partial(_ffn_up_body, rows, cols),
        grid=(r // tm, nt),
        in_specs=[pl.BlockSpec((tm, d), lambda i, j: (i, 0), pipeline_mode=pl.Buffered(1)),
                  pl.BlockSpec((d, tn), lambda i, j: (0, j)), pl.BlockSpec((d, tn), lambda i, j: (0, nt + j)),
                  pl.BlockSpec((9, tn), lambda i, j: (0, j)), pl.BlockSpec((9, tn), lambda i, j: (0, nt + j)),
                  pl.BlockSpec((1, tn), lambda i, j: (0, j)), pl.BlockSpec((1, tn), lambda i, j: (0, nt + j))],
        out_specs=pl.BlockSpec((tm, tn), lambda i, j: (i, j)),
        out_shape=jax.ShapeDtypeStruct((r, f), BF16),
        compiler_params=_cparams(("parallel", "arbitrary")),
        name=f"ffn_up_{rows}x{cols}",
    )(h2, w_up, w_up, cw, cw, cb, cb)


def _ffn_down_body(a_ref, w_ref, x_ref, g2_ref, fg_ref, o_ref):
    x2 = x_ref[...] + g2_ref[...] * jnp.dot(a_ref[...], w_ref[...], preferred_element_type=F32)
    ms = jnp.mean(x2 * x2, axis=-1, keepdims=True)
    o_ref[...] = x2 * lax.rsqrt(ms + EPS) * fg_ref[...]


def _ffn_down(act, w, x1, mod4, cond_of_tile, final_g, tm=256):
    r, f = act.shape
    d = w.shape[1]
    return pl.pallas_call(
        _ffn_down_body,
        grid=(r // tm,),
        in_specs=[pl.BlockSpec((tm, f), lambda i: (i, 0)),
                  _resident((f, d), lambda i: (0, 0)),
                  pl.BlockSpec((tm, d), lambda i: (i, 0)),
                  pl.BlockSpec((None, None, 1, d), lambda i: (cond_of_tile(i, tm), 5, 0, 0)),
                  _resident((1, d), lambda i: (0, 0))],
        out_specs=pl.BlockSpec((tm, d), lambda i: (i, 0)),
        out_shape=jax.ShapeDtypeStruct((r, d), F32),
        compiler_params=_cparams(("parallel",)),
        name="ffn_down",
    )(act, w, x1, mod4, final_g.reshape(1, d))


def _segment(x, cond_base, per_batch_cond, s0, rows, cols, hy_block, hy_nseq, mod4, p, final_g):
    bsz, l, d = x.shape
    xf = x.reshape(bsz * l, d)
    c_h = p["wbh"].shape[0]
    c_s = p["wbs"].shape[0]

    def cond_of_tile(i, tm):
        return cond_base + (i * tm) // l if per_batch_cond else cond_base

    proj = _in_proj(xf, mod4, cond_of_tile, p["norm1_g"], p["w_in"], p["b_in"])
    yh = _hyena(proj, l, hy_block, hy_nseq, p["hy_short_w"], p["hy_short_b"], p["hy_bias"], p["ghat"][l], c_h,
                ct=128 if l // hy_block > 1 else 256)
    ys, fin_re, fin_im = _s5(proj, 3 * c_h, bsz, l, p["s5_ops"], s0)
    x1, h2 = _merge(yh, ys, proj, xf, mod4, cond_of_tile, p["ssm_d"], p["glu_w"], p["glu_b"],
                    p["wbh"], p["wbs"], p["wo"], p["norm2_g"], c_h, c_s)
    act = _ffn_up(h2, p["ffn_up"], p["ffn_conv_w"], p["ffn_conv_b"], rows, cols)
    y = _ffn_down(act, p["ffn_down"], x1, mod4, cond_of_tile, final_g)
    return y.reshape(bsz, l, d), fin_re, fin_im


def kernel(x_prompt, x_sample, state_ssm_re, state_ssm_im, c, c_ctx, ada_w, ada_b, norm1_g, norm2_g, final_g, w_in, b_in, hy_short_w, hy_short_b, hy_f_w1, hy_f_b1, hy_f_w2, hy_f_b2, hy_f_w3, hy_f_b3, hy_f_w4, hy_f_freq, hy_bias, w_branch_h, ssm_lambda_re, ssm_lambda_im, ssm_log_dt, ssm_b_re, ssm_b_im, ssm_c_re, ssm_c_im, ssm_d, ssm_glu_w, ssm_glu_b, w_branch_s, w_out, ffn_up, ffn_conv_w, ffn_conv_b, ffn_down):
    depth = ada_w.shape[0]
    assert depth == 1, "the final norm is fused into the (single) layer's last kernel"
    d = x_prompt.shape[-1]
    ctx_len = x_prompt.shape[1]
    lat_len = x_sample.shape[1]
    dec_b = x_sample.shape[0]
    c_h = w_branch_h.shape[1]
    assert 1 + dec_b <= 8
    cond = jnp.zeros((8, d), F32).at[0].set(c_ctx).at[1:1 + dec_b].set(c)
    lay = 0
    mod4 = _ada(cond, ada_w[lay], ada_b[lay]).reshape(8, N_MOD, 1, d)
    hy_blocks = {ctx_len: 256, lat_len: 512}
    ghat = {l: _hyena_filters(l, hy_blocks[l], hy_f_w1[lay], hy_f_b1[lay], hy_f_w2[lay], hy_f_b2[lay],
                              hy_f_w3[lay], hy_f_b3[lay], hy_f_w4[lay], hy_f_freq[lay], c_h)
            for l in (ctx_len, lat_len)}
    s5_ops = _s5_operators(ssm_lambda_re[lay], ssm_lambda_im[lay], ssm_log_dt[lay],
                           ssm_b_re[lay], ssm_b_im[lay], ssm_c_re[lay], ssm_c_im[lay])
    p = dict(norm1_g=norm1_g[lay], norm2_g=norm2_g[lay], w_in=w_in[lay].astype(BF16), b_in=b_in[lay],
             hy_short_w=hy_short_w[lay], hy_short_b=hy_short_b[lay], hy_bias=hy_bias[lay], ghat=ghat,
             s5_ops=s5_ops, ssm_d=ssm_d[lay], glu_w=ssm_glu_w[lay].astype(BF16),
             glu_b=ssm_glu_b[lay], wbh=w_branch_h[lay].astype(BF16), wbs=w_branch_s[lay].astype(BF16),
             wo=w_out[lay].astype(BF16), ffn_up=ffn_up[lay].astype(BF16), ffn_conv_w=ffn_conv_w[lay],
             ffn_conv_b=ffn_conv_b[lay], ffn_down=ffn_down[lay].astype(BF16))
    y_prompt, st_re, st_im = _segment(x_prompt, 0, False, None, 1, ctx_len, hy_blocks[ctx_len], 16, mod4, p, final_g)
    s0 = (state_ssm_re[:, lay].astype(F32), state_ssm_im[:, lay].astype(F32))
    y_sample, _, _ = _segment(x_sample, 1, True, s0, lat_len // GRID_W, GRID_W, hy_blocks[lat_len], 1, mod4, p, final_g)
    return y_prompt, y_sample, st_re[:, None], st_im[:, None]
```

```python
import functools
import math

import jax
import jax.numpy as jnp
import numpy as np
from jax import lax
from jax.experimental import pallas as pl
from jax.experimental.pallas import tpu as pltpu

F32 = jnp.float32
BF16 = jnp.bfloat16
HIGHEST = lax.Precision.HIGHEST

GRID_W = 64
EPS = 1e-6
SSM_GROUP = 16
SSM_STATE = 64
FILTER_EMB = 33
FILTER_HIDDEN = 64
HYENA_ORDER = 2
DECAY_FAST = 0.3
DECAY_SLOW = 1.5
DECAY_TARGET = 1e-2
LAMBDA_RE_MAX = -1e-4

S5_CHUNK = 16
S5_TILE_GROUPS = 8
V7X_VMEM_LIMIT = 56 * 2**20
N_MOD = 6


def _cparams(sem):
    return pltpu.CompilerParams(dimension_semantics=sem, vmem_limit_bytes=V7X_VMEM_LIMIT)


def _resident(shape, index_map):
    return pl.BlockSpec(shape, index_map, pipeline_mode=pl.Buffered(1))


def _mod_rmsnorm(x, g, sc, sh):
    ms = jnp.mean(x * x, axis=-1, keepdims=True)
    return x * lax.rsqrt(ms + EPS) * g * (1.0 + sc) + sh


def _ada_body(c_ref, w_ref, b_ref, o_ref):
    c = c_ref[...]
    s = c * jax.nn.sigmoid(c)
    o_ref[...] = _dot_3pass(s, *_split_bf16(w_ref[...])) + b_ref[...]


def _ada(cond, w, b):
    d, n = w.shape
    tn = 1024
    return pl.pallas_call(
        _ada_body,
        grid=(n // tn,),
        in_specs=[pl.BlockSpec((8, d), lambda j: (0, 0)),
                  pl.BlockSpec((d, tn), lambda j: (0, j)),
                  pl.BlockSpec((1, tn), lambda j: (0, j))],
        out_specs=pl.BlockSpec((8, tn), lambda j: (0, j)),
        out_shape=jax.ShapeDtypeStruct((8, n), F32),
        compiler_params=_cparams(("parallel",)),
        name="ada",
    )(cond, w, b.reshape(1, n))


def _inproj_body(x_ref, g_ref, sc_ref, sh_ref, w_ref, b_ref, o_ref, h_scr):
    @pl.when(pl.program_id(1) == 0)
    def _():
        h_scr[...] = _mod_rmsnorm(x_ref[...], g_ref[...], sc_ref[...], sh_ref[...]).astype(BF16)

    o_ref[...] = (jnp.dot(h_scr[...], w_ref[...], preferred_element_type=F32) + b_ref[...]).astype(BF16)


def _in_proj(x, mod4, cond_of_tile, g, w, b, tm=1024, tn=1024):
    r, d = x.shape
    n = w.shape[1]
    mod_spec = lambda chunk: pl.BlockSpec((None, None, 1, d), lambda i, j: (cond_of_tile(i, tm), chunk, 0, 0))
    return pl.pallas_call(
        _inproj_body,
        grid=(r // tm, n // tn),
        in_specs=[pl.BlockSpec((tm, d), lambda i, j: (i, 0)),
                  pl.BlockSpec((1, d), lambda i, j: (0, 0)),
                  mod_spec(1), mod_spec(0),
                  pl.BlockSpec((d, tn), lambda i, j: (0, j)),
                  pl.BlockSpec((1, tn), lambda i, j: (0, j))],
        out_specs=pl.BlockSpec((tm, tn), lambda i, j: (i, j)),
        out_shape=jax.ShapeDtypeStruct((r, n), BF16),
        scratch_shapes=[pltpu.VMEM((tm, d), BF16)],
        compiler_params=_cparams(("parallel", "arbitrary")),
        name="in_proj",
    )(x, g.reshape(1, d), mod4, mod4, w, b.reshape(1, n))


def _shifted_dft_tables(p):
    theta = np.pi * (2.0 * np.arange(p) + 1.0) / (2.0 * p)
    n = np.arange(p)
    a = theta[:, None] * n[None, :]
    fwd = np.concatenate([np.cos(a), -np.sin(a)], axis=0)
    m = np.arange(2 * p) - p
    am = theta[:, None] * m[None, :]
    filt = np.concatenate([np.cos(am), -np.sin(am)], axis=0)
    filt[:, 0] = 0.0
    inv = np.concatenate([np.cos(a).T, -np.sin(a).T], axis=1) / p
    return fwd, filt, inv


_FILTER_RC = 1024


def _filter_positions(l):
    pos = np.abs(np.arange(2 * l) - l).astype(np.float64)
    t = pos / (l - 1)
    bands = (FILTER_EMB - 1) // 2
    fr = np.linspace(1e-4, bands - 1, bands)
    ang = (2.0 * math.pi / l) * pos[:, None] * fr[None, :]
    z = np.concatenate([t[:, None], np.cos(ang), -np.sin(ang)], axis=-1)
    zp = np.zeros((2 * l, 128), np.float32)
    zp[:, :FILTER_EMB] = z
    return zp


def _filter_hidden_body(z_ref, w1_ref, b1_ref, w2_ref, b2_ref, w3_ref, b3_ref, fq_ref, o_ref):
    fq = fq_ref[...]
    dot = functools.partial(jnp.dot, precision=HIGHEST, preferred_element_type=F32)
    h = jnp.sin(fq * (dot(z_ref[...], w1_ref[...]) + b1_ref[...]))
    h = jnp.sin(fq * (dot(h, w2_ref[...]) + b2_ref[...]))
    o_ref[...] = jnp.sin(fq * (dot(h, w3_ref[...]) + b3_ref[...]))


def _filter_hidden(zp, w1p, b1, w2, b2, w3, b3, freq):
    rows, kin = zp.shape
    rc = min(rows, _FILTER_RC)
    hd = w2.shape[0]
    full = lambda shape: pl.BlockSpec(shape, lambda i: (0,) * len(shape))
    row = lambda a: a.reshape(1, -1)
    return pl.pallas_call(
        _filter_hidden_body,
        grid=(rows // rc,),
        in_specs=[pl.BlockSpec((rc, kin), lambda i: (i, 0)), full((kin, hd)), full((1, hd)), full((hd, hd)),
                  full((1, hd)), full((hd, hd)), full((1, hd)), full((1, hd))],
        out_specs=pl.BlockSpec((rc, hd), lambda i: (i, 0)),
        out_shape=jax.ShapeDtypeStruct((rows, hd), F32),
        compiler_params=_cparams(("parallel",)),
        name=f"hyena_filter_mlp_{rows}",
    )(zp, w1p, row(b1), w2, row(b2), w3, row(b3), row(freq))


def _split_bf16(x):
    hi = x.astype(BF16)
    return hi, (x - hi.astype(F32)).astype(BF16)


def _dot_3pass(x, w_hi, w_lo):
    x_hi, x_lo = _split_bf16(x)
    d = functools.partial(jnp.dot, preferred_element_type=F32)
    return d(x_hi, w_hi) + (d(x_lo, w_hi) + d(x_hi, w_lo))


def _filter_body(l, p, h_ref, w4f_ref, w4b_ref, dl_ref, mg_ref, o_ref, kk_scr, mg_scr):
    mg_scr[...] = mg_ref[...].astype(BF16)
    rc = min(l, _FILTER_RC)
    ct = kk_scr.shape[1]
    nhalf = l // rc

    def taps(w_ref, first):
        w_hi, w_lo = _split_bf16(w_ref[...])

        def body(i, asum):
            r0 = pl.multiple_of(i * rc, rc)
            q = r0 + lax.broadcasted_iota(jnp.int32, (rc, ct), 0)
            t = jnp.abs(q - l).astype(F32) * (1.0 / (l - 1))
            k = _dot_3pass(h_ref[pl.ds(r0, rc), :], w_hi, w_lo) * jnp.exp(-t * dl_ref[...])
            k = jnp.where(q == 0, 0.0, k)
            kk_scr[pl.ds(r0, rc), :] = k.astype(BF16)
            return asum + jnp.sum(jnp.abs(k), axis=0, keepdims=True)

        return lambda asum: lax.fori_loop(first, first + nhalf, body, asum)

    asum = taps(w4f_ref, nhalf)(taps(w4b_ref, 0)(jnp.zeros((1, ct), F32)))
    scale = 1.0 / asum
    nwin = 2 * (l // p) - 1

    def win(w, _):
        seg = kk_scr[pl.ds(pl.multiple_of(w * p, p), 2 * p), :]
        o_ref[w] = jnp.dot(mg_scr[...], seg, preferred_element_type=F32) * scale
        return 0

    lax.fori_loop(0, nwin, win, 0)


def _hyena_filters(l, p, w1, b1, w2, b2, w3, b3, w4, freq, c, ct=256):
    nwin = 2 * (l // p) - 1
    nct = c // ct
    zp = _filter_positions(l)
    zp2 = jnp.asarray(np.concatenate([zp[:l], zp[l:]], axis=1))
    w1p = jnp.pad(w1, ((0, 128 - FILTER_EMB), (0, 0)))
    twice = lambda w: jnp.concatenate([jnp.pad(w, ((0, 0), (0, w.shape[1]))), jnp.pad(w, ((0, 0), (w.shape[1], 0)))], axis=0)
    both = lambda v: jnp.concatenate([v, v])
    hid2 = _filter_hidden(zp2, twice(w1p), both(b1), twice(w2), both(b2), twice(w3), both(b3), both(freq))
    hid = jnp.concatenate([hid2[:, :FILTER_HIDDEN], hid2[:, FILTER_HIDDEN:]], axis=0)
    deltas = np.abs(np.linspace(math.log(DECAY_TARGET) / DECAY_SLOW, math.log(DECAY_TARGET) / DECAY_FAST, c))
    mg = jnp.asarray(_shifted_dft_tables(p)[1], F32)
    full = lambda shape: pl.BlockSpec(shape, lambda o, j: (0,) * len(shape))
    return pl.pallas_call(
        functools.partial(_filter_body, l, p),
        grid=(HYENA_ORDER, nct),
        in_specs=[full((2 * l, FILTER_HIDDEN)),
                  pl.BlockSpec((FILTER_HIDDEN, ct), lambda o, j: (0, o * 2 * nct + j)),
                  pl.BlockSpec((FILTER_HIDDEN, ct), lambda o, j: (0, o * 2 * nct + nct + j)),
                  pl.BlockSpec((1, ct), lambda o, j: (0, j)),
                  full((2 * p, 2 * p))],
        out_specs=pl.BlockSpec((None, nwin, 2 * p, ct), lambda o, j: (o, 0, 0, j)),
        out_shape=jax.ShapeDtypeStruct((HYENA_ORDER, nwin, 2 * p, c), F32),
        scratch_shapes=[pltpu.VMEM((2 * l, ct), BF16), pltpu.VMEM((2 * p, 2 * p), BF16)],
        compiler_params=_cparams(("parallel", "parallel")),
        name=f"hyena_filter_{l}",
    )(hid, w4, w4, jnp.asarray(deltas, F32).reshape(1, c), mg)


_HY_PASS_VREGS = 8


def _hyena_body(nseq, l, p, x1_ref, x2_ref, v_ref, w1_ref, w2_ref, wv_ref, b1_ref, b2_ref, bv_ref, hb_ref,
                g_ref, fz_ref, fi_ref, o_ref, z_scr, c_scr, zh_scr, yh_scr, z1_scr, fz_scr, fi_scr):
    fz_scr[...] = fz_ref[...].astype(BF16)
    fi_scr[...] = fi_ref[...].astype(BF16)
    o = pl.program_id(1)
    s = pl.program_id(2)
    rows = nseq * l
    nb = l // p
    ct = o_ref.shape[1]
    hrc = _HY_PASS_VREGS * 8 * 128 // ct
    pair = yh_scr.shape[1] // ct
    pos = lax.rem(lax.broadcasted_iota(jnp.int32, (rows, ct), 0), l)

    def short_conv(u_ref, w_ref, b_ref):
        u = u_ref[...].astype(F32)
        um = jnp.where(pos == 0, 0.0, pltpu.roll(u, 1, 0))
        up = jnp.where(pos == l - 1, 0.0, pltpu.roll(u, rows - 1, 0))
        w = w_ref[...]
        return um * w[0:1] + u * w[1:2] + up * w[2:3] + b_ref[...]

    def long_conv():
        for q in range(nseq):
            base = q * l

            def block(k):
                return pl.ds(pl.multiple_of(base + k * p, p), p)

            def fwd(jj, _):
                zb = jnp.concatenate([z_scr[block(jj * pair + k), :] for k in range(pair)], axis=1).astype(BF16)
                zh = jnp.dot(fz_scr[...], zb, preferred_element_type=F32)
                for k in range(pair):
                    zh_scr[jj * pair + k] = zh[:, k * ct:(k + 1) * ct]
                return 0

            lax.fori_loop(0, nb // pair, fwd, 0)

            def out_block(ii, _):
                for k in range(pair):
                    i = ii * pair + k
                    for rc in range(p // hrc):
                        re = pl.ds(rc * hrc, hrc)
                        im = pl.ds(p + rc * hrc, hrc)

                        def acc(j, carry):
                            ar, ai = carry
                            w = i - j + (nb - 1)
                            gr, gi = g_ref[w, re, :], g_ref[w, im, :]
                            zr, zi = zh_scr[j, re, :], zh_scr[j, im, :]
                            return ar + gr * zr - gi * zi, ai + gr * zi + gi * zr

                        zero = jnp.zeros((hrc, ct), F32)
                        ar, ai = lax.fori_loop(0, nb, acc, (zero, zero), unroll=True)
                        yh_scr[re, k * ct:(k + 1) * ct] = ar
                        yh_scr[im, k * ct:(k + 1) * ct] = ai
                c = jnp.dot(fi_scr[...], yh_scr[...].astype(BF16), preferred_element_type=F32)
                for k in range(pair):
                    c_scr[block(ii * pair + k), :] = c[:, k * ct:(k + 1) * ct]
                return 0

            lax.fori_loop(0, nb // pair, out_block, 0)

    hb = hb_ref[...]

    @pl.when(o == 0)
    def _():
        v = short_conv(v_ref, wv_ref, bv_ref)
        z_scr[...] = v
        long_conv()
        x1 = short_conv(x1_ref, w1_ref, b1_ref)
        z1_scr[s] = x1 * (c_scr[...] + hb[0:1] * v)

    @pl.when(o == 1)
    def _():
        z1 = z1_scr[s]
        z_scr[...] = z1
        long_conv()
        x2 = short_conv(x2_ref, w2_ref, b2_ref)
        o_ref[...] = (x2 * (c_scr[...] + hb[1:2] * z1)).astype(BF16)


def _hyena(proj, l, p, nseq, short_w, short_b, hy_bias, ghat, c, ct=128):
    r = proj.shape[0]
    rows = nseq * l
    nsb = r // rows
    nct = c // ct
    nwin = ghat.shape[1]
    fwd, _, inv = _shifted_dft_tables(p)
    pair = 2 if (ct < 256 and (l // p) % 2 == 0) else 1
    fz = jnp.asarray(fwd, F32)
    fi = jnp.asarray(inv, F32)
    sb = short_b.reshape(1, 3 * c)
    col = lambda k: (lambda j, o, s: (s, k * nct + j))
    wcol = lambda k: (lambda j, o, s: (0, k * nct + j))
    return pl.pallas_call(
        functools.partial(_hyena_body, nseq, l, p),
        grid=(nct, HYENA_ORDER, nsb),
        in_specs=[pl.BlockSpec((rows, ct), col(0)), pl.BlockSpec((rows, ct), col(1)), pl.BlockSpec((rows, ct), col(2)),
                  pl.BlockSpec((3, ct), wcol(0)), pl.BlockSpec((3, ct), wcol(1)), pl.BlockSpec((3, ct), wcol(2)),
                  pl.BlockSpec((1, ct), wcol(0)), pl.BlockSpec((1, ct), wcol(1)), pl.BlockSpec((1, ct), wcol(2)),
                  pl.BlockSpec((HYENA_ORDER, ct), lambda j, o, s: (0, j)),
                  pl.BlockSpec((None, nwin, 2 * p, ct), lambda j, o, s: (o, 0, 0, j)),
                  pl.BlockSpec((2 * p, p), lambda j, o, s: (0, 0)),
                  pl.BlockSpec((p, 2 * p), lambda j, o, s: (0, 0))],
        out_specs=pl.BlockSpec((rows, ct), lambda j, o, s: (s * o, j)),
        out_shape=jax.ShapeDtypeStruct((r, c), BF16),
        scratch_shapes=[pltpu.VMEM((rows, ct), F32), pltpu.VMEM((rows, ct), F32),
                        pltpu.VMEM((l // p, 2 * p, ct), F32), pltpu.VMEM((2 * p, pair * ct), F32),
                        pltpu.VMEM((nsb, rows, ct), F32), pltpu.VMEM((2 * p, p), BF16), pltpu.VMEM((p, 2 * p), BF16)],
        compiler_params=_cparams(("parallel", "arbitrary", "arbitrary")),
        name=f"hyena_{l}",
    )(proj, proj, proj, short_w, short_w, short_w, sb, sb, sb, hy_bias, ghat, fz, fi)


def _s5_param_body(backward, lre_ref, lim_ref, ldt_ref, btr_ref, bti_ref, cr_ref, ci_ref,
                   k_ref, wsr_ref, wsi_ref, wor_ref, woi_ref, etr_ref, eti_ref):
    t = S5_CHUNK
    lr = jnp.minimum(lre_ref[...], LAMBDA_RE_MAX)
    li = lim_ref[...]
    dt = jnp.exp(ldt_ref[...])
    ar, ai = lr * dt, li * dt

    def power(e):
        mag = jnp.exp(e * ar)
        return mag * jnp.cos(e * ai), mag * jnp.sin(e * ai)

    per_step = lambda a: a[:, None, :]

    one = jnp.ones((1, 1), F32)
    lbr, lbi = power(one)
    nr, ni = lbr - 1.0, lbi
    den = lr * lr + li * li
    qr, qi = (nr * lr + ni * li) / den, (ni * lr - nr * li) / den
    btr, bti = btr_ref[...], bti_ref[...]
    bbr, bbi = qr * btr - qi * bti, qr * bti + qi * btr
    cr, ci = cr_ref[...], ci_ref[...]

    step = lax.broadcasted_iota(jnp.int32, (t, 1), 0).astype(F32)
    er, ei = map(per_step, power(step))
    cer, cei = cr[None] * er - ci[None] * ei, cr[None] * ei + ci[None] * er
    nt = (((1,), (1,)), ((), ()))
    dg = functools.partial(lax.dot_general, dimension_numbers=nt, precision=HIGHEST, preferred_element_type=F32)
    k_ref[...] = (dg(cer.reshape(t * SSM_GROUP, SSM_STATE), bbr)
                  - dg(cei.reshape(t * SSM_GROUP, SSM_STATE), bbi))
    er, ei = map(per_step, power(jnp.where(backward, step, (t - 1.0) - step)))
    wsr_ref[...] = er * bbr[None] - ei * bbi[None]
    wsi_ref[...] = er * bbi[None] + ei * bbr[None]
    er, ei = map(per_step, power(jnp.where(backward, t - step, step + 1.0)))
    wor_ref[...] = cr[None] * er - ci[None] * ei
    woi_ref[...] = -(cr[None] * ei + ci[None] * er)
    etr, eti = power(one * t)
    etr_ref[...] = etr
    eti_ref[...] = eti


def _s5_param_tile_body(lre, lim, ldt, btr, bti, cr, ci, k_ref, ws_ref, wo_ref, et_ref):
    ps = SSM_STATE
    for d in range(2):
        for gi in range(lre.shape[1]):
            lanes = lambda part: pl.ds((2 * part + d) * ps, ps)
            _s5_param_body(d == 1, *(r.at[d, gi] for r in (lre, lim, ldt, btr, bti, cr, ci)), k_ref.at[d, gi],
                           ws_ref.at[gi, :, :, lanes(0)], ws_ref.at[gi, :, :, lanes(1)],
                           wo_ref.at[gi, :, :, lanes(0)], wo_ref.at[gi, :, :, lanes(1)],
                           et_ref.at[gi, :, lanes(0)], et_ref.at[gi, :, lanes(1)])


def _s5_params(lam_re, lam_im, log_dt, b_re, b_im, c_re, c_im):
    g = lam_re.shape[1]
    gb = S5_TILE_GROUPS
    t, cg, ps = S5_CHUNK, SSM_GROUP, SSM_STATE
    vec = lambda a: a.reshape(2, g, 1, ps)
    ldt = jnp.broadcast_to(log_dt[:, :, None, None], (2, g, 1, ps))
    bt = lambda a: jnp.swapaxes(a, -1, -2)
    in_spec = lambda *shape: pl.BlockSpec((2, gb) + shape, lambda i: (0, i) + (0,) * len(shape))
    out_spec = lambda *shape: pl.BlockSpec((gb,) + shape, lambda i: (i,) + (0,) * len(shape))
    outs = [jax.ShapeDtypeStruct((2, g, t * cg, cg), F32)] + \
           [jax.ShapeDtypeStruct((g, t, cg, 4 * ps), F32)] * 2 + [jax.ShapeDtypeStruct((g, 1, 4 * ps), F32)]
    return pl.pallas_call(
        _s5_param_tile_body,
        grid=(g // gb,),
        in_specs=[in_spec(1, ps), in_spec(1, ps), in_spec(1, ps), in_spec(cg, ps), in_spec(cg, ps),
                  in_spec(cg, ps), in_spec(cg, ps)],
        out_specs=[in_spec(t * cg, cg), out_spec(t, cg, 4 * ps), out_spec(t, cg, 4 * ps), out_spec(1, 4 * ps)],
        out_shape=outs,
        compiler_params=_cparams(("parallel",)),
        name="s5_params",
    )(vec(lam_re), vec(lam_im), ldt, bt(b_re), bt(b_im), c_re, c_im)


def _s5_operators(lam_re, lam_im, log_dt, b_re, b_im, c_re, c_im):
    kk, ws, wo, et = _s5_params(lam_re, lam_im, log_dt, b_re, b_im, c_re, c_im)
    kk, ws, wo = (a.astype(BF16) for a in (kk, ws, wo))
    g = kk.shape[1]
    t, cg, ps = S5_CHUNK, SSM_GROUP, SSM_STATE
    kd = jnp.transpose(kk.reshape(2, g, t, cg, cg), (0, 1, 4, 2, 3))
    none = jnp.zeros((g, cg, t - 1, cg), BF16)

    def toeplitz(by_lag):
        rows = [by_lag[:, :, t - 1 - s:2 * t - 1 - s] for s in range(t)]
        return jnp.stack(rows, axis=1).reshape(g, t * cg, t * cg)

    mf = toeplitz(jnp.concatenate([none, kd[0]], axis=2))
    mb = toeplitz(jnp.concatenate([kd[1][:, :, ::-1], none], axis=2))
    w1 = jnp.concatenate([mf, mb, ws.reshape(g, t * cg, 4 * ps)], axis=-1)
    w2 = jnp.swapaxes(wo.reshape(g, t * cg, 4 * ps), 1, 2)
    lam = et.reshape(g // S5_TILE_GROUPS, 1, S5_TILE_GROUPS * 4 * ps)
    return w1, w2, lam


def _chunk_lane_permutation():
    t, gl, cg = S5_CHUNK, S5_TILE_GROUPS, SSM_GROUP
    src = np.arange(t * gl * cg).reshape(t, gl, cg)
    dst = np.transpose(src, (1, 0, 2)).reshape(-1)
    perm = np.zeros((t * gl * cg, t * gl * cg), np.float32)
    perm[dst, np.arange(t * gl * cg)] = 1.0
    return perm


def _s5_body(bsz, nk, chunk_major, u_ref, perm_ref, permt_ref, w1_ref, w2_ref, lam_ref, s0_ref, y_ref, fin_ref,
             tok_scr, u_scr, ug_scr, yl_scr, x_scr, p_scr):
    t_, lt, gl = S5_CHUNK, 128, S5_TILE_GROUPS
    half = SSM_STATE
    sw = 4 * half
    l = nk * t_
    tw = t_ * SSM_GROUP
    nrows = bsz * nk

    def token_rows(t):
        if chunk_major:
            return [(k * bsz, bsz, k * t_ + t, l) for k in range(nk)]
        return [(b * nk, nk, b * l + t, t_) for b in range(bsz)]

    tok_scr[...] = u_ref[...].astype(F32)
    for t in range(t_):
        for r0, n, tok0, stride in token_rows(t):
            u_scr[r0:r0 + n, t * lt:(t + 1) * lt] = tok_scr[pl.ds(tok0, n, stride=stride), :].astype(BF16)
    ug_scr[...] = jnp.dot(u_scr[...], perm_ref[...], preferred_element_type=F32).astype(BF16)
    for gi in range(gl):
        z = jnp.dot(ug_scr[:, gi * tw:(gi + 1) * tw], w1_ref[gi], preferred_element_type=F32)
        yl_scr[:, gi * tw:(gi + 1) * tw] = z[:, :tw] + z[:, tw:2 * tw]
        if chunk_major:
            x_scr[:, gi * sw:(gi + 1) * sw] = z[:, 2 * tw:]
        else:
            x_scr[0, pl.ds(gi, nrows, stride=gl), :] = z[:, 2 * tw:2 * tw + 2 * half]
            x_scr[1, pl.ds(gi, nrows, stride=gl), :] = z[:, 2 * tw + 2 * half:]
    p_scr[...] = jnp.zeros(p_scr.shape, F32)

    srows = bsz if chunk_major else gl
    fwd_lane = lax.broadcasted_iota(jnp.int32, (srows, 2 * half), 1) < half
    bwd_lane = jnp.logical_not(fwd_lane)
    lam = lam_ref[...]

    def advance(sr, si, lr, li, kf, kb, xr_ref, xi_ref, pr_ref, pi_ref):
        for ref, val in ((pr_ref, sr), (pi_ref, si)):
            pltpu.store(ref.at[kf, :], val, mask=fwd_lane)
            pltpu.store(ref.at[kb, :], val, mask=bwd_lane)
        xr = jnp.where(fwd_lane, xr_ref[kf, :], xr_ref[kb, :])
        xi = jnp.where(fwd_lane, xi_ref[kf, :], xi_ref[kb, :])
        return lr * sr - li * si + xr, lr * si + li * sr + xi

    if chunk_major:
        def step(t, carry):
            kf = pl.ds(pl.multiple_of(t * bsz, bsz), bsz)
            kb = pl.ds(pl.multiple_of((nk - 1 - t) * bsz, bsz), bsz)
            out = []
            for gi in range(gl):
                ca, cb = pl.ds(gi * sw, 2 * half), pl.ds(gi * sw + 2 * half, 2 * half)
                out.append(advance(*carry[gi], lam[:, gi * sw:gi * sw + 2 * half],
                                   lam[:, gi * sw + 2 * half:(gi + 1) * sw], kf, kb,
                                   x_scr.at[:, ca], x_scr.at[:, cb], p_scr.at[:, ca], p_scr.at[:, cb]))
            return tuple(out)

        init = tuple((s0_ref[:, gi * sw:gi * sw + 2 * half], s0_ref[:, gi * sw + 2 * half:(gi + 1) * sw])
                     for gi in range(gl))
        fin = lax.fori_loop(0, nk, step, init)
        for gi in range(gl):
            fin_ref[:, gi * sw:gi * sw + 2 * half] = fin[gi][0]
            fin_ref[:, gi * sw + 2 * half:(gi + 1) * sw] = fin[gi][1]
    else:
        by_group = lambda row: jnp.concatenate([row[:, gi * sw:(gi + 1) * sw] for gi in range(gl)], axis=0)
        lam8 = by_group(lam)

        def step(t, carry):
            out = []
            for q in range(bsz):
                kf = pl.ds(pl.multiple_of((q * nk + t) * gl, gl), gl)
                kb = pl.ds(pl.multiple_of((q * nk + nk - 1 - t) * gl, gl), gl)
                out.append(advance(*carry[q], lam8[:, 0:2 * half], lam8[:, 2 * half:], kf, kb,
                                   x_scr.at[0], x_scr.at[1], p_scr.at[0], p_scr.at[1]))
            return tuple(out)

        s0 = [by_group(s0_ref[q:q + 1, :]) for q in range(bsz)]
        fin = lax.fori_loop(0, nk, step, tuple((a[:, 0:2 * half], a[:, 2 * half:]) for a in s0), unroll=8)
        for q in range(bsz):
            for gi in range(gl):
                fin_ref[q:q + 1, gi * sw:gi * sw + 2 * half] = fin[q][0][gi:gi + 1]
                fin_ref[q:q + 1, gi * sw + 2 * half:(gi + 1) * sw] = fin[q][1][gi:gi + 1]

    for gi in range(gl):
        if chunk_major:
            pg = p_scr[:, gi * sw:(gi + 1) * sw]
        else:
            pg = jnp.concatenate([p_scr[0, pl.ds(gi, nrows, stride=gl), :], p_scr[1, pl.ds(gi, nrows, stride=gl), :]],
                                 axis=1)
        y = yl_scr[:, gi * tw:(gi + 1) * tw] + jnp.dot(pg.astype(BF16), w2_ref[gi], preferred_element_type=F32)
        u_scr[:, gi * tw:(gi + 1) * tw] = y.astype(BF16)
    yt = jnp.dot(u_scr[...], permt_ref[...], preferred_element_type=F32)
    for t in range(t_):
        for r0, n, tok0, stride in token_rows(t):
            tok_scr[pl.ds(tok0, n, stride=stride), :] = yt[r0:r0 + n, t * lt:(t + 1) * lt]
    y_ref[...] = tok_scr[...].astype(BF16)


def _s5(proj, col0, bsz, l, ops, s0):
    w1, w2, lam = ops
    g = w1.shape[0]
    t, ps, gl, lt = S5_CHUNK, SSM_STATE, S5_TILE_GROUPS, 128
    nb = g // gl
    nk = l // t
    r = bsz * l
    nrows = bsz * nk
    sw = 4 * ps
    cw = t * lt
    tw = t * SSM_GROUP
    chunk_major = bsz % 8 == 0
    state_shape = (nrows, gl * sw) if chunk_major else (2, nrows * gl, sw // 2)
    if s0 is None:
        s0p = jnp.zeros((bsz, g * sw), F32)
    else:
        s0_re, s0_im = s0
        s0p = jnp.concatenate([s0_re[:, 0], s0_re[:, 1], s0_im[:, 0], s0_im[:, 1]], axis=-1).reshape(bsz, g * sw)
    perm = _chunk_lane_permutation()
    const = lambda shape: _resident(shape, lambda i: (0,) * len(shape))
    y, fin = pl.pallas_call(
        functools.partial(_s5_body, bsz, nk, chunk_major),
        grid=(nb,),
        in_specs=[pl.BlockSpec((r, lt), lambda i: (0, col0 // lt + i), pipeline_mode=pl.Buffered(1)),
                  const((cw, cw)), const((cw, cw)),
                  pl.BlockSpec((gl, tw, 3 * tw), lambda i: (i, 0, 0)), pl.BlockSpec((gl, tw, tw), lambda i: (i, 0, 0)),
                  pl.BlockSpec((None, 1, gl * sw), lambda i: (i, 0, 0)),
                  pl.BlockSpec((bsz, gl * sw), lambda i: (0, i))],
        out_specs=[pl.BlockSpec((r, lt), lambda i: (0, i)), pl.BlockSpec((bsz, gl * sw), lambda i: (0, i))],
        out_shape=[jax.ShapeDtypeStruct((r, g * SSM_GROUP), BF16), jax.ShapeDtypeStruct((bsz, g * sw), F32)],
        scratch_shapes=[pltpu.VMEM((r, lt), F32), pltpu.VMEM((nrows, cw), BF16), pltpu.VMEM((nrows, cw), BF16),
                        pltpu.VMEM((nrows, cw), F32)] + [pltpu.VMEM(state_shape, F32)] * 2,
        compiler_params=_cparams(("parallel",)),
        name=f"s5_scan_{nk}",
    )(proj, jnp.asarray(perm, BF16), jnp.asarray(perm.T, BF16), w1, w2, lam, s0p)
    fin = jnp.transpose(fin.reshape(bsz, g, 2, 2, ps), (2, 0, 3, 1, 4))
    return y, fin[0], fin[1]


def _merge_body(yh_ref, ys_ref, u_ref, gh_ref, gs_ref, x_ref, d_ref, gw_ref, gbias_ref, wbh_ref, wbs_ref, wo_ref,
                g1_ref, n2_ref, sc2_ref, sh2_ref, x1_ref, h2_ref):
    y = ys_ref[...].astype(F32) + u_ref[...].astype(F32) * d_ref[...]
    g = jax.nn.gelu(y)
    gl = jnp.dot(g.astype(BF16), gw_ref[...], preferred_element_type=F32) + gbias_ref[...]
    ys = (g * jax.nn.sigmoid(gl)).astype(BF16)
    bh = jnp.dot(yh_ref[...], wbh_ref[...], preferred_element_type=F32)
    bs = jnp.dot(ys, wbs_ref[...], preferred_element_type=F32)
    merged = jax.nn.sigmoid(gh_ref[...].astype(F32)) * bh + jax.nn.sigmoid(gs_ref[...].astype(F32)) * bs
    x1 = x_ref[...] + g1_ref[...] * jnp.dot(merged.astype(BF16), wo_ref[...], preferred_element_type=F32)
    x1_ref[...] = x1
    h2_ref[...] = _mod_rmsnorm(x1, n2_ref[...], sc2_ref[...], sh2_ref[...]).astype(BF16)


def _merge(yh, ys, proj, x, mod4, cond_of_tile, ssm_d, glu_w, glu_b, wbh, wbs, wo, n2g, c_h, c_s, tm=256):
    r, d = x.shape
    row_blk = lambda width, k: pl.BlockSpec((tm, width), lambda i: (i, k))
    mod_spec = lambda chunk: pl.BlockSpec((None, None, 1, d), lambda i: (cond_of_tile(i, tm), chunk, 0, 0))
    const = lambda shape: _resident(shape, lambda i: (0,) * len(shape))
    u_col = (3 * c_h) // c_s
    gate0 = (3 * c_h + c_s) // d
    return pl.pallas_call(
        _merge_body,
        grid=(r // tm,),
        in_specs=[row_blk(c_h, 0), row_blk(c_s, 0), row_blk(c_s, u_col), row_blk(d, gate0), row_blk(d, gate0 + 1),
                  row_blk(d, 0), const((1, c_s)), const((c_s, c_s)), const((1, c_s)), const((c_h, d)),
                  const((c_s, d)), const((d, d)), mod_spec(2), const((1, d)), mod_spec(4), mod_spec(3)],
        out_specs=[row_blk(d, 0), row_blk(d, 0)],
        out_shape=[jax.ShapeDtypeStruct((r, d), F32), jax.ShapeDtypeStruct((r, d), BF16)],
        compiler_params=_cparams(("parallel",)),
        name="merge",
    )(yh, ys, proj, proj, proj, x, ssm_d.reshape(1, c_s), glu_w, glu_b.reshape(1, c_s), wbh, wbs, wo,
      mod4, n2g.reshape(1, d), mod4, mod4)


_FFN_RC = 512


def _ffn_up_body(rows, cols, h_ref, wa_ref, wb_ref, cwa_ref, cwb_ref, cba_ref, cbb_ref, o_ref):
    tm, tn = o_ref.shape
    rc = _FFN_RC
    n = tm // rc
    col = lax.rem(lax.broadcasted_iota(jnp.int32, (rc, tn), 0), cols)
    not_first = col != 0
    not_last = col != cols - 1
    tdt = BF16 if rows > 1 else F32
    zero = jnp.zeros((cols, tn), tdt)

    def taps(hc, w_ref, cw_ref, cb_ref):
        a = jnp.dot(hc, w_ref[...], preferred_element_type=F32)
        am = jnp.where(not_first, pltpu.roll(a, 1, 0), 0.0).astype(tdt)
        ap = jnp.where(not_last, pltpu.roll(a, rc - 1, 0), 0.0).astype(tdt)
        a = a.astype(tdt)
        cw = cw_ref[...].astype(tdt)
        tap = lambda i: am * cw[3 * i:3 * i + 1] + a * cw[3 * i + 1:3 * i + 2] + ap * cw[3 * i + 2:3 * i + 3]
        mid = tap(1) + cb_ref[...].astype(tdt)
        return (tap(0), mid, tap(2)) if rows > 1 else (None, mid, None)

    def finish(prev, cur, nxt):
        t0, mid, t2 = cur
        if rows == 1:
            return mid
        above = jnp.concatenate([prev[0][rc - cols:] if prev is not None else zero, t0[:rc - cols]], axis=0)
        below = jnp.concatenate([t2[cols:], nxt[2][:cols] if nxt is not None else zero], axis=0)
        return mid + above + below

    chunks = []
    for i in range(n + 1):
        if i < n:
            hc = h_ref[i * rc:(i + 1) * rc, :]
            chunks.append((taps(hc, wa_ref, cwa_ref, cba_ref), taps(hc, wb_ref, cwb_ref, cbb_ref)))
        if i >= 1:
            j = i - 1
            pick = lambda k, side: chunks[k][side] if 0 <= k < n else None
            a = finish(pick(j - 1, 0), pick(j, 0), pick(j + 1, 0))
            b = finish(pick(j - 1, 1), pick(j, 1), pick(j + 1, 1))
            o_ref[j * rc:(j + 1) * rc, :] = (jax.nn.gelu(a.astype(F32)) * b.astype(F32)).astype(BF16)


def _ffn_up(h2, w_up, conv_w, conv_b, rows, cols, tn=256):
    r, d = h2.shape
    f = w_up.shape[1] // 2
    tm = rows * cols if rows > 1 else 4096
    assert tm % _FFN_RC == 0 and _FFN_RC % cols == 0 and cols % 8 == 0
    nt = f // tn
    cw = conv_w.reshape(9, 2 * f)
    cb = conv_b.reshape(1, 2 * f)
    return pl.pallas_call(
        functools.partial(_ffn_up_body, rows, cols),
        grid=(r // tm, nt),
        in_specs=[pl.BlockSpec((tm, d), lambda i, j: (i, 0), pipeline_mode=pl.Buffered(1)),
                  pl.BlockSpec((d, tn), lambda i, j: (0, j)), pl.BlockSpec((d, tn), lambda i, j: (0, nt + j)),
                  pl.BlockSpec((9, tn), lambda i, j: (0, j)), pl.BlockSpec((9, tn), lambda i, j: (0, nt + j)),
                  pl.BlockSpec((1, tn), lambda i, j: (0, j)), pl.BlockSpec((1, tn), lambda i, j: (0, nt + j))],
        out_specs=pl.BlockSpec((tm, tn), lambda i, j: (i, j)),
        out_shape=jax.ShapeDtypeStruct((r, f), BF16),
        compiler_params=_cparams(("parallel", "arbitrary")),
        name=f"ffn_up_{rows}x{cols}",
    )(h2, w_up, w_up, cw, cw, cb, cb)


def _ffn_down_body(a_ref, w_ref, x_ref, g2_ref, fg_ref, o_ref):
    x2 = x_ref[...] + g2_ref[...] * jnp.dot(a_ref[...], w_ref[...], preferred_element_type=F32)
    ms = jnp.mean(x2 * x2, axis=-1, keepdims=True)
    o_ref[...] = x2 * lax.rsqrt(ms + EPS) * fg_ref[...]


def _ffn_down(act, w, x1, mod4, cond_of_tile, final_g, tm=256):
    r, f = act.shape
    d = w.shape[1]
    return pl.pallas_call(
        _ffn_down_body,
        grid=(r // tm,),
        in_specs=[pl.BlockSpec((tm, f), lambda i: (i, 0)),
                  _resident((f, d), lambda i: (0, 0)),
                  pl.BlockSpec((tm, d), lambda i: (i, 0)),
                  pl.BlockSpec((None, None, 1, d), lambda i: (cond_of_tile(i, tm), 5, 0, 0)),
                  _resident((1, d), lambda i: (0, 0))],
        out_specs=pl.BlockSpec((tm, d), lambda i: (i, 0)),
        out_shape=jax.ShapeDtypeStruct((r, d), F32),
        compiler_params=_cparams(("parallel",)),
        name="ffn_down",
    )(act, w, x1, mod4, final_g.reshape(1, d))


def _segment(x, cond_base, per_batch_cond, s0, rows, cols, hy_block, hy_nseq, mod4, p, final_g):
    bsz, l, d = x.shape
    xf = x.reshape(bsz * l, d)
    c_h = p["wbh"].shape[0]
    c_s = p["wbs"].shape[0]

    def cond_of_tile(i, tm):
        return cond_base + (i * tm) // l if per_batch_cond else cond_base

    proj = _in_proj(xf, mod4, cond_of_tile, p["norm1_g"], p["w_in"], p["b_in"])
    yh = _hyena(proj, l, hy_block, hy_nseq, p["hy_short_w"], p["hy_short_b"], p["hy_bias"], p["ghat"][l], c_h,
                ct=128 if l // hy_block > 1 else 256)
    ys, fin_re, fin_im = _s5(proj, 3 * c_h, bsz, l, p["s5_ops"], s0)
    x1, h2 = _merge(yh, ys, proj, xf, mod4, cond_of_tile, p["ssm_d"], p["glu_w"], p["glu_b"],
                    p["wbh"], p["wbs"], p["wo"], p["norm2_g"], c_h, c_s)
    act = _ffn_up(h2, p["ffn_up"], p["ffn_conv_w"], p["ffn_conv_b"], rows, cols)
    y = _ffn_down(act, p["ffn_down"], x1, mod4, cond_of_tile, final_g)
    return y.reshape(bsz, l, d), fin_re, fin_im


def kernel(x_prompt, x_sample, state_ssm_re, state_ssm_im, c, c_ctx, ada_w, ada_b, norm1_g, norm2_g, final_g, w_in, b_in, hy_short_w, hy_short_b, hy_f_w1, hy_f_b1, hy_f_w2, hy_f_b2, hy_f_w3, hy_f_b3, hy_f_w4, hy_f_freq, hy_bias, w_branch_h, ssm_lambda_re, ssm_lambda_im, ssm_log_dt, ssm_b_re, ssm_b_im, ssm_c_re, ssm_c_im, ssm_d, ssm_glu_w, ssm_glu_b, w_branch_s, w_out, ffn_up, ffn_conv_w, ffn_conv_b, ffn_down):
    depth = ada_w.shape[0]
    assert depth == 1, "the final norm is fused into the (single) layer's last kernel"
    d = x_prompt.shape[-1]
    ctx_len = x_prompt.shape[1]
    lat_len = x_sample.shape[1]
    dec_b = x_sample.shape[0]
    c_h = w_branch_h.shape[1]
    assert 1 + dec_b <= 8
    cond = jnp.concatenate([c_ctx[None], c, jnp.zeros((8 - 1 - dec_b, d), F32)], axis=0)
    lay = 0
    mod4 = _ada(cond, ada_w[lay], ada_b[lay]).reshape(8, N_MOD, 1, d)
    hy_blocks = {ctx_len: 256, lat_len: 512}
    ghat = {l: _hyena_filters(l, hy_blocks[l], hy_f_w1[lay], hy_f_b1[lay], hy_f_w2[lay], hy_f_b2[lay],
                              hy_f_w3[lay], hy_f_b3[lay], hy_f_w4[lay], hy_f_freq[lay], c_h)
            for l in (ctx_len, lat_len)}
    s5_ops = _s5_operators(ssm_lambda_re[lay], ssm_lambda_im[lay], ssm_log_dt[lay],
                           ssm_b_re[lay], ssm_b_im[lay], ssm_c_re[lay], ssm_c_im[lay])
    p = dict(norm1_g=norm1_g[lay], norm2_g=norm2_g[lay], w_in=w_in[lay].astype(BF16), b_in=b_in[lay],
             hy_short_w=hy_short_w[lay], hy_short_b=hy_short_b[lay], hy_bias=hy_bias[lay], ghat=ghat,
             s5_ops=s5_ops, ssm_d=ssm_d[lay], glu_w=ssm_glu_w[lay].astype(BF16),
             glu_b=ssm_glu_b[lay], wbh=w_branch_h[lay].astype(BF16), wbs=w_branch_s[lay].astype(BF16),
             wo=w_out[lay].astype(BF16), ffn_up=ffn_up[lay].astype(BF16), ffn_conv_w=ffn_conv_w[lay],
             ffn_conv_b=ffn_conv_b[lay], ffn_down=ffn_down[lay].astype(BF16))
    y_prompt, st_re, st_im = _segment(x_prompt, 0, False, None, 1, ctx_len, hy_blocks[ctx_len], 16, mod4, p, final_g)
    s0 = (state_ssm_re[:, lay].astype(F32), state_ssm_im[:, lay].astype(F32))
    y_sample, _, _ = _segment(x_sample, 1, True, s0, lat_len // GRID_W, GRID_W, hy_blocks[lat_len], 1, mod4, p, final_g)
    return y_prompt, y_sample, st_re[:, None], st_im[:, None]
```

```python
import functools
import math

import jax
import jax.numpy as jnp
import numpy as np
from jax import lax
from jax.experimental import pallas as pl
from jax.experimental.pallas import tpu as pltpu

F32 = jnp.float32
BF16 = jnp.bfloat16
HIGHEST = lax.Precision.HIGHEST

GRID_W = 64
EPS = 1e-6
SSM_GROUP = 16
SSM_STATE = 64
FILTER_EMB = 33
FILTER_HIDDEN = 64
HYENA_ORDER = 2
DECAY_FAST = 0.3
DECAY_SLOW = 1.5
DECAY_TARGET = 1e-2
LAMBDA_RE_MAX = -1e-4

S5_CHUNK = 16
S5_TILE_GROUPS = 8
V7X_VMEM_LIMIT = 56 * 2**20
N_MOD = 6


def _cparams(sem):
    return pltpu.CompilerParams(dimension_semantics=sem, vmem_limit_bytes=V7X_VMEM_LIMIT)


def _resident(shape, index_map):
    return pl.BlockSpec(shape, index_map, pipeline_mode=pl.Buffered(1))


def _mod_rmsnorm(x, g, sc, sh):
    ms = jnp.mean(x * x, axis=-1, keepdims=True)
    return x * lax.rsqrt(ms + EPS) * g * (1.0 + sc) + sh


def _ada_body(c_ref, w_ref, b_ref, o_ref):
    c = c_ref[...]
    s = c * jax.nn.sigmoid(c)
    o_ref[...] = _dot_3pass(s, *_split_bf16(w_ref[...])) + b_ref[...]


def _ada(cond, w, b):
    d, n = w.shape
    tn = 1024
    return pl.pallas_call(
        _ada_body,
        grid=(n // tn,),
        in_specs=[pl.BlockSpec((8, d), lambda j: (0, 0)),
                  pl.BlockSpec((d, tn), lambda j: (0, j)),
                  pl.BlockSpec((1, tn), lambda j: (0, j))],
        out_specs=pl.BlockSpec((8, tn), lambda j: (0, j)),
        out_shape=jax.ShapeDtypeStruct((8, n), F32),
        compiler_params=_cparams(("parallel",)),
        name="ada",
    )(cond, w, b.reshape(1, n))


def _inproj_body(x_ref, g_ref, sc_ref, sh_ref, w_ref, b_ref, o_ref, h_scr):
    @pl.when(pl.program_id(1) == 0)
    def _():
        h_scr[...] = _mod_rmsnorm(x_ref[...], g_ref[...], sc_ref[...], sh_ref[...]).astype(BF16)

    o_ref[...] = (jnp.dot(h_scr[...], w_ref[...], preferred_element_type=F32) + b_ref[...]).astype(BF16)


def _in_proj(x, mod4, cond_of_tile, g, w, b, tm=1024, tn=1024):
    r, d = x.shape
    n = w.shape[1]
    mod_spec = lambda chunk: pl.BlockSpec((None, None, 1, d), lambda i, j: (cond_of_tile(i, tm), chunk, 0, 0))
    return pl.pallas_call(
        _inproj_body,
        grid=(r // tm, n // tn),
        in_specs=[pl.BlockSpec((tm, d), lambda i, j: (i, 0)),
                  pl.BlockSpec((1, d), lambda i, j: (0, 0)),
                  mod_spec(1), mod_spec(0),
                  pl.BlockSpec((d, tn), lambda i, j: (0, j)),
                  pl.BlockSpec((1, tn), lambda i, j: (0, j))],
        out_specs=pl.BlockSpec((tm, tn), lambda i, j: (i, j)),
        out_shape=jax.ShapeDtypeStruct((r, n), BF16),
        scratch_shapes=[pltpu.VMEM((tm, d), BF16)],
        compiler_params=_cparams(("parallel", "arbitrary")),
        name="in_proj",
    )(x, g.reshape(1, d), mod4, mod4, w, b.reshape(1, n))


def _shifted_dft_tables(p):
    theta = np.pi * (2.0 * np.arange(p) + 1.0) / (2.0 * p)
    n = np.arange(p)
    a = theta[:, None] * n[None, :]
    fwd = np.concatenate([np.cos(a), -np.sin(a)], axis=0)
    m = np.arange(2 * p) - p
    am = theta[:, None] * m[None, :]
    filt = np.concatenate([np.cos(am), -np.sin(am)], axis=0)
    filt[:, 0] = 0.0
    inv = np.concatenate([np.cos(a).T, -np.sin(a).T], axis=1) / p
    return fwd, filt, inv


_FILTER_RC = 1024


def _filter_positions(l):
    pos = np.abs(np.arange(2 * l) - l).astype(np.float64)
    t = pos / (l - 1)
    bands = (FILTER_EMB - 1) // 2
    fr = np.linspace(1e-4, bands - 1, bands)
    ang = (2.0 * math.pi / l) * pos[:, None] * fr[None, :]
    z = np.concatenate([t[:, None], np.cos(ang), -np.sin(ang)], axis=-1)
    zp = np.zeros((2 * l, 128), np.float32)
    zp[:, :FILTER_EMB] = z
    return zp


def _filter_hidden_body(z_ref, w1_ref, b1_ref, w2_ref, b2_ref, w3_ref, b3_ref, fq_ref, o_ref):
    fq = fq_ref[...]
    dot = functools.partial(jnp.dot, precision=HIGHEST, preferred_element_type=F32)
    h = jnp.sin(fq * (dot(z_ref[...], w1_ref[...]) + b1_ref[...]))
    h = jnp.sin(fq * (dot(h, w2_ref[...]) + b2_ref[...]))
    o_ref[...] = jnp.sin(fq * (dot(h, w3_ref[...]) + b3_ref[...]))


def _filter_hidden(zp, w1p, b1, w2, b2, w3, b3, freq):
    rows, kin = zp.shape
    rc = min(rows, _FILTER_RC)
    hd = w2.shape[0]
    full = lambda shape: pl.BlockSpec(shape, lambda i: (0,) * len(shape))
    row = lambda a: a.reshape(1, -1)
    return pl.pallas_call(
        _filter_hidden_body,
        grid=(rows // rc,),
        in_specs=[pl.BlockSpec((rc, kin), lambda i: (i, 0)), full((kin, hd)), full((1, hd)), full((hd, hd)),
                  full((1, hd)), full((hd, hd)), full((1, hd)), full((1, hd))],
        out_specs=pl.BlockSpec((rc, hd), lambda i: (i, 0)),
        out_shape=jax.ShapeDtypeStruct((rows, hd), F32),
        compiler_params=_cparams(("parallel",)),
        name=f"hyena_filter_mlp_{rows}",
    )(zp, w1p, row(b1), w2, row(b2), w3, row(b3), row(freq))


def _split_bf16(x):
    hi = x.astype(BF16)
    return hi, (x - hi.astype(F32)).astype(BF16)


def _dot_3pass(x, w_hi, w_lo):
    x_hi, x_lo = _split_bf16(x)
    d = functools.partial(jnp.dot, preferred_element_type=F32)
    return d(x_hi, w_hi) + (d(x_lo, w_hi) + d(x_hi, w_lo))


def _filter_body(l, p, h_ref, w4f_ref, w4b_ref, dl_ref, mg_ref, o_ref, kk_scr, mg_scr):
    mg_scr[...] = mg_ref[...].astype(BF16)
    rc = min(l, _FILTER_RC)
    ct = kk_scr.shape[1]
    nhalf = l // rc

    def taps(w_ref, first):
        w_hi, w_lo = _split_bf16(w_ref[...])

        def body(i, asum):
            r0 = pl.multiple_of(i * rc, rc)
            q = r0 + lax.broadcasted_iota(jnp.int32, (rc, ct), 0)
            t = jnp.abs(q - l).astype(F32) * (1.0 / (l - 1))
            k = _dot_3pass(h_ref[pl.ds(r0, rc), :], w_hi, w_lo) * jnp.exp(-t * dl_ref[...])
            k = jnp.where(q == 0, 0.0, k)
            kk_scr[pl.ds(r0, rc), :] = k.astype(BF16)
            return asum + jnp.sum(jnp.abs(k), axis=0, keepdims=True)

        return lambda asum: lax.fori_loop(first, first + nhalf, body, asum)

    asum = taps(w4f_ref, nhalf)(taps(w4b_ref, 0)(jnp.zeros((1, ct), F32)))
    scale = 1.0 / asum
    nwin = 2 * (l // p) - 1

    def win(w, _):
        seg = kk_scr[pl.ds(pl.multiple_of(w * p, p), 2 * p), :]
        o_ref[w] = jnp.dot(mg_scr[...], seg, preferred_element_type=F32) * scale
        return 0

    lax.fori_loop(0, nwin, win, 0)


def _hyena_filters(l, p, w1, b1, w2, b2, w3, b3, w4, freq, c, ct=256):
    nwin = 2 * (l // p) - 1
    nct = c // ct
    zp = _filter_positions(l)
    zp2 = jnp.asarray(np.concatenate([zp[:l], zp[l:]], axis=1))
    w1p = jnp.pad(w1, ((0, 128 - FILTER_EMB), (0, 0)))
    twice = lambda w: jnp.concatenate([jnp.pad(w, ((0, 0), (0, w.shape[1]))), jnp.pad(w, ((0, 0), (w.shape[1], 0)))], axis=0)
    both = lambda v: jnp.concatenate([v, v])
    hid2 = _filter_hidden(zp2, twice(w1p), both(b1), twice(w2), both(b2), twice(w3), both(b3), both(freq))
    hid = jnp.concatenate([hid2[:, :FILTER_HIDDEN], hid2[:, FILTER_HIDDEN:]], axis=0)
    deltas = np.abs(np.linspace(math.log(DECAY_TARGET) / DECAY_SLOW, math.log(DECAY_TARGET) / DECAY_FAST, c))
    mg = jnp.asarray(_shifted_dft_tables(p)[1], F32)
    full = lambda shape: pl.BlockSpec(shape, lambda o, j: (0,) * len(shape))
    return pl.pallas_call(
        functools.partial(_filter_body, l, p),
        grid=(HYENA_ORDER, nct),
        in_specs=[full((2 * l, FILTER_HIDDEN)),
                  pl.BlockSpec((FILTER_HIDDEN, ct), lambda o, j: (0, o * 2 * nct + j)),
                  pl.BlockSpec((FILTER_HIDDEN, ct), lambda o, j: (0, o * 2 * nct + nct + j)),
                  pl.BlockSpec((1, ct), lambda o, j: (0, j)),
                  full((2 * p, 2 * p))],
        out_specs=pl.BlockSpec((None, nwin, 2 * p, ct), lambda o, j: (o, 0, 0, j)),
        out_shape=jax.ShapeDtypeStruct((HYENA_ORDER, nwin, 2 * p, c), F32),
        scratch_shapes=[pltpu.VMEM((2 * l, ct), BF16), pltpu.VMEM((2 * p, 2 * p), BF16)],
        compiler_params=_cparams(("parallel", "parallel")),
        name=f"hyena_filter_{l}",
    )(hid, w4, w4, jnp.asarray(deltas, F32).reshape(1, c), mg)


_HY_PASS_VREGS = 8


def _hyena_body(nseq, l, p, x1_ref, x2_ref, v_ref, w1_ref, w2_ref, wv_ref, b1_ref, b2_ref, bv_ref, hb_ref,
                g_ref, fz_ref, fi_ref, o_ref, z_scr, c_scr, zh_scr, yh_scr, z1_scr, fz_scr, fi_scr):
    fz_scr[...] = fz_ref[...].astype(BF16)
    fi_scr[...] = fi_ref[...].astype(BF16)
    o = pl.program_id(1)
    s = pl.program_id(2)
    rows = nseq * l
    nb = l // p
    ct = o_ref.shape[1]
    hrc = _HY_PASS_VREGS * 8 * 128 // ct
    pair = yh_scr.shape[1] // ct
    pos = lax.rem(lax.broadcasted_iota(jnp.int32, (rows, ct), 0), l)

    def short_conv(u_ref, w_ref, b_ref):
        u = u_ref[...].astype(F32)
        um = jnp.where(pos == 0, 0.0, pltpu.roll(u, 1, 0))
        up = jnp.where(pos == l - 1, 0.0, pltpu.roll(u, rows - 1, 0))
        w = w_ref[...]
        return um * w[0:1] + u * w[1:2] + up * w[2:3] + b_ref[...]

    def long_conv():
        for q in range(nseq):
            base = q * l

            def block(k):
                return pl.ds(pl.multiple_of(base + k * p, p), p)

            def fwd(jj, _):
                zb = jnp.concatenate([z_scr[block(jj * pair + k), :] for k in range(pair)], axis=1).astype(BF16)
                zh = jnp.dot(fz_scr[...], zb, preferred_element_type=F32)
                for k in range(pair):
                    zh_scr[jj * pair + k] = zh[:, k * ct:(k + 1) * ct]
                return 0

            lax.fori_loop(0, nb // pair, fwd, 0)

            def out_block(ii, _):
                for k in range(pair):
                    i = ii * pair + k
                    for rc in range(p // hrc):
                        re = pl.ds(rc * hrc, hrc)
                        im = pl.ds(p + rc * hrc, hrc)

                        def acc(j, carry):
                            ar, ai = carry
                            w = i - j + (nb - 1)
                            gr, gi = g_ref[w, re, :], g_ref[w, im, :]
                            zr, zi = zh_scr[j, re, :], zh_scr[j, im, :]
                            return ar + gr * zr - gi * zi, ai + gr * zi + gi * zr

                        zero = jnp.zeros((hrc, ct), F32)
                        ar, ai = lax.fori_loop(0, nb, acc, (zero, zero), unroll=True)
                        yh_scr[re, k * ct:(k + 1) * ct] = ar
                        yh_scr[im, k * ct:(k + 1) * ct] = ai
                c = jnp.dot(fi_scr[...], yh_scr[...].astype(BF16), preferred_element_type=F32)
                for k in range(pair):
                    c_scr[block(ii * pair + k), :] = c[:, k * ct:(k + 1) * ct]
                return 0

            lax.fori_loop(0, nb // pair, out_block, 0)

    hb = hb_ref[...]

    @pl.when(o == 0)
    def _():
        v = short_conv(v_ref, wv_ref, bv_ref)
        z_scr[...] = v
        long_conv()
        x1 = short_conv(x1_ref, w1_ref, b1_ref)
        z1_scr[s] = x1 * (c_scr[...] + hb[0:1] * v)

    @pl.when(o == 1)
    def _():
        z1 = z1_scr[s]
        z_scr[...] = z1
        long_conv()
        x2 = short_conv(x2_ref, w2_ref, b2_ref)
        o_ref[...] = (x2 * (c_scr[...] + hb[1:2] * z1)).astype(BF16)


def _hyena(proj, l, p, nseq, short_w, short_b, hy_bias, ghat, c, ct=128):
    r = proj.shape[0]
    rows = nseq * l
    nsb = r // rows
    nct = c // ct
    nwin = ghat.shape[1]
    fwd, _, inv = _shifted_dft_tables(p)
    pair = 2 if (ct < 256 and (l // p) % 2 == 0) else 1
    fz = jnp.asarray(fwd, F32)
    fi = jnp.asarray(inv, F32)
    sb = short_b.reshape(1, 3 * c)
    col = lambda k: (lambda j, o, s: (s, k * nct + j))
    wcol = lambda k: (lambda j, o, s: (0, k * nct + j))
    return pl.pallas_call(
        functools.partial(_hyena_body, nseq, l, p),
        grid=(nct, HYENA_ORDER, nsb),
        in_specs=[pl.BlockSpec((rows, ct), col(0)), pl.BlockSpec((rows, ct), col(1)), pl.BlockSpec((rows, ct), col(2)),
                  pl.BlockSpec((3, ct), wcol(0)), pl.BlockSpec((3, ct), wcol(1)), pl.BlockSpec((3, ct), wcol(2)),
                  pl.BlockSpec((1, ct), wcol(0)), pl.BlockSpec((1, ct), wcol(1)), pl.BlockSpec((1, ct), wcol(2)),
                  pl.BlockSpec((HYENA_ORDER, ct), lambda j, o, s: (0, j)),
                  pl.BlockSpec((None, nwin, 2 * p, ct), lambda j, o, s: (o, 0, 0, j)),
                  pl.BlockSpec((2 * p, p), lambda j, o, s: (0, 0)),
                  pl.BlockSpec((p, 2 * p), lambda j, o, s: (0, 0))],
        out_specs=pl.BlockSpec((rows, ct), lambda j, o, s: (s * o, j)),
        out_shape=jax.ShapeDtypeStruct((r, c), BF16),
        scratch_shapes=[pltpu.VMEM((rows, ct), F32), pltpu.VMEM((rows, ct), F32),
                        pltpu.VMEM((l // p, 2 * p, ct), F32), pltpu.VMEM((2 * p, pair * ct), F32),
                        pltpu.VMEM((nsb, rows, ct), F32), pltpu.VMEM((2 * p, p), BF16), pltpu.VMEM((p, 2 * p), BF16)],
        compiler_params=_cparams(("parallel", "arbitrary", "arbitrary")),
        name=f"hyena_{l}",
    )(proj, proj, proj, short_w, short_w, short_w, sb, sb, sb, hy_bias, ghat, fz, fi)


def _s5_param_body(backward, lre_ref, lim_ref, ldt_ref, btr_ref, bti_ref, cr_ref, ci_ref,
                   k_ref, wsr_ref, wsi_ref, wor_ref, woi_ref, etr_ref, eti_ref):
    t = S5_CHUNK
    lr = jnp.minimum(lre_ref[...], LAMBDA_RE_MAX)
    li = lim_ref[...]
    dt = jnp.exp(ldt_ref[...])
    ar, ai = lr * dt, li * dt

    def power(e):
        mag = jnp.exp(e * ar)
        return mag * jnp.cos(e * ai), mag * jnp.sin(e * ai)

    per_step = lambda a: a[:, None, :]

    one = jnp.ones((1, 1), F32)
    lbr, lbi = power(one)
    nr, ni = lbr - 1.0, lbi
    den = lr * lr + li * li
    qr, qi = (nr * lr + ni * li) / den, (ni * lr - nr * li) / den
    btr, bti = btr_ref[...], bti_ref[...]
    bbr, bbi = qr * btr - qi * bti, qr * bti + qi * btr
    cr, ci = cr_ref[...], ci_ref[...]

    step = lax.broadcasted_iota(jnp.int32, (t, 1), 0).astype(F32)
    er, ei = map(per_step, power(step))
    cer, cei = cr[None] * er - ci[None] * ei, cr[None] * ei + ci[None] * er
    nt = (((1,), (1,)), ((), ()))
    dg = functools.partial(lax.dot_general, dimension_numbers=nt, precision=HIGHEST, preferred_element_type=F32)
    k_ref[...] = (dg(cer.reshape(t * SSM_GROUP, SSM_STATE), bbr)
                  - dg(cei.reshape(t * SSM_GROUP, SSM_STATE), bbi))
    er, ei = map(per_step, power(jnp.where(backward, step, (t - 1.0) - step)))
    wsr_ref[...] = er * bbr[None] - ei * bbi[None]
    wsi_ref[...] = er * bbi[None] + ei * bbr[None]
    er, ei = map(per_step, power(jnp.where(backward, t - step, step + 1.0)))
    wor_ref[...] = cr[None] * er - ci[None] * ei
    woi_ref[...] = -(cr[None] * ei + ci[None] * er)
    etr, eti = power(one * t)
    etr_ref[...] = etr
    eti_ref[...] = eti


def _s5_param_tile_body(lre, lim, ldt, btr, bti, cr, ci, k_ref, ws_ref, wo_ref, et_ref):
    ps = SSM_STATE
    for d in range(2):
        for gi in range(lre.shape[1]):
            lanes = lambda part: pl.ds((2 * part + d) * ps, ps)
            _s5_param_body(d == 1, *(r.at[d, gi] for r in (lre, lim, ldt, btr, bti, cr, ci)), k_ref.at[d, gi],
                           ws_ref.at[gi, :, :, lanes(0)], ws_ref.at[gi, :, :, lanes(1)],
                           wo_ref.at[gi, :, :, lanes(0)], wo_ref.at[gi, :, :, lanes(1)],
                           et_ref.at[gi, :, lanes(0)], et_ref.at[gi, :, lanes(1)])


def _s5_params(lam_re, lam_im, log_dt, b_re, b_im, c_re, c_im):
    g = lam_re.shape[1]
    gb = S5_TILE_GROUPS
    t, cg, ps = S5_CHUNK, SSM_GROUP, SSM_STATE
    vec = lambda a: a.reshape(2, g, 1, ps)
    ldt = jnp.broadcast_to(log_dt[:, :, None, None], (2, g, 1, ps))
    bt = lambda a: jnp.swapaxes(a, -1, -2)
    in_spec = lambda *shape: pl.BlockSpec((2, gb) + shape, lambda i: (0, i) + (0,) * len(shape))
    out_spec = lambda *shape: pl.BlockSpec((gb,) + shape, lambda i: (i,) + (0,) * len(shape))
    outs = [jax.ShapeDtypeStruct((2, g, t * cg, cg), F32)] + \
           [jax.ShapeDtypeStruct((g, t, cg, 4 * ps), F32)] * 2 + [jax.ShapeDtypeStruct((g, 1, 4 * ps), F32)]
    return pl.pallas_call(
        _s5_param_tile_body,
        grid=(g // gb,),
        in_specs=[in_spec(1, ps), in_spec(1, ps), in_spec(1, ps), in_spec(cg, ps), in_spec(cg, ps),
                  in_spec(cg, ps), in_spec(cg, ps)],
        out_specs=[in_spec(t * cg, cg), out_spec(t, cg, 4 * ps), out_spec(t, cg, 4 * ps), out_spec(1, 4 * ps)],
        out_shape=outs,
        compiler_params=_cparams(("parallel",)),
        name="s5_params",
    )(vec(lam_re), vec(lam_im), ldt, bt(b_re), bt(b_im), c_re, c_im)


def _s5_operators(lam_re, lam_im, log_dt, b_re, b_im, c_re, c_im):
    kk, ws, wo, et = _s5_params(lam_re, lam_im, log_dt, b_re, b_im, c_re, c_im)
    kk, ws, wo = (a.astype(BF16) for a in (kk, ws, wo))
    g = kk.shape[1]
    t, cg, ps = S5_CHUNK, SSM_GROUP, SSM_STATE
    kd = jnp.transpose(kk.reshape(2, g, t, cg, cg), (0, 1, 4, 2, 3))
    none = jnp.zeros((g, cg, t - 1, cg), BF16)

    def toeplitz(by_lag):
        rows = [by_lag[:, :, t - 1 - s:2 * t - 1 - s] for s in range(t)]
        return jnp.stack(rows, axis=1).reshape(g, t * cg, t * cg)

    mf = toeplitz(jnp.concatenate([none, kd[0]], axis=2))
    mb = toeplitz(jnp.concatenate([kd[1][:, :, ::-1], none], axis=2))
    w1 = jnp.concatenate([mf, mb, ws.reshape(g, t * cg, 4 * ps)], axis=-1)
    w2 = jnp.swapaxes(wo.reshape(g, t * cg, 4 * ps), 1, 2)
    lam = et.reshape(g // S5_TILE_GROUPS, 1, S5_TILE_GROUPS * 4 * ps)
    return w1, w2, lam


def _chunk_lane_permutation():
    t, gl, cg = S5_CHUNK, S5_TILE_GROUPS, SSM_GROUP
    src = np.arange(t * gl * cg).reshape(t, gl, cg)
    dst = np.transpose(src, (1, 0, 2)).reshape(-1)
    perm = np.zeros((t * gl * cg, t * gl * cg), np.float32)
    perm[dst, np.arange(t * gl * cg)] = 1.0
    return perm


def _s5_body(bsz, nk, chunk_major, u_ref, perm_ref, permt_ref, w1_ref, w2_ref, lam_ref, s0_ref, y_ref, fin_ref,
             tok_scr, u_scr, ug_scr, yl_scr, x_scr, p_scr):
    t_, lt, gl = S5_CHUNK, 128, S5_TILE_GROUPS
    half = SSM_STATE
    sw = 4 * half
    l = nk * t_
    tw = t_ * SSM_GROUP
    nrows = bsz * nk

    def token_rows(t):
        if chunk_major:
            return [(k * bsz, bsz, k * t_ + t, l) for k in range(nk)]
        return [(b * nk, nk, b * l + t, t_) for b in range(bsz)]

    tok_scr[...] = u_ref[...].astype(F32)
    for t in range(t_):
        for r0, n, tok0, stride in token_rows(t):
            u_scr[r0:r0 + n, t * lt:(t + 1) * lt] = tok_scr[pl.ds(tok0, n, stride=stride), :].astype(BF16)
    ug_scr[...] = jnp.dot(u_scr[...], perm_ref[...], preferred_element_type=F32).astype(BF16)
    for gi in range(gl):
        z = jnp.dot(ug_scr[:, gi * tw:(gi + 1) * tw], w1_ref[gi], preferred_element_type=F32)
        yl_scr[:, gi * tw:(gi + 1) * tw] = z[:, :tw] + z[:, tw:2 * tw]
        if chunk_major:
            x_scr[:, gi * sw:(gi + 1) * sw] = z[:, 2 * tw:]
        else:
            x_scr[0, pl.ds(gi, nrows, stride=gl), :] = z[:, 2 * tw:2 * tw + 2 * half]
            x_scr[1, pl.ds(gi, nrows, stride=gl), :] = z[:, 2 * tw + 2 * half:]
    p_scr[...] = jnp.zeros(p_scr.shape, F32)

    srows = bsz if chunk_major else gl
    fwd_lane = lax.broadcasted_iota(jnp.int32, (srows, 2 * half), 1) < half
    bwd_lane = jnp.logical_not(fwd_lane)
    lam = lam_ref[...]

    def advance(sr, si, lr, li, kf, kb, xr_ref, xi_ref, pr_ref, pi_ref):
        for ref, val in ((pr_ref, sr), (pi_ref, si)):
            pltpu.store(ref.at[kf, :], val, mask=fwd_lane)
            pltpu.store(ref.at[kb, :], val, mask=bwd_lane)
        xr = jnp.where(fwd_lane, xr_ref[kf, :], xr_ref[kb, :])
        xi = jnp.where(fwd_lane, xi_ref[kf, :], xi_ref[kb, :])
        return lr * sr - li * si + xr, lr * si + li * sr + xi

    if chunk_major:
        def step(t, carry):
            kf = pl.ds(pl.multiple_of(t * bsz, bsz), bsz)
            kb = pl.ds(pl.multiple_of((nk - 1 - t) * bsz, bsz), bsz)
            out = []
            for gi in range(gl):
                ca, cb = pl.ds(gi * sw, 2 * half), pl.ds(gi * sw + 2 * half, 2 * half)
                out.append(advance(*carry[gi], lam[:, gi * sw:gi * sw + 2 * half],
                                   lam[:, gi * sw + 2 * half:(gi + 1) * sw], kf, kb,
                                   x_scr.at[:, ca], x_scr.at[:, cb], p_scr.at[:, ca], p_scr.at[:, cb]))
            return tuple(out)

        init = tuple((s0_ref[:, gi * sw:gi * sw + 2 * half], s0_ref[:, gi * sw + 2 * half:(gi + 1) * sw])
                     for gi in range(gl))
        fin = lax.fori_loop(0, nk, step, init)
        for gi in range(gl):
            fin_ref[:, gi * sw:gi * sw + 2 * half] = fin[gi][0]
            fin_ref[:, gi * sw + 2 * half:(gi + 1) * sw] = fin[gi][1]
    else:
        by_group = lambda row: jnp.concatenate([row[:, gi * sw:(gi + 1) * sw] for gi in range(gl)], axis=0)
        lam8 = by_group(lam)

        def step(t, carry):
            out = []
            for q in range(bsz):
                kf = pl.ds(pl.multiple_of((q * nk + t) * gl, gl), gl)
                kb = pl.ds(pl.multiple_of((q * nk + nk - 1 - t) * gl, gl), gl)
                out.append(advance(*carry[q], lam8[:, 0:2 * half], lam8[:, 2 * half:], kf, kb,
                                   x_scr.at[0], x_scr.at[1], p_scr.at[0], p_scr.at[1]))
            return tuple(out)

        s0 = [by_group(s0_ref[q:q + 1, :]) for q in range(bsz)]
        fin = lax.fori_loop(0, nk, step, tuple((a[:, 0:2 * half], a[:, 2 * half:]) for a in s0), unroll=8)
        for q in range(bsz):
            for gi in range(gl):
                fin_ref[q:q + 1, gi * sw:gi * sw + 2 * half] = fin[q][0][gi:gi + 1]
                fin_ref[q:q + 1, gi * sw + 2 * half:(gi + 1) * sw] = fin[q][1][gi:gi + 1]

    for gi in range(gl):
        if chunk_major:
            pg = p_scr[:, gi * sw:(gi + 1) * sw]
        else:
            pg = jnp.concatenate([p_scr[0, pl.ds(gi, nrows, stride=gl), :], p_scr[1, pl.ds(gi, nrows, stride=gl), :]],
                                 axis=1)
        y = yl_scr[:, gi * tw:(gi + 1) * tw] + jnp.dot(pg.astype(BF16), w2_ref[gi], preferred_element_type=F32)
        u_scr[:, gi * tw:(gi + 1) * tw] = y.astype(BF16)
    yt = jnp.dot(u_scr[...], permt_ref[...], preferred_element_type=F32)
    for t in range(t_):
        for r0, n, tok0, stride in token_rows(t):
            tok_scr[pl.ds(tok0, n, stride=stride), :] = yt[r0:r0 + n, t * lt:(t + 1) * lt]
    y_ref[...] = tok_scr[...].astype(BF16)


def _s5(proj, col0, bsz, l, ops, s0):
    w1, w2, lam = ops
    g = w1.shape[0]
    t, ps, gl, lt = S5_CHUNK, SSM_STATE, S5_TILE_GROUPS, 128
    nb = g // gl
    nk = l // t
    r = bsz * l
    nrows = bsz * nk
    sw = 4 * ps
    cw = t * lt
    tw = t * SSM_GROUP
    chunk_major = bsz % 8 == 0
    state_shape = (nrows, gl * sw) if chunk_major else (2, nrows * gl, sw // 2)
    if s0 is None:
        s0p = jnp.zeros((bsz, g * sw), F32)
    else:
        s0_re, s0_im = s0
        s0p = jnp.concatenate([s0_re[:, 0], s0_re[:, 1], s0_im[:, 0], s0_im[:, 1]], axis=-1).reshape(bsz, g * sw)
    perm = _chunk_lane_permutation()
    const = lambda shape: _resident(shape, lambda i: (0,) * len(shape))
    y, fin = pl.pallas_call(
        functools.partial(_s5_body, bsz, nk, chunk_major),
        grid=(nb,),
        in_specs=[pl.BlockSpec((r, lt), lambda i: (0, col0 // lt + i)),
                  const((cw, cw)), const((cw, cw)),
                  pl.BlockSpec((gl, tw, 3 * tw), lambda i: (i, 0, 0)), pl.BlockSpec((gl, tw, tw), lambda i: (i, 0, 0)),
                  pl.BlockSpec((None, 1, gl * sw), lambda i: (i, 0, 0)),
                  pl.BlockSpec((bsz, gl * sw), lambda i: (0, i))],
        out_specs=[pl.BlockSpec((r, lt), lambda i: (0, i)), pl.BlockSpec((bsz, gl * sw), lambda i: (0, i))],
        out_shape=[jax.ShapeDtypeStruct((r, g * SSM_GROUP), BF16), jax.ShapeDtypeStruct((bsz, g * sw), F32)],
        scratch_shapes=[pltpu.VMEM((r, lt), F32), pltpu.VMEM((nrows, cw), BF16), pltpu.VMEM((nrows, cw), BF16),
                        pltpu.VMEM((nrows, cw), F32)] + [pltpu.VMEM(state_shape, F32)] * 2,
        compiler_params=_cparams(("parallel",)),
        name=f"s5_scan_{nk}",
    )(proj, jnp.asarray(perm, BF16), jnp.asarray(perm.T, BF16), w1, w2, lam, s0p)
    fin = jnp.transpose(fin.reshape(bsz, g, 2, 2, ps), (2, 0, 3, 1, 4))
    return y, fin[0], fin[1]


def _merge_body(yh_ref, ys_ref, u_ref, gh_ref, gs_ref, x_ref, d_ref, gw_ref, gbias_ref, wbh_ref, wbs_ref, wo_ref,
                g1_ref, n2_ref, sc2_ref, sh2_ref, x1_ref, h2_ref):
    y = ys_ref[...].astype(F32) + u_ref[...].astype(F32) * d_ref[...]
    g = jax.nn.gelu(y)
    gl = jnp.dot(g.astype(BF16), gw_ref[...], preferred_element_type=F32) + gbias_ref[...]
    ys = (g * jax.nn.sigmoid(gl)).astype(BF16)
    bh = jnp.dot(yh_ref[...], wbh_ref[...], preferred_element_type=F32)
    bs = jnp.dot(ys, wbs_ref[...], preferred_element_type=F32)
    merged = jax.nn.sigmoid(gh_ref[...].astype(F32)) * bh + jax.nn.sigmoid(gs_ref[...].astype(F32)) * bs
    x1 = x_ref[...] + g1_ref[...] * jnp.dot(merged.astype(BF16), wo_ref[...], preferred_element_type=F32)
    x1_ref[...] = x1
    h2_ref[...] = _mod_rmsnorm(x1, n2_ref[...], sc2_ref[...], sh2_ref[...]).astype(BF16)


def _merge(yh, ys, proj, x, mod4, cond_of_tile, ssm_d, glu_w, glu_b, wbh, wbs, wo, n2g, c_h, c_s, tm=256):
    r, d = x.shape
    row_blk = lambda width, k: pl.BlockSpec((tm, width), lambda i: (i, k))
    mod_spec = lambda chunk: pl.BlockSpec((None, None, 1, d), lambda i: (cond_of_tile(i, tm), chunk, 0, 0))
    const = lambda shape: _resident(shape, lambda i: (0,) * len(shape))
    u_col = (3 * c_h) // c_s
    gate0 = (3 * c_h + c_s) // d
    return pl.pallas_call(
        _merge_body,
        grid=(r // tm,),
        in_specs=[row_blk(c_h, 0), row_blk(c_s, 0), row_blk(c_s, u_col), row_blk(d, gate0), row_blk(d, gate0 + 1),
                  row_blk(d, 0), const((1, c_s)), const((c_s, c_s)), const((1, c_s)), const((c_h, d)),
                  const((c_s, d)), const((d, d)), mod_spec(2), const((1, d)), mod_spec(4), mod_spec(3)],
        out_specs=[row_blk(d, 0), row_blk(d, 0)],
        out_shape=[jax.ShapeDtypeStruct((r, d), F32), jax.ShapeDtypeStruct((r, d), BF16)],
        compiler_params=_cparams(("parallel",)),
        name="merge",
    )(yh, ys, proj, proj, proj, x, ssm_d.reshape(1, c_s), glu_w, glu_b.reshape(1, c_s), wbh, wbs, wo,
      mod4, n2g.reshape(1, d), mod4, mod4)


_FFN_RC = 512


def _ffn_up_body(rows, cols, h_ref, wa_ref, wb_ref, cwa_ref, cwb_ref, cba_ref, cbb_ref, o_ref):
    tm, tn = o_ref.shape
    rc = _FFN_RC
    n = tm // rc
    col = lax.rem(lax.broadcasted_iota(jnp.int32, (rc, tn), 0), cols)
    not_first = col != 0
    not_last = col != cols - 1
    tdt = BF16 if rows > 1 else F32
    zero = jnp.zeros((cols, tn), tdt)

    def taps(hc, w_ref, cw_ref, cb_ref):
        a = jnp.dot(hc, w_ref[...], preferred_element_type=F32)
        am = jnp.where(not_first, pltpu.roll(a, 1, 0), 0.0).astype(tdt)
        ap = jnp.where(not_last, pltpu.roll(a, rc - 1, 0), 0.0).astype(tdt)
        a = a.astype(tdt)
        cw = cw_ref[...].astype(tdt)
        tap = lambda i: am * cw[3 * i:3 * i + 1] + a * cw[3 * i + 1:3 * i + 2] + ap * cw[3 * i + 2:3 * i + 3]
        mid = tap(1) + cb_ref[...].astype(tdt)
        return (tap(0), mid, tap(2)) if rows > 1 else (None, mid, None)

    def finish(prev, cur, nxt):
        t0, mid, t2 = cur
        if rows == 1:
            return mid
        above = jnp.concatenate([prev[0][rc - cols:] if prev is not None else zero, t0[:rc - cols]], axis=0)
        below = jnp.concatenate([t2[cols:], nxt[2][:cols] if nxt is not None else zero], axis=0)
        return mid + above + below

    chunks = []
    for i in range(n + 1):
        if i < n:
            hc = h_ref[i * rc:(i + 1) * rc, :]
            chunks.append((taps(hc, wa_ref, cwa_ref, cba_ref), taps(hc, wb_ref, cwb_ref, cbb_ref)))
        if i >= 1:
            j = i - 1
            pick = lambda k, side: chunks[k][side] if 0 <= k < n else None
            a = finish(pick(j - 1, 0), pick(j, 0), pick(j + 1, 0))
            b = finish(pick(j - 1, 1), pick(j, 1), pick(j + 1, 1))
            o_ref[j * rc:(j + 1) * rc, :] = (jax.nn.gelu(a.astype(F32)) * b.astype(F32)).astype(BF16)


def _ffn_up(h2, w_up, conv_w, conv_b, rows, cols, tn=256):
    r, d = h2.shape
    f = w_up.shape[1] // 2
    tm = rows * cols if rows > 1 else 4096
    assert tm % _FFN_RC == 0 and _FFN_RC % cols == 0 and cols % 8 == 0
    nt = f // tn
    cw = conv_w.reshape(9, 2 * f)
    cb = conv_b.reshape(1, 2 * f)
    return pl.pallas_call(
        functools.partial(_ffn_up_body, rows, cols),
        grid=(r // tm, nt),
        in_specs=[pl.BlockSpec((tm, d), lambda i, j: (i, 0)),
                  pl.BlockSpec((d, tn), lambda i, j: (0, j)), pl.BlockSpec((d, tn), lambda i, j: (0, nt + j)),
                  pl.BlockSpec((9, tn), lambda i, j: (0, j)), pl.BlockSpec((9, tn), lambda i, j: (0, nt + j)),
                  pl.BlockSpec((1, tn), lambda i, j: (0, j)), pl.BlockSpec((1, tn), lambda i, j: (0, nt + j))],
        out_specs=pl.BlockSpec((tm, tn), lambda i, j: (i, j)),
        out_shape=jax.ShapeDtypeStruct((r, f), BF16),
        compiler_params=_cparams(("parallel", "arbitrary")),
        name=f"ffn_up_{rows}x{cols}",
    )(h2, w_up, w_up, cw, cw, cb, cb)


def _ffn_down_body(a_ref, w_ref, x_ref, g2_ref, fg_ref, o_ref):
    x2 = x_ref[...] + g2_ref[...] * jnp.dot(a_ref[...], w_ref[...], preferred_element_type=F32)
    ms = jnp.mean(x2 * x2, axis=-1, keepdims=True)
    o_ref[...] = x2 * lax.rsqrt(ms + EPS) * fg_ref[...]


def _ffn_down(act, w, x1, mod4, cond_of_tile, final_g, tm=256):
    r, f = act.shape
    d = w.shape[1]
    return pl.pallas_call(
        _ffn_down_body,
        grid=(r // tm,),
        in_specs=[pl.BlockSpec((tm, f), lambda i: (i, 0)),
                  _resident((f, d), lambda i: (0, 0)),
                  pl.BlockSpec((tm, d), lambda i: (i, 0)),
                  pl.BlockSpec((None, None, 1, d), lambda i: (cond_of_tile(i, tm), 5, 0, 0)),
                  _resident((1, d), lambda i: (0, 0))],
        out_specs=pl.BlockSpec((tm, d), lambda i: (i, 0)),
        out_shape=jax.ShapeDtypeStruct((r, d), F32),
        compiler_params=_cparams(("parallel",)),
        name="ffn_down",
    )(act, w, x1, mod4, final_g.reshape(1, d))


def _segment(x, cond_base, per_batch_cond, s0, rows, cols, hy_block, hy_nseq, mod4, p, final_g):
    bsz, l, d = x.shape
    xf = x.reshape(bsz * l, d)
    c_h = p["wbh"].shape[0]
    c_s = p["wbs"].shape[0]

    def cond_of_tile(i, tm):
        return cond_base + (i * tm) // l if per_batch_cond else cond_base

    proj = _in_proj(xf, mod4, cond_of_tile, p["norm1_g"], p["w_in"], p["b_in"])
    yh = _hyena(proj, l, hy_block, hy_nseq, p["hy_short_w"], p["hy_short_b"], p["hy_bias"], p["ghat"][l], c_h,
                ct=128 if l // hy_block > 1 else 256)
    ys, fin_re, fin_im = _s5(proj, 3 * c_h, bsz, l, p["s5_ops"], s0)
    x1, h2 = _merge(yh, ys, proj, xf, mod4, cond_of_tile, p["ssm_d"], p["glu_w"], p["glu_b"],
                    p["wbh"], p["wbs"], p["wo"], p["norm2_g"], c_h, c_s)
    act = _ffn_up(h2, p["ffn_up"], p["ffn_conv_w"], p["ffn_conv_b"], rows, cols)
    y = _ffn_down(act, p["ffn_down"], x1, mod4, cond_of_tile, final_g)
    return y.reshape(bsz, l, d), fin_re, fin_im


def kernel(x_prompt, x_sample, state_ssm_re, state_ssm_im, c, c_ctx, ada_w, ada_b, norm1_g, norm2_g, final_g, w_in, b_in, hy_short_w, hy_short_b, hy_f_w1, hy_f_b1, hy_f_w2, hy_f_b2, hy_f_w3, hy_f_b3, hy_f_w4, hy_f_freq, hy_bias, w_branch_h, ssm_lambda_re, ssm_lambda_im, ssm_log_dt, ssm_b_re, ssm_b_im, ssm_c_re, ssm_c_im, ssm_d, ssm_glu_w, ssm_glu_b, w_branch_s, w_out, ffn_up, ffn_conv_w, ffn_conv_b, ffn_down):
    depth = ada_w.shape[0]
    assert depth == 1, "the final norm is fused into the (single) layer's last kernel"
    d = x_prompt.shape[-1]
    ctx_len = x_prompt.shape[1]
    lat_len = x_sample.shape[1]
    dec_b = x_sample.shape[0]
    c_h = w_branch_h.shape[1]
    assert 1 + dec_b <= 8
    cond = jnp.concatenate([c_ctx[None], c, jnp.zeros((8 - 1 - dec_b, d), F32)], axis=0)
    lay = 0
    mod4 = _ada(cond, ada_w[lay], ada_b[lay]).reshape(8, N_MOD, 1, d)
    hy_blocks = {ctx_len: 256, lat_len: 512}
    ghat = {l: _hyena_filters(l, hy_blocks[l], hy_f_w1[lay], hy_f_b1[lay], hy_f_w2[lay], hy_f_b2[lay],
                              hy_f_w3[lay], hy_f_b3[lay], hy_f_w4[lay], hy_f_freq[lay], c_h)
            for l in (ctx_len, lat_len)}
    s5_ops = _s5_operators(ssm_lambda_re[lay], ssm_lambda_im[lay], ssm_log_dt[lay],
                           ssm_b_re[lay], ssm_b_im[lay], ssm_c_re[lay], ssm_c_im[lay])
    p = dict(norm1_g=norm1_g[lay], norm2_g=norm2_g[lay], w_in=w_in[lay].astype(BF16), b_in=b_in[lay],
             hy_short_w=hy_short_w[lay], hy_short_b=hy_short_b[lay], hy_bias=hy_bias[lay], ghat=ghat,
             s5_ops=s5_ops, ssm_d=ssm_d[lay], glu_w=ssm_glu_w[lay].astype(BF16),
             glu_b=ssm_glu_b[lay], wbh=w_branch_h[lay].astype(BF16), wbs=w_branch_s[lay].astype(BF16),
             wo=w_out[lay].astype(BF16), ffn_up=ffn_up[lay].astype(BF16), ffn_conv_w=ffn_conv_w[lay],
             ffn_conv_b=ffn_conv_b[lay], ffn_down=ffn_down[lay].astype(BF16))
    y_prompt, st_re, st_im = _segment(x_prompt, 0, False, None, 1, ctx_len, hy_blocks[ctx_len], 16, mod4, p, final_g)
    s0 = (state_ssm_re[:, lay].astype(F32), state_ssm_im[:, lay].astype(F32))
    y_sample, _, _ = _segment(x_sample, 1, True, s0, lat_len // GRID_W, GRID_W, hy_blocks[lat_len], 1, mod4, p, final_g)
    return y_prompt, y_sample, st_re[:, None], st_im[:, None]
```
